```python
import jax
import jax.numpy as jnp
from jax import lax
import numpy as np

D_MODEL = 1024
BATCH = 16
SEQ = 2048
DEPTH = 4
DEC_BATCH = 128
DEC_SEQ = 4
PAST_LEN = 8192
PAGE_SIZE = 128

N_MIXERS = 4
N_LAYERS_A = (DEPTH + 3) // 4
N_LAYERS_B = (DEPTH + 2) // 4
N_LAYERS_C = (DEPTH + 1) // 4
N_LAYERS_D = DEPTH // 4
D_FF = 2816
RMS_EPS = 1e-6
NEG = -1e30
Q_BLOCK = 128

ML_HEADS = 8
ML_DQK = 64
ML_DV = 128
ML_CHUNK = 64
ML_GATE_CAP = 15.0
ML_IN = 2 * ML_HEADS * ML_DQK + 2 * ML_HEADS * ML_DV + 2 * ML_HEADS

MLA_HEADS = 16
MLA_NOPE = 64
MLA_ROPE = 32
MLA_VD = 64
MLA_KV_LORA = 256
MLA_Q_LORA = 384
MLA_LAT = MLA_KV_LORA + MLA_ROPE
MLA_IN = MLA_Q_LORA + MLA_KV_LORA + MLA_ROPE
MLA_SCALE = (MLA_NOPE + MLA_ROPE) ** -0.5
ROPE_THETA = 10000.0

NSA_HEADS = 16
NSA_GROUPS = 2
NSA_HPG = NSA_HEADS // NSA_GROUPS
NSA_HD = 64
NSA_BLOCK = 64
NSA_TOPK = 15
NSA_WINDOW = 512
NSA_QB = 32
NSA_IN = NSA_HEADS * NSA_HD + 6 * NSA_GROUPS * NSA_HD + 3 * NSA_HEADS
NSA_SCALE = NSA_HD ** -0.5

GDN_HEADS = 8
GDN_DK = 128
GDN_DV = 128
GDN_CONV = 4
GDN_CHUNK = 64
GDN_CONV_DIM = GDN_HEADS * (2 * GDN_DK + GDN_DV)
GDN_IN = GDN_CONV_DIM + 2 * GDN_HEADS + GDN_HEADS * GDN_DV

kernel_name = 'hybrid_mlstm_mla_nsa_gdn_step'


def rmsnorm(x, g):
    xf = x.astype(jnp.float32)
    y = xf * lax.rsqrt(jnp.mean(xf * xf, axis=-1, keepdims=True) + RMS_EPS)
    return (y * g.astype(jnp.float32)).astype(x.dtype)


def l2norm(x):
    xf = x.astype(jnp.float32)
    return xf * lax.rsqrt(jnp.sum(xf * xf, axis=-1, keepdims=True) + 1e-6)


def rope(x, pos):
    half = x.shape[-1] // 2
    freqs = ROPE_THETA ** (-jnp.arange(half, dtype=jnp.float32) / half)
    ang = pos[:, None] * freqs[None, :]
    ang = ang.reshape((ang.shape[0],) + (1,) * (x.ndim - 3) + (half,))
    cos, sin = jnp.cos(ang), jnp.sin(ang)
    xf = x.astype(jnp.float32)
    x1, x2 = xf[..., :half], xf[..., half:]
    return jnp.concatenate([x1 * cos - x2 * sin, x2 * cos + x1 * sin], axis=-1).astype(x.dtype)


def masked_softmax(s, mask):
    return jax.nn.softmax(jnp.where(mask, s.astype(jnp.float32), NEG), axis=-1) * mask


def adaln(c, w, b):
    return (jax.nn.silu(c) @ w + b).reshape(c.shape[0], 3, 3, c.shape[1])


def modulate(x, g, mod, j):
    return rmsnorm(x, g) * (1.0 + mod[:, j, 1][:, None]) + mod[:, j, 0][:, None]


def swiglu(h, w_up, w_down):
    a, b = jnp.split(h @ w_up, 2, axis=-1)
    return (jax.nn.silu(a) * b) @ w_down


def ffn_half(x, g, mod, j, w_up, w_down):
    return x + 0.5 * mod[:, j, 2][:, None] * swiglu(modulate(x, g, mod, j), w_up, w_down)


def _chunks(a, chunk):
    b, h, t = a.shape[:3]
    return jnp.moveaxis(a.reshape((b, h, t // chunk, chunk) + a.shape[3:]), 2, 0)


def _unchunk(a):
    a = jnp.moveaxis(a, 0, 2)
    return a.reshape(a.shape[:2] + (a.shape[2] * a.shape[3],) + a.shape[4:])


def _stack(entries, j):
    return jnp.stack([e[j] for e in entries])


def mlstm_scan(q, k, v, ig, lf, C0, n0, m0, chunk):
    causal = jnp.tril(jnp.ones((chunk, chunk), bool))

    def step(carry, inp):
        C, n, m = carry
        qc, kc, vc, ic, fc = inp
        b = jnp.cumsum(fc, axis=-1)
        dmat = jnp.where(causal, b[..., :, None] - b[..., None, :] + ic[..., None, :], NEG)
        inter = b + m[..., None]
        mt = jnp.maximum(inter, jnp.max(dmat, axis=-1))
        w_intra = jnp.exp(dmat - mt[..., None])
        w_inter = jnp.exp(inter - mt)
        s = jnp.einsum('bhld,bhsd->bhls', qc, kc) * w_intra
        num = w_inter[..., None] * jnp.einsum('bhld,bhdv->bhlv', qc, C) + jnp.einsum('bhls,bhsv->bhlv', s, vc)
        den = w_inter * jnp.einsum('bhld,bhd->bhl', qc, n) + jnp.sum(s, axis=-1)
        hc = num / jnp.maximum(jnp.abs(den), jnp.exp(-mt))[..., None]
        m_new = mt[..., -1]
        w_state = jnp.exp(b[..., -1:] - b + ic - m_new[..., None])
        decay = jnp.exp(b[..., -1] + m - m_new)
        C_new = decay[..., None, None] * C + jnp.einsum('bhs,bhsd,bhsv->bhdv', w_state, kc, vc)
        n_new = decay[..., None] * n + jnp.einsum('bhs,bhsd->bhd', w_state, kc)
        return (C_new, n_new, m_new), hc

    xs = tuple(_chunks(a, chunk) for a in (q, k, v, ig, lf))
    (C, n, m), h = lax.scan(step, (C0, n0, m0), xs)
    return _unchunk(h), C, n, m


def mlstm_mixer(h, w_in, b_gate, g_hn, w_out, C0, n0, m0):
    b, t, _ = h.shape
    qk_w, v_w = ML_HEADS * ML_DQK, ML_HEADS * ML_DV
    q, k, v, o, gates = jnp.split(h @ w_in, [qk_w, 2 * qk_w, 2 * qk_w + v_w, 2 * qk_w + 2 * v_w], axis=-1)

    def heads(a, d):
        return a.reshape(b, t, ML_HEADS, d).transpose(0, 2, 1, 3).astype(jnp.float32)

    gates = ML_GATE_CAP * jnp.tanh((gates + b_gate).astype(jnp.float32) / ML_GATE_CAP)
    ig = gates[..., :ML_HEADS].transpose(0, 2, 1)
    lf = jax.nn.log_sigmoid(gates[..., ML_HEADS:]).transpose(0, 2, 1)
    hh, C, n, m = mlstm_scan(heads(q, ML_DQK) * ML_DQK ** -0.5, heads(k, ML_DQK), heads(v, ML_DV), ig, lf,
                             C0.astype(jnp.float32), n0.astype(jnp.float32), m0.astype(jnp.float32),
                             min(ML_CHUNK, t))
    hh = rmsnorm(hh.transpose(0, 2, 1, 3), g_hn).astype(h.dtype).reshape(b, t, v_w)
    return (jax.nn.sigmoid(o) * hh) @ w_out, C.astype(h.dtype), n.astype(h.dtype), m.astype(h.dtype)


def gdn_scan(q, k, v, beta, g, S0, chunk):
    causal = jnp.tril(jnp.ones((chunk, chunk), bool))
    strict = jnp.tril(jnp.ones((chunk, chunk), bool), -1)

    def step(S, inp):
        qc, kc, vc, bc, gc = inp
        b = jnp.cumsum(gc, axis=-1)
        decay = jnp.exp(jnp.where(causal, b[..., :, None] - b[..., None, :], NEG))
        a_mat = jnp.where(strict, bc[..., :, None] * jnp.einsum('bhtd,bhsd->bhts', kc, kc) * decay, 0.0)
        eb = jnp.exp(b)
        rhs = bc[..., None] * (vc - eb[..., None] * jnp.einsum('bhtd,bhdv->bhtv', kc, S))
        u = lax.linalg.triangular_solve(a_mat, rhs, left_side=True, lower=True, unit_diagonal=True)
        qk = jnp.einsum('bhtd,bhsd->bhts', qc, kc) * decay
        o = eb[..., None] * jnp.einsum('bhtd,bhdv->bhtv', qc, S) + jnp.einsum('bhts,bhsv->bhtv', qk, u)
        w_state = jnp.exp(b[..., -1:] - b)
        S_new = eb[..., -1][..., None, None] * S + jnp.einsum('bhs,bhsd,bhsv->bhdv', w_state, kc, u)
        return S_new, o

    xs = tuple(_chunks(a, chunk) for a in (q, k, v, beta, g))
    S, o = lax.scan(step, S0, xs)
    return _unchunk(o), S


def gdn_mixer(h, w_in, w_conv, a_log, dt_bias, g_out, w_out, S0, conv0):
    b, t, _ = h.shape
    qkv, beta, a, gate = jnp.split(h @ w_in, [GDN_CONV_DIM, GDN_CONV_DIM + GDN_HEADS, GDN_CONV_DIM + 2 * GDN_HEADS], axis=-1)
    xc = jnp.concatenate([conv0.astype(qkv.dtype), qkv], axis=1)
    conv = lax.conv_general_dilated(xc, w_conv[:, None, :].astype(xc.dtype), (1,), 'VALID',
                                    dimension_numbers=('NWC', 'WIO', 'NWC'), feature_group_count=GDN_CONV_DIM)
    q, k, v = jnp.split(jax.nn.silu(conv), [GDN_HEADS * GDN_DK, 2 * GDN_HEADS * GDN_DK], axis=-1)
    q = l2norm(q.reshape(b, t, GDN_HEADS, GDN_DK)).transpose(0, 2, 1, 3) * GDN_DK ** -0.5
    k = l2norm(k.reshape(b, t, GDN_HEADS, GDN_DK)).transpose(0, 2, 1, 3)
    v = v.reshape(b, t, GDN_HEADS, GDN_DV).transpose(0, 2, 1, 3).astype(jnp.float32)
    beta = jax.nn.sigmoid(beta.astype(jnp.float32)).transpose(0, 2, 1)
    g = (-jnp.exp(a_log.astype(jnp.float32)) * jax.nn.softplus((a + dt_bias).astype(jnp.float32))).transpose(0, 2, 1)
    o, S = gdn_scan(q, k, v, beta, g, S0.astype(jnp.float32), min(GDN_CHUNK, t))
    o = rmsnorm(o.transpose(0, 2, 1, 3), g_out).astype(h.dtype).reshape(b, t, GDN_HEADS * GDN_DV)
    return (o * jax.nn.silu(gate)) @ w_out, S.astype(h.dtype), xc[:, -(GDN_CONV - 1):]


def mla_project(h, pos, w_in, g_qa, g_kva, w_uq, g_qk):
    b, t, _ = h.shape
    cq, ckv, kr = jnp.split(h @ w_in, [MLA_Q_LORA, MLA_Q_LORA + MLA_KV_LORA], axis=-1)
    q = (rmsnorm(cq, g_qa) @ w_uq).reshape(b, t, MLA_HEADS, MLA_NOPE + MLA_ROPE)
    q_nope = rmsnorm(q[..., :MLA_NOPE], g_qk[0, :MLA_NOPE])
    q_rope = rope(rmsnorm(q[..., MLA_NOPE:], g_qk[0, MLA_NOPE:]), pos)
    k_rope = rope(rmsnorm(kr, g_qk[1, MLA_NOPE:]), pos)
    latent = jnp.concatenate([rmsnorm(ckv, g_kva), k_rope], axis=-1)
    return q_nope, q_rope, latent


def mla_scores(q_nope, q_rope, k_nope, k_rope):
    s = jnp.einsum('bthd,bshd->bhts', q_nope, k_nope) + jnp.einsum('bthr,bsr->bhts', q_rope, k_rope)
    return s.astype(jnp.float32) * MLA_SCALE


def mla_keys(latent, w_uk, g_kn):
    ckv = latent[..., :MLA_KV_LORA]
    return ckv, rmsnorm(jnp.einsum('bsc,chd->bshd', ckv, w_uk), g_kn), latent[..., MLA_KV_LORA:]


def mla_prompt(h, w_in, g_qa, g_kva, w_uq, w_uk, w_uv, g_qk, w_out):
    b, t, _ = h.shape
    q_nope, q_rope, latent = mla_project(h, jnp.arange(t, dtype=jnp.float32), w_in, g_qa, g_kva, w_uq, g_qk)
    ckv, k_nope, k_rope = mla_keys(latent, w_uk, g_qk[1, :MLA_NOPE])
    v = jnp.einsum('bsc,chd->bshd', ckv, w_uv)
    kpos = jnp.arange(t)

    def block(i):
        t0 = i * Q_BLOCK
        qn = lax.dynamic_slice_in_dim(q_nope, t0, Q_BLOCK, axis=1)
        qr = lax.dynamic_slice_in_dim(q_rope, t0, Q_BLOCK, axis=1)
        mask = kpos[None, :] <= (t0 + jnp.arange(Q_BLOCK))[:, None]
        p = masked_softmax(mla_scores(qn, qr, k_nope, k_rope), mask)
        return jnp.einsum('bhts,bshd->bthd', p.astype(v.dtype), v)

    o = jnp.moveaxis(lax.map(block, jnp.arange(t // Q_BLOCK)), 0, 1).reshape(b, t, MLA_HEADS * MLA_VD)
    return o @ w_out, latent


def mla_sample(h, pool, li, page_table, w_in, g_qa, g_kva, w_uq, w_uk, w_uv, g_qk, w_out):
    b, t, _ = h.shape
    q_nope, q_rope, latent = mla_project(h, PAST_LEN + jnp.arange(t, dtype=jnp.float32), w_in, g_qa, g_kva, w_uq, g_qk)
    g_kn = g_qk[1, :MLA_NOPE]

    def update(carry, s, ckv):
        m, l, acc = carry
        m_new = jnp.maximum(m, jnp.max(s, axis=-1))
        corr = jnp.exp(m - m_new)
        p = jnp.exp(s - m_new[..., None])
        return (m_new, l * corr + jnp.sum(p, axis=-1),
                acc * corr[..., None] + jnp.einsum('bhts,bsc->bhtc', p, ckv.astype(jnp.float32)))

    def page_step(carry, p):
        ckv, k_nope, k_rope = mla_keys(pool[li, page_table[:, p]], w_uk, g_kn)
        return update(carry, mla_scores(q_nope, q_rope, k_nope, k_rope), ckv), None

    init = (jnp.full((b, MLA_HEADS, t), NEG, jnp.float32), jnp.zeros((b, MLA_HEADS, t), jnp.float32),
            jnp.zeros((b, MLA_HEADS, t, MLA_KV_LORA), jnp.float32))
    carry, _ = lax.scan(page_step, init, jnp.arange(page_table.shape[1]))
    ckv, k_nope, k_rope = mla_keys(latent, w_uk, g_kn)
    s_new = jnp.where(jnp.tril(jnp.ones((t, t), bool)), mla_scores(q_nope, q_rope, k_nope, k_rope), NEG)
    _, l, acc = update(carry, s_new, ckv)
    o = jnp.einsum('bhtc,chd->bthd', acc / l[..., None], w_uv.astype(jnp.float32))
    return o.astype(h.dtype).reshape(b, t, MLA_HEADS * MLA_VD) @ w_out, latent


def nsa_project(h, w_in, b_gate, g_qk):
    b, t, _ = h.shape
    qw, kvw = NSA_HEADS * NSA_HD, 6 * NSA_GROUPS * NSA_HD
    q, kv, gates = jnp.split(h @ w_in, [qw, qw + kvw], axis=-1)
    q = rmsnorm(q.reshape(b, t, NSA_GROUPS, NSA_HPG, NSA_HD), g_qk[0])
    kv = kv.reshape(b, t, 6, NSA_GROUPS, NSA_HD)
    rows = jnp.stack([kv[:, :, 0], kv[:, :, 1], rmsnorm(kv[:, :, 2], g_qk[1]), kv[:, :, 3]], axis=2)
    win = jnp.stack([rmsnorm(kv[:, :, 4], g_qk[1]), kv[:, :, 5]], axis=2)
    gates = jax.nn.sigmoid(gates + b_gate).reshape(b, t, NSA_GROUPS, NSA_HPG, 3)
    return q, rows, win, gates


def nsa_compress(kv_rows, w_cmp, g_k):
    b, l = kv_rows.shape[:2]
    blk = kv_rows.reshape(b, l // NSA_BLOCK, NSA_BLOCK, 2, NSA_GROUPS, NSA_HD)
    c = jnp.einsum('bnpcgd,cpde->bncge', blk, w_cmp)
    return rmsnorm(c[:, :, 0], g_k), c[:, :, 1]


def nsa_cmp_attn(q, k_c, v_c, qpos):
    nb = k_c.shape[1]
    s = jnp.einsum('btghd,bngd->bghtn', q, k_c) * NSA_SCALE
    mask = ((jnp.arange(nb) + 1) * NSA_BLOCK - 1)[None, :] <= qpos[:, None]
    p = masked_softmax(s, mask)
    o = jnp.einsum('bghtn,bngd->btghd', p.astype(v_c.dtype), v_c)
    return o, jnp.sum(p, axis=2)


def nsa_select(imp, qpos):
    nb = imp.shape[-1]
    cand = jnp.arange(nb)[None, :] < (qpos // NSA_BLOCK)[:, None]
    val, idx = lax.top_k(jnp.where(cand, imp, -jnp.inf), min(NSA_TOPK, nb))
    return idx, jnp.isfinite(val)


def nsa_sel_prompt(q, k_s, v_s, idx, valid):
    b, t = q.shape[:2]
    nb = t // NSA_BLOCK
    kb = k_s.reshape(b, nb, NSA_BLOCK, NSA_GROUPS, NSA_HD).transpose(0, 3, 1, 2, 4)
    vb = v_s.reshape(b, nb, NSA_BLOCK, NSA_GROUPS, NSA_HD).transpose(0, 3, 1, 2, 4)
    bi = jnp.arange(b)[:, None, None, None]
    gi = jnp.arange(NSA_GROUPS)[None, None, :, None]
    idx, valid = idx.transpose(0, 2, 1, 3), valid.transpose(0, 2, 1, 3)

    def block(i):
        t0 = i * NSA_QB
        qpos = t0 + jnp.arange(NSA_QB)
        qq = lax.dynamic_slice_in_dim(q, t0, NSA_QB, axis=1)
        cur = jnp.broadcast_to((qpos // NSA_BLOCK)[None, :, None, None], (b, NSA_QB, NSA_GROUPS, 1))
        ii = jnp.concatenate([lax.dynamic_slice_in_dim(idx, t0, NSA_QB, axis=1), cur], axis=-1)
        ok = jnp.concatenate([lax.dynamic_slice_in_dim(valid, t0, NSA_QB, axis=1), jnp.ones(cur.shape, bool)], axis=-1)
        kpos = ii[..., None] * NSA_BLOCK + jnp.arange(NSA_BLOCK)
        mask = ok[..., None] & (kpos <= qpos[None, :, None, None, None])
        n = ii.shape[-1] * NSA_BLOCK
        ks, vs = kb[bi, gi, ii], vb[bi, gi, ii]
        s = jnp.einsum('btghd,btgkpd->bghtkp', qq, ks).reshape(b, NSA_GROUPS, NSA_HPG, NSA_QB, n) * NSA_SCALE
        mask = mask.transpose(0, 2, 1, 3, 4).reshape(b, NSA_GROUPS, 1, NSA_QB, n)
        p = masked_softmax(s, mask)
        return jnp.einsum('bghtn,btgnd->btghd', p.astype(vs.dtype), vs.reshape(b, NSA_QB, NSA_GROUPS, n, NSA_HD))

    return jnp.moveaxis(lax.map(block, jnp.arange(t // NSA_QB)), 0, 1).reshape(q.shape)


def nsa_sel_sample(q, k_new, v_new, idx, valid, pool, li, page_table):
    b, t = q.shape[:2]
    bpp = PAGE_SIZE // NSA_BLOCK
    kk = idx.shape[-1]
    n = kk * NSA_BLOCK
    ii, ok = idx.transpose(0, 2, 1, 3), valid.transpose(0, 2, 1, 3)
    phys = page_table[jnp.arange(b)[:, None, None, None], ii // bpp][..., None]
    rows = (ii % bpp)[..., None] * NSA_BLOCK + jnp.arange(NSA_BLOCK)
    gi = jnp.arange(NSA_GROUPS)[None, None, :, None, None]
    ks = pool[li, phys, rows, 2, gi].reshape(b, t, NSA_GROUPS, n, NSA_HD)
    vs = pool[li, phys, rows, 3, gi].reshape(b, t, NSA_GROUPS, n, NSA_HD)
    s = jnp.concatenate([jnp.einsum('btghd,btgnd->bghtn', q, ks), jnp.einsum('btghd,bsgd->bghts', q, k_new)], axis=-1)
    mask_past = jnp.repeat(ok, NSA_BLOCK, axis=-1).transpose(0, 2, 1, 3)[:, :, None]
    mask_new = jnp.broadcast_to(jnp.tril(jnp.ones((t, t), bool)), (b, NSA_GROUPS, 1, t, t))
    p = masked_softmax(s * NSA_SCALE, jnp.concatenate([mask_past, mask_new], axis=-1)).astype(v_new.dtype)
    return (jnp.einsum('bghtn,btgnd->btghd', p[..., :n], vs)
            + jnp.einsum('bghts,bsgd->btghd', p[..., n:], v_new))


def nsa_win_prompt(q, k_w, v_w):
    t = q.shape[1]
    pad = ((0, 0), (NSA_WINDOW, 0), (0, 0), (0, 0))
    kp, vp = jnp.pad(k_w, pad), jnp.pad(v_w, pad)
    span = NSA_WINDOW + Q_BLOCK

    def block(i):
        t0 = i * Q_BLOCK
        qq = lax.dynamic_slice_in_dim(q, t0, Q_BLOCK, axis=1)
        kk = lax.dynamic_slice_in_dim(kp, t0, span, axis=1)
        vv = lax.dynamic_slice_in_dim(vp, t0, span, axis=1)
        qpos = t0 + jnp.arange(Q_BLOCK)
        kpos = t0 - NSA_WINDOW + jnp.arange(span)
        mask = ((kpos[None, :] <= qpos[:, None]) & (kpos[None, :] > qpos[:, None] - NSA_WINDOW)
                & (kpos[None, :] >= 0))
        p = masked_softmax(jnp.einsum('bqghd,bkgd->bghqk', qq, kk) * NSA_SCALE, mask)
        return jnp.einsum('bghqk,bkgd->bqghd', p.astype(vv.dtype), vv)

    return jnp.moveaxis(lax.map(block, jnp.arange(t // Q_BLOCK)), 0, 1).reshape(q.shape)


def nsa_win_sample(q, win, buf):
    t = q.shape[1]
    wb = buf.shape[1]
    kv = jnp.concatenate([buf.astype(win.dtype), win], axis=1)
    qpos = PAST_LEN + jnp.arange(t)
    kpos = PAST_LEN - wb + jnp.arange(wb + t)
    mask = (kpos[None, :] <= qpos[:, None]) & (kpos[None, :] > qpos[:, None] - NSA_WINDOW)
    p = masked_softmax(jnp.einsum('btghd,bsgd->bghts', q, kv[:, :, 0]) * NSA_SCALE, mask)
    o = jnp.einsum('bghts,bsgd->btghd', p.astype(kv.dtype), kv[:, :, 1])
    return o, kv[:, -wb:]


def nsa_merge(gates, o_cmp, o_sel, o_win):
    o = gates[..., 0:1] * o_cmp + gates[..., 1:2] * o_sel + gates[..., 2:3] * o_win
    return o.reshape(o.shape[0], o.shape[1], NSA_HEADS * NSA_HD)


def nsa_prompt(h, w_in, b_gate, g_qk, w_cmp, w_out):
    t = h.shape[1]
    q, rows, win, gates = nsa_project(h, w_in, b_gate, g_qk)
    qpos = jnp.arange(t)
    k_c, v_c = nsa_compress(rows[:, :, :2], w_cmp, g_qk[1])
    o_cmp, imp = nsa_cmp_attn(q, k_c, v_c, qpos)
    idx, valid = nsa_select(imp, qpos)
    o_sel = nsa_sel_prompt(q, rows[:, :, 2], rows[:, :, 3], idx, valid)
    o_win = nsa_win_prompt(q, win[:, :, 0], win[:, :, 1])
    return nsa_merge(gates, o_cmp, o_sel, o_win) @ w_out, rows, win[:, -min(NSA_WINDOW, t):]


def nsa_sample(h, pool, li, page_table, win_buf, w_in, b_gate, g_qk, w_cmp, w_out):
    b, t, _ = h.shape
    q, rows, win, gates = nsa_project(h, w_in, b_gate, g_qk)
    qpos = PAST_LEN + jnp.arange(t)

    def page_cmp(p):
        return nsa_compress(pool[li, page_table[:, p], :, :2], w_cmp, g_qk[1])

    k_c, v_c = lax.map(page_cmp, jnp.arange(page_table.shape[1]))
    k_c = jnp.moveaxis(k_c, 0, 1).reshape(b, -1, NSA_GROUPS, NSA_HD)
    v_c = jnp.moveaxis(v_c, 0, 1).reshape(b, -1, NSA_GROUPS, NSA_HD)
    o_cmp, imp = nsa_cmp_attn(q, k_c, v_c, qpos)
    idx, valid = nsa_select(imp, qpos)
    o_sel = nsa_sel_sample(q, rows[:, :, 2], rows[:, :, 3], idx, valid, pool, li, page_table)
    o_win, new_buf = nsa_win_sample(q, win, win_buf)
    return nsa_merge(gates, o_cmp, o_sel, o_win) @ w_out, rows, new_buf


def setup_inputs(seed: int = 0) -> dict:
    ks = iter(jax.random.split(jax.random.key(seed), 64))

    def nrm(shape, scale=1.0):
        return jax.random.normal(next(ks), shape, jnp.float32) * scale

    def wgt(shape, fan_in):
        return nrm(shape, fan_in ** -0.5)

    def gain(shape):
        return 1.0 + nrm(shape, 0.1)

    n_pages = PAST_LEN // PAGE_SIZE
    n_used = DEC_BATCH * n_pages
    n_pool = n_used + n_used // 4
    f_bias = jnp.concatenate([jnp.zeros((ML_HEADS,), jnp.float32), jnp.full((ML_HEADS,), 3.0, jnp.float32)])
    return {
        'x_prompt': nrm((BATCH, SEQ, D_MODEL)),
        'x_sample': nrm((DEC_BATCH, DEC_SEQ, D_MODEL)),
        'state_mlstm_C': nrm((N_LAYERS_A, DEC_BATCH, ML_HEADS, ML_DQK, ML_DV), 0.5),
        'state_mlstm_n': nrm((N_LAYERS_A, DEC_BATCH, ML_HEADS, ML_DQK), 0.5),
        'state_mlstm_m': nrm((N_LAYERS_A, DEC_BATCH, ML_HEADS)),
        'cache_mla': nrm((N_LAYERS_B, n_pool, PAGE_SIZE, MLA_LAT)),
        'cache_nsa': nrm((N_LAYERS_C, n_pool, PAGE_SIZE, 4, NSA_GROUPS, NSA_HD)),
        'state_nsa_win': nrm((N_LAYERS_C, DEC_BATCH, min(NSA_WINDOW, PAST_LEN), 2, NSA_GROUPS, NSA_HD)),
        'state_gdn_S': nrm((N_LAYERS_D, DEC_BATCH, GDN_HEADS, GDN_DK, GDN_DV), 0.5),
        'state_gdn_conv': nrm((N_LAYERS_D, DEC_BATCH, GDN_CONV - 1, GDN_CONV_DIM)),
        'page_table': jax.random.permutation(next(ks), n_pool)[:n_used].reshape(DEC_BATCH, n_pages).astype(jnp.int32),
        'c_prompt': nrm((BATCH, D_MODEL)),
        'c_sample': nrm((DEC_BATCH, D_MODEL)),
        'w_ada': wgt((DEPTH, D_MODEL, 9 * D_MODEL), D_MODEL),
        'b_ada': nrm((DEPTH, 9 * D_MODEL), 0.02),
        'g_norm': gain((DEPTH, 3, D_MODEL)),
        'w_ffn_up': wgt((DEPTH, 2, D_MODEL, 2 * D_FF), D_MODEL),
        'w_ffn_down': wgt((DEPTH, 2, D_FF, D_MODEL), D_FF),
        'w_ml_in': wgt((N_LAYERS_A, D_MODEL, ML_IN), D_MODEL),
        'b_ml_gate': nrm((N_LAYERS_A, 2 * ML_HEADS), 0.1) + f_bias,
        'g_ml_hn': gain((N_LAYERS_A, ML_HEADS, ML_DV)),
        'w_ml_out': wgt((N_LAYERS_A, ML_HEADS * ML_DV, D_MODEL), ML_HEADS * ML_DV),
        'w_mla_in': wgt((N_LAYERS_B, D_MODEL, MLA_IN), D_MODEL),
        'g_mla_qa': gain((N_LAYERS_B, MLA_Q_LORA)),
        'g_mla_kva': gain((N_LAYERS_B, MLA_KV_LORA)),
        'w_mla_uq': wgt((N_LAYERS_B, MLA_Q_LORA, MLA_HEADS * (MLA_NOPE + MLA_ROPE)), MLA_Q_LORA),
        'w_mla_uk': wgt((N_LAYERS_B, MLA_KV_LORA, MLA_HEADS, MLA_NOPE), MLA_KV_LORA),
        'w_mla_uv': wgt((N_LAYERS_B, MLA_KV_LORA, MLA_HEADS, MLA_VD), MLA_KV_LORA),
        'g_mla_qk': gain((N_LAYERS_B, 2, MLA_NOPE + MLA_ROPE)),
        'w_mla_out': wgt((N_LAYERS_B, MLA_HEADS * MLA_VD, D_MODEL), MLA_HEADS * MLA_VD),
        'w_nsa_in': wgt((N_LAYERS_C, D_MODEL, NSA_IN), D_MODEL),
        'b_nsa_gate': nrm((N_LAYERS_C, 3 * NSA_HEADS), 0.1),
        'g_nsa_qk': gain((N_LAYERS_C, 2, NSA_HD)),
        'w_nsa_cmp': wgt((N_LAYERS_C, 2, NSA_BLOCK, NSA_HD, NSA_HD), NSA_BLOCK * NSA_HD),
        'w_nsa_out': wgt((N_LAYERS_C, NSA_HEADS * NSA_HD, D_MODEL), NSA_HEADS * NSA_HD),
        'w_gdn_in': wgt((N_LAYERS_D, D_MODEL, GDN_IN), D_MODEL),
        'w_gdn_conv': wgt((N_LAYERS_D, GDN_CONV, GDN_CONV_DIM), GDN_CONV),
        'a_gdn_log': nrm((N_LAYERS_D, GDN_HEADS), 0.5),
        'b_gdn_dt': nrm((N_LAYERS_D, GDN_HEADS), 0.5) - 2.0,
        'g_gdn_out': gain((N_LAYERS_D, GDN_DV)),
        'w_gdn_out': wgt((N_LAYERS_D, GDN_HEADS * GDN_DV, D_MODEL), GDN_HEADS * GDN_DV),
    }


def reference(x_prompt, x_sample, state_mlstm_C, state_mlstm_n, state_mlstm_m, cache_mla, cache_nsa,
              state_nsa_win, state_gdn_S, state_gdn_conv, page_table, c_prompt, c_sample,
              w_ada, b_ada, g_norm, w_ffn_up, w_ffn_down,
              w_ml_in, b_ml_gate, g_ml_hn, w_ml_out,
              w_mla_in, g_mla_qa, g_mla_kva, w_mla_uq, w_mla_uk, w_mla_uv, g_mla_qk, w_mla_out,
              w_nsa_in, b_nsa_gate, g_nsa_qk, w_nsa_cmp, w_nsa_out,
              w_gdn_in, w_gdn_conv, a_gdn_log, b_gdn_dt, g_gdn_out, w_gdn_out):
    bp = x_prompt.shape[0]
    xp, xs = x_prompt, x_sample
    ml_p, ml_s, mla_p, mla_s, nsa_p, nsa_s, gdn_p, gdn_s = [], [], [], [], [], [], [], []
    for i in range(DEPTH):
        kind, li = i % N_MIXERS, i // N_MIXERS
        mod_p = adaln(c_prompt, w_ada[i], b_ada[i])
        mod_s = adaln(c_sample, w_ada[i], b_ada[i])
        xp = ffn_half(xp, g_norm[i, 0], mod_p, 0, w_ffn_up[i, 0], w_ffn_down[i, 0])
        xs = ffn_half(xs, g_norm[i, 0], mod_s, 0, w_ffn_up[i, 0], w_ffn_down[i, 0])
        hp = modulate(xp, g_norm[i, 1], mod_p, 1)
        hs = modulate(xs, g_norm[i, 1], mod_s, 1)
        if kind == 0:
            wts = (w_ml_in[li], b_ml_gate[li], g_ml_hn[li], w_ml_out[li])
            op, c_p, n_p, m_p = mlstm_mixer(hp, *wts, jnp.zeros((bp, ML_HEADS, ML_DQK, ML_DV), jnp.float32),
                                            jnp.zeros((bp, ML_HEADS, ML_DQK), jnp.float32),
                                            jnp.zeros((bp, ML_HEADS), jnp.float32))
            os_, c_s, n_s, m_s = mlstm_mixer(hs, *wts, state_mlstm_C[li], state_mlstm_n[li], state_mlstm_m[li])
            ml_p.append((c_p, n_p, m_p))
            ml_s.append((c_s, n_s, m_s))
        elif kind == 1:
            wts = (w_mla_in[li], g_mla_qa[li], g_mla_kva[li], w_mla_uq[li], w_mla_uk[li], w_mla_uv[li],
                   g_mla_qk[li], w_mla_out[li])
            op, r_p = mla_prompt(hp, *wts)
            os_, r_s = mla_sample(hs, cache_mla, li, page_table, *wts)
            mla_p.append((r_p,))
            mla_s.append((r_s,))
        elif kind == 2:
            wts = (w_nsa_in[li], b_nsa_gate[li], g_nsa_qk[li], w_nsa_cmp[li], w_nsa_out[li])
            op, r_p, wb_p = nsa_prompt(hp, *wts)
            os_, r_s, wb_s = nsa_sample(hs, cache_nsa, li, page_table, state_nsa_win[li], *wts)
            nsa_p.append((r_p, wb_p))
            nsa_s.append((r_s, wb_s))
        else:
            wts = (w_gdn_in[li], w_gdn_conv[li], a_gdn_log[li], b_gdn_dt[li], g_gdn_out[li], w_gdn_out[li])
            op, s_p, cv_p = gdn_mixer(hp, *wts, jnp.zeros((bp, GDN_HEADS, GDN_DK, GDN_DV), jnp.float32),
                                      jnp.zeros((bp, GDN_CONV - 1, GDN_CONV_DIM), hp.dtype))
            os_, s_s, cv_s = gdn_mixer(hs, *wts, state_gdn_S[li], state_gdn_conv[li])
            gdn_p.append((s_p, cv_p))
            gdn_s.append((s_s, cv_s))
        xp = xp + mod_p[:, 1, 2][:, None] * op
        xs = xs + mod_s[:, 1, 2][:, None] * os_
        xp = ffn_half(xp, g_norm[i, 2], mod_p, 2, w_ffn_up[i, 1], w_ffn_down[i, 1])
        xs = ffn_half(xs, g_norm[i, 2], mod_s, 2, w_ffn_up[i, 1], w_ffn_down[i, 1])
    return (xp, xs,
            _stack(ml_p, 0), _stack(ml_s, 0), _stack(ml_p, 1), _stack(ml_s, 1), _stack(ml_p, 2), _stack(ml_s, 2),
            _stack(mla_p, 0), _stack(mla_s, 0),
            _stack(nsa_p, 0), _stack(nsa_s, 0), _stack(nsa_p, 1), _stack(nsa_s, 1),
            _stack(gdn_p, 0), _stack(gdn_s, 0), _stack(gdn_p, 1), _stack(gdn_s, 1))
```

```python
import functools

import jax
import jax.numpy as jnp
from jax import lax
from jax.experimental import pallas as pl
from jax.experimental.pallas import tpu as pltpu

D_MODEL = 1024
BATCH = 16
SEQ = 2048
DEPTH = 4
DEC_BATCH = 128
DEC_SEQ = 4
PAST_LEN = 8192
PAGE_SIZE = 128

N_MIXERS = 4
D_FF = 2816
RMS_EPS = 1e-6
NEG = -1e30
Q_BLOCK = 128

ML_HEADS = 8
ML_DQK = 64
ML_DV = 128
ML_CHUNK = 64
ML_GATE_CAP = 15.0

MLA_HEADS = 16
MLA_NOPE = 64
MLA_ROPE = 32
MLA_VD = 64
MLA_KV_LORA = 256
MLA_Q_LORA = 384
MLA_LAT = MLA_KV_LORA + MLA_ROPE
MLA_SCALE = (MLA_NOPE + MLA_ROPE) ** -0.5
ROPE_THETA = 10000.0

NSA_HEADS = 16
NSA_GROUPS = 2
NSA_HPG = NSA_HEADS // NSA_GROUPS
NSA_HD = 64
NSA_BLOCK = 64
NSA_TOPK = 15
NSA_WINDOW = 512
NSA_QB = 32
NSA_SCALE = NSA_HD ** -0.5

GDN_HEADS = 8
GDN_DK = 128
GDN_DV = 128
GDN_CONV = 4
GDN_CHUNK = 64
GDN_CONV_DIM = GDN_HEADS * (2 * GDN_DK + GDN_DV)

VMEM_LIMIT_BYTES = 56 * 1024 * 1024
FFN_ROW_TILE = 512
FFN_COL_CHUNK = 256


def _ffn_body(x_ref, shift_ref, scale_ref, gate_ref, g_ref, wup_ref, wdn_ref, o_ref, acc_ref):
    x = x_ref[...]
    y = x * lax.rsqrt(jnp.mean(x * x, axis=-1, keepdims=True) + RMS_EPS) * g_ref[...]
    h = (y * (1.0 + scale_ref[...]) + shift_ref[...]).astype(jnp.bfloat16)
    for j in range(D_FF // FFN_COL_CHUNK):
        lo = j * FFN_COL_CHUNK
        a = jnp.dot(h, wup_ref[:, lo:lo + FFN_COL_CHUNK], preferred_element_type=jnp.float32)
        b = jnp.dot(h, wup_ref[:, D_FF + lo:D_FF + lo + FFN_COL_CHUNK], preferred_element_type=jnp.float32)
        act = (a * jax.nn.sigmoid(a) * b).astype(jnp.bfloat16)
        part = jnp.dot(act, wdn_ref[lo:lo + FFN_COL_CHUNK, :], preferred_element_type=jnp.float32)
        if j == 0:
            acc_ref[...] = part
        else:
            acc_ref[...] += part
    o_ref[...] = x + 0.5 * gate_ref[...] * acc_ref[...]


def _ffn_call(x2d, shift, scale, gate, g, w_up, w_down, rows_per_mod):
    n = x2d.shape[0]
    tm = min(FFN_ROW_TILE, n)
    assert n % tm == 0
    if rows_per_mod:
        assert rows_per_mod % tm == 0
        tiles_per_mod = rows_per_mod // tm
        mod_spec = pl.BlockSpec((None, 1, D_MODEL), lambda i: (i // tiles_per_mod, 0, 0))
    else:
        mod_spec = pl.BlockSpec((tm, D_MODEL), lambda i: (i, 0))
    resident = functools.partial(pl.BlockSpec, pipeline_mode=pl.Buffered(1))
    return pl.pallas_call(
        _ffn_body,
        out_shape=jax.ShapeDtypeStruct((n, D_MODEL), jnp.float32),
        grid=(n // tm,),
        in_specs=[
            pl.BlockSpec((tm, D_MODEL), lambda i: (i, 0)),
            mod_spec, mod_spec, mod_spec,
            resident((1, D_MODEL), lambda i: (0, 0)),
            resident((D_MODEL, 2 * D_FF), lambda i: (0, 0)),
            resident((D_FF, D_MODEL), lambda i: (0, 0)),
        ],
        out_specs=pl.BlockSpec((tm, D_MODEL), lambda i: (i, 0)),
        scratch_shapes=[pltpu.VMEM((tm, D_MODEL), jnp.float32)],
        compiler_params=pltpu.CompilerParams(dimension_semantics=("arbitrary",),
                                             vmem_limit_bytes=VMEM_LIMIT_BYTES),
        name="ffn_half",
    )(x2d, shift, scale, gate, g.reshape(1, D_MODEL), w_up, w_down)


def ffn_half_prompt(x, g, mod, j, w_up, w_down):
    b, t, d = x.shape
    m = [mod[:, j, k][:, None, :] for k in range(3)]
    return _ffn_call(x.reshape(b * t, d), m[0], m[1], m[2], g, w_up, w_down, t).reshape(b, t, d)


def ffn_half_sample(x, g, mod, j, w_up, w_down):
    b, t, d = x.shape
    m = [jnp.repeat(mod[:, j, k], t, axis=0) for k in range(3)]
    return _ffn_call(x.reshape(b * t, d), m[0], m[1], m[2], g, w_up, w_down, 0).reshape(b, t, d)


def rmsnorm(x, g):
    xf = x.astype(jnp.float32)
    y = xf * lax.rsqrt(jnp.mean(xf * xf, axis=-1, keepdims=True) + RMS_EPS)
    return (y * g.astype(jnp.float32)).astype(x.dtype)


def l2norm(x):
    xf = x.astype(jnp.float32)
    return xf * lax.rsqrt(jnp.sum(xf * xf, axis=-1, keepdims=True) + 1e-6)


def rope(x, pos):
    half = x.shape[-1] // 2
    freqs = ROPE_THETA ** (-jnp.arange(half, dtype=jnp.float32) / half)
    ang = pos[:, None] * freqs[None, :]
    ang = ang.reshape((ang.shape[0],) + (1,) * (x.ndim - 3) + (half,))
    cos, sin = jnp.cos(ang), jnp.sin(ang)
    xf = x.astype(jnp.float32)
    x1, x2 = xf[..., :half], xf[..., half:]
    return jnp.concatenate([x1 * cos - x2 * sin, x2 * cos + x1 * sin], axis=-1).astype(x.dtype)


def masked_softmax(s, mask):
    return jax.nn.softmax(jnp.where(mask, s.astype(jnp.float32), NEG), axis=-1) * mask


def adaln(c, w, b):
    return (jax.nn.silu(c) @ w + b).reshape(c.shape[0], 3, 3, c.shape[1])


def modulate(x, g, mod, j):
    return rmsnorm(x, g) * (1.0 + mod[:, j, 1][:, None]) + mod[:, j, 0][:, None]


def _chunks(a, chunk):
    b, h, t = a.shape[:3]
    return jnp.moveaxis(a.reshape((b, h, t // chunk, chunk) + a.shape[3:]), 2, 0)


def _unchunk(a):
    a = jnp.moveaxis(a, 0, 2)
    return a.reshape(a.shape[:2] + (a.shape[2] * a.shape[3],) + a.shape[4:])


def _stack(entries, j):
    return jnp.stack([e[j] for e in entries])


def mlstm_scan(q, k, v, ig, lf, C0, n0, m0, chunk):
    causal = jnp.tril(jnp.ones((chunk, chunk), bool))

    def step(carry, inp):
        C, n, m = carry
        qc, kc, vc, ic, fc = inp
        b = jnp.cumsum(fc, axis=-1)
        dmat = jnp.where(causal, b[..., :, None] - b[..., None, :] + ic[..., None, :], NEG)
        inter = b + m[..., None]
        mt = jnp.maximum(inter, jnp.max(dmat, axis=-1))
        w_intra = jnp.exp(dmat - mt[..., None])
        w_inter = jnp.exp(inter - mt)
        s = jnp.einsum('bhld,bhsd->bhls', qc, kc) * w_intra
        num = w_inter[..., None] * jnp.einsum('bhld,bhdv->bhlv', qc, C) + jnp.einsum('bhls,bhsv->bhlv', s, vc)
        den = w_inter * jnp.einsum('bhld,bhd->bhl', qc, n) + jnp.sum(s, axis=-1)
        hc = num / jnp.maximum(jnp.abs(den), jnp.exp(-mt))[..., None]
        m_new = mt[..., -1]
        w_state = jnp.exp(b[..., -1:] - b + ic - m_new[..., None])
        decay = jnp.exp(b[..., -1] + m - m_new)
        C_new = decay[..., None, None] * C + jnp.einsum('bhs,bhsd,bhsv->bhdv', w_state, kc, vc)
        n_new = decay[..., None] * n + jnp.einsum('bhs,bhsd->bhd', w_state, kc)
        return (C_new, n_new, m_new), hc

    xs = tuple(_chunks(a, chunk) for a in (q, k, v, ig, lf))
    (C, n, m), h = lax.scan(step, (C0, n0, m0), xs)
    return _unchunk(h), C, n, m


def mlstm_mixer(h, w_in, b_gate, g_hn, w_out, C0, n0, m0):
    b, t, _ = h.shape
    qk_w, v_w = ML_HEADS * ML_DQK, ML_HEADS * ML_DV
    q, k, v, o, gates = jnp.split(h @ w_in, [qk_w, 2 * qk_w, 2 * qk_w + v_w, 2 * qk_w + 2 * v_w], axis=-1)

    def heads(a, d):
        return a.reshape(b, t, ML_HEADS, d).transpose(0, 2, 1, 3).astype(jnp.float32)

    gates = ML_GATE_CAP * jnp.tanh((gates + b_gate).astype(jnp.float32) / ML_GATE_CAP)
    ig = gates[..., :ML_HEADS].transpose(0, 2, 1)
    lf = jax.nn.log_sigmoid(gates[..., ML_HEADS:]).transpose(0, 2, 1)
    hh, C, n, m = mlstm_scan(heads(q, ML_DQK) * ML_DQK ** -0.5, heads(k, ML_DQK), heads(v, ML_DV), ig, lf,
                             C0.astype(jnp.float32), n0.astype(jnp.float32), m0.astype(jnp.float32),
                             min(ML_CHUNK, t))
    hh = rmsnorm(hh.transpose(0, 2, 1, 3), g_hn).astype(h.dtype).reshape(b, t, v_w)
    return (jax.nn.sigmoid(o) * hh) @ w_out, C.astype(h.dtype), n.astype(h.dtype), m.astype(h.dtype)


def gdn_scan(q, k, v, beta, g, S0, chunk):
    causal = jnp.tril(jnp.ones((chunk, chunk), bool))
    strict = jnp.tril(jnp.ones((chunk, chunk), bool), -1)

    def step(S, inp):
        qc, kc, vc, bc, gc = inp
        b = jnp.cumsum(gc, axis=-1)
        decay = jnp.exp(jnp.where(causal, b[..., :, None] - b[..., None, :], NEG))
        a_mat = jnp.where(strict, bc[..., :, None] * jnp.einsum('bhtd,bhsd->bhts', kc, kc) * decay, 0.0)
        eb = jnp.exp(b)
        rhs = bc[..., None] * (vc - eb[..., None] * jnp.einsum('bhtd,bhdv->bhtv', kc, S))
        u = lax.linalg.triangular_solve(a_mat, rhs, left_side=True, lower=True, unit_diagonal=True)
        qk = jnp.einsum('bhtd,bhsd->bhts', qc, kc) * decay
        o = eb[..., None] * jnp.einsum('bhtd,bhdv->bhtv', qc, S) + jnp.einsum('bhts,bhsv->bhtv', qk, u)
        w_state = jnp.exp(b[..., -1:] - b)
        S_new = eb[..., -1][..., None, None] * S + jnp.einsum('bhs,bhsd,bhsv->bhdv', w_state, kc, u)
        return S_new, o

    xs = tuple(_chunks(a, chunk) for a in (q, k, v, beta, g))
    S, o = lax.scan(step, S0, xs)
    return _unchunk(o), S


def gdn_mixer(h, w_in, w_conv, a_log, dt_bias, g_out, w_out, S0, conv0):
    b, t, _ = h.shape
    qkv, beta, a, gate = jnp.split(h @ w_in, [GDN_CONV_DIM, GDN_CONV_DIM + GDN_HEADS, GDN_CONV_DIM + 2 * GDN_HEADS], axis=-1)
    xc = jnp.concatenate([conv0.astype(qkv.dtype), qkv], axis=1)
    conv = lax.conv_general_dilated(xc, w_conv[:, None, :].astype(xc.dtype), (1,), 'VALID',
                                    dimension_numbers=('NWC', 'WIO', 'NWC'), feature_group_count=GDN_CONV_DIM)
    q, k, v = jnp.split(jax.nn.silu(conv), [GDN_HEADS * GDN_DK, 2 * GDN_HEADS * GDN_DK], axis=-1)
    q = l2norm(q.reshape(b, t, GDN_HEADS, GDN_DK)).transpose(0, 2, 1, 3) * GDN_DK ** -0.5
    k = l2norm(k.reshape(b, t, GDN_HEADS, GDN_DK)).transpose(0, 2, 1, 3)
    v = v.reshape(b, t, GDN_HEADS, GDN_DV).transpose(0, 2, 1, 3).astype(jnp.float32)
    beta = jax.nn.sigmoid(beta.astype(jnp.float32)).transpose(0, 2, 1)
    g = (-jnp.exp(a_log.astype(jnp.float32)) * jax.nn.softplus((a + dt_bias).astype(jnp.float32))).transpose(0, 2, 1)
    o, S = gdn_scan(q, k, v, beta, g, S0.astype(jnp.float32), min(GDN_CHUNK, t))
    o = rmsnorm(o.transpose(0, 2, 1, 3), g_out).astype(h.dtype).reshape(b, t, GDN_HEADS * GDN_DV)
    return (o * jax.nn.silu(gate)) @ w_out, S.astype(h.dtype), xc[:, -(GDN_CONV - 1):]


def mla_project(h, pos, w_in, g_qa, g_kva, w_uq, g_qk):
    b, t, _ = h.shape
    cq, ckv, kr = jnp.split(h @ w_in, [MLA_Q_LORA, MLA_Q_LORA + MLA_KV_LORA], axis=-1)
    q = (rmsnorm(cq, g_qa) @ w_uq).reshape(b, t, MLA_HEADS, MLA_NOPE + MLA_ROPE)
    q_nope = rmsnorm(q[..., :MLA_NOPE], g_qk[0, :MLA_NOPE])
    q_rope = rope(rmsnorm(q[..., MLA_NOPE:], g_qk[0, MLA_NOPE:]), pos)
    k_rope = rope(rmsnorm(kr, g_qk[1, MLA_NOPE:]), pos)
    latent = jnp.concatenate([rmsnorm(ckv, g_kva), k_rope], axis=-1)
    return q_nope, q_rope, latent


def mla_scores(q_nope, q_rope, k_nope, k_rope):
    s = jnp.einsum('bthd,bshd->bhts', q_nope, k_nope) + jnp.einsum('bthr,bsr->bhts', q_rope, k_rope)
    return s.astype(jnp.float32) * MLA_SCALE


def mla_keys(latent, w_uk, g_kn):
    ckv = latent[..., :MLA_KV_LORA]
    return ckv, rmsnorm(jnp.einsum('bsc,chd->bshd', ckv, w_uk), g_kn), latent[..., MLA_KV_LORA:]


def mla_prompt(h, w_in, g_qa, g_kva, w_uq, w_uk, w_uv, g_qk, w_out):
    b, t, _ = h.shape
    q_nope, q_rope, latent = mla_project(h, jnp.arange(t, dtype=jnp.float32), w_in, g_qa, g_kva, w_uq, g_qk)
    ckv, k_nope, k_rope = mla_keys(latent, w_uk, g_qk[1, :MLA_NOPE])
    v = jnp.einsum('bsc,chd->bshd', ckv, w_uv)
    kpos = jnp.arange(t)

    def block(i):
        t0 = i * Q_BLOCK
        qn = lax.dynamic_slice_in_dim(q_nope, t0, Q_BLOCK, axis=1)
        qr = lax.dynamic_slice_in_dim(q_rope, t0, Q_BLOCK, axis=1)
        mask = kpos[None, :] <= (t0 + jnp.arange(Q_BLOCK))[:, None]
        p = masked_softmax(mla_scores(qn, qr, k_nope, k_rope), mask)
        return jnp.einsum('bhts,bshd->bthd', p.astype(v.dtype), v)

    o = jnp.moveaxis(lax.map(block, jnp.arange(t // Q_BLOCK)), 0, 1).reshape(b, t, MLA_HEADS * MLA_VD)
    return o @ w_out, latent


def mla_sample(h, pool, li, page_table, w_in, g_qa, g_kva, w_uq, w_uk, w_uv, g_qk, w_out):
    b, t, _ = h.shape
    q_nope, q_rope, latent = mla_project(h, PAST_LEN + jnp.arange(t, dtype=jnp.float32), w_in, g_qa, g_kva, w_uq, g_qk)
    g_kn = g_qk[1, :MLA_NOPE]

    def update(carry, s, ckv):
        m, l, acc = carry
        m_new = jnp.maximum(m, jnp.max(s, axis=-1))
        corr = jnp.exp(m - m_new)
        p = jnp.exp(s - m_new[..., None])
        return (m_new, l * corr + jnp.sum(p, axis=-1),
                acc * corr[..., None] + jnp.einsum('bhts,bsc->bhtc', p, ckv.astype(jnp.float32)))

    def page_step(carry, p):
        ckv, k_nope, k_rope = mla_keys(pool[li, page_table[:, p]], w_uk, g_kn)
        return update(carry, mla_scores(q_nope, q_rope, k_nope, k_rope), ckv), None

    init = (jnp.full((b, MLA_HEADS, t), NEG, jnp.float32), jnp.zeros((b, MLA_HEADS, t), jnp.float32),
            jnp.zeros((b, MLA_HEADS, t, MLA_KV_LORA), jnp.float32))
    carry, _ = lax.scan(page_step, init, jnp.arange(page_table.shape[1]))
    ckv, k_nope, k_rope = mla_keys(latent, w_uk, g_kn)
    s_new = jnp.where(jnp.tril(jnp.ones((t, t), bool)), mla_scores(q_nope, q_rope, k_nope, k_rope), NEG)
    _, l, acc = update(carry, s_new, ckv)
    o = jnp.einsum('bhtc,chd->bthd', acc / l[..., None], w_uv.astype(jnp.float32))
    return o.astype(h.dtype).reshape(b, t, MLA_HEADS * MLA_VD) @ w_out, latent


def nsa_project(h, w_in, b_gate, g_qk):
    b, t, _ = h.shape
    qw, kvw = NSA_HEADS * NSA_HD, 6 * NSA_GROUPS * NSA_HD
    q, kv, gates = jnp.split(h @ w_in, [qw, qw + kvw], axis=-1)
    q = rmsnorm(q.reshape(b, t, NSA_GROUPS, NSA_HPG, NSA_HD), g_qk[0])
    kv = kv.reshape(b, t, 6, NSA_GROUPS, NSA_HD)
    rows = jnp.stack([kv[:, :, 0], kv[:, :, 1], rmsnorm(kv[:, :, 2], g_qk[1]), kv[:, :, 3]], axis=2)
    win = jnp.stack([rmsnorm(kv[:, :, 4], g_qk[1]), kv[:, :, 5]], axis=2)
    gates = jax.nn.sigmoid(gates + b_gate).reshape(b, t, NSA_GROUPS, NSA_HPG, 3)
    return q, rows, win, gates


def nsa_compress(kv_rows, w_cmp, g_k):
    b, l = kv_rows.shape[:2]
    blk = kv_rows.reshape(b, l // NSA_BLOCK, NSA_BLOCK, 2, NSA_GROUPS, NSA_HD)
    c = jnp.einsum('bnpcgd,cpde->bncge', blk, w_cmp)
    return rmsnorm(c[:, :, 0], g_k), c[:, :, 1]


def nsa_cmp_attn(q, k_c, v_c, qpos):
    nb = k_c.shape[1]
    s = jnp.einsum('btghd,bngd->bghtn', q, k_c) * NSA_SCALE
    mask = ((jnp.arange(nb) + 1) * NSA_BLOCK - 1)[None, :] <= qpos[:, None]
    p = masked_softmax(s, mask)
    o = jnp.einsum('bghtn,bngd->btghd', p.astype(v_c.dtype), v_c)
    return o, jnp.sum(p, axis=2)


def nsa_select(imp, qpos):
    nb = imp.shape[-1]
    cand = jnp.arange(nb)[None, :] < (qpos // NSA_BLOCK)[:, None]
    val, idx = lax.top_k(jnp.where(cand, imp, -jnp.inf), min(NSA_TOPK, nb))
    return idx, jnp.isfinite(val)


def nsa_sel_prompt(q, k_s, v_s, idx, valid):
    b, t = q.shape[:2]
    nb = t // NSA_BLOCK
    kb = k_s.reshape(b, nb, NSA_BLOCK, NSA_GROUPS, NSA_HD).transpose(0, 3, 1, 2, 4)
    vb = v_s.reshape(b, nb, NSA_BLOCK, NSA_GROUPS, NSA_HD).transpose(0, 3, 1, 2, 4)
    bi = jnp.arange(b)[:, None, None, None]
    gi = jnp.arange(NSA_GROUPS)[None, None, :, None]
    idx, valid = idx.transpose(0, 2, 1, 3), valid.transpose(0, 2, 1, 3)

    def block(i):
        t0 = i * NSA_QB
        qpos = t0 + jnp.arange(NSA_QB)
        qq = lax.dynamic_slice_in_dim(q, t0, NSA_QB, axis=1)
        cur = jnp.broadcast_to((qpos // NSA_BLOCK)[None, :, None, None], (b, NSA_QB, NSA_GROUPS, 1))
        ii = jnp.concatenate([lax.dynamic_slice_in_dim(idx, t0, NSA_QB, axis=1), cur], axis=-1)
        ok = jnp.concatenate([lax.dynamic_slice_in_dim(valid, t0, NSA_QB, axis=1), jnp.ones(cur.shape, bool)], axis=-1)
        kpos = ii[..., None] * NSA_BLOCK + jnp.arange(NSA_BLOCK)
        mask = ok[..., None] & (kpos <= qpos[None, :, None, None, None])
        n = ii.shape[-1] * NSA_BLOCK
        ks, vs = kb[bi, gi, ii], vb[bi, gi, ii]
        s = jnp.einsum('btghd,btgkpd->bghtkp', qq, ks).reshape(b, NSA_GROUPS, NSA_HPG, NSA_QB, n) * NSA_SCALE
        mask = mask.transpose(0, 2, 1, 3, 4).reshape(b, NSA_GROUPS, 1, NSA_QB, n)
        p = masked_softmax(s, mask)
        return jnp.einsum('bghtn,btgnd->btghd', p.astype(vs.dtype), vs.reshape(b, NSA_QB, NSA_GROUPS, n, NSA_HD))

    return jnp.moveaxis(lax.map(block, jnp.arange(t // NSA_QB)), 0, 1).reshape(q.shape)


def nsa_sel_sample(q, k_new, v_new, idx, valid, pool, li, page_table):
    b, t = q.shape[:2]
    bpp = PAGE_SIZE // NSA_BLOCK
    kk = idx.shape[-1]
    n = kk * NSA_BLOCK
    ii, ok = idx.transpose(0, 2, 1, 3), valid.transpose(0, 2, 1, 3)
    phys = page_table[jnp.arange(b)[:, None, None, None], ii // bpp][..., None]
    rows = (ii % bpp)[..., None] * NSA_BLOCK + jnp.arange(NSA_BLOCK)
    gi = jnp.arange(NSA_GROUPS)[None, None, :, None, None]
    ks = pool[li, phys, rows, 2, gi].reshape(b, t, NSA_GROUPS, n, NSA_HD)
    vs = pool[li, phys, rows, 3, gi].reshape(b, t, NSA_GROUPS, n, NSA_HD)
    s = jnp.concatenate([jnp.einsum('btghd,btgnd->bghtn', q, ks), jnp.einsum('btghd,bsgd->bghts', q, k_new)], axis=-1)
    mask_past = jnp.repeat(ok, NSA_BLOCK, axis=-1).transpose(0, 2, 1, 3)[:, :, None]
    mask_new = jnp.broadcast_to(jnp.tril(jnp.ones((t, t), bool)), (b, NSA_GROUPS, 1, t, t))
    p = masked_softmax(s * NSA_SCALE, jnp.concatenate([mask_past, mask_new], axis=-1)).astype(v_new.dtype)
    return (jnp.einsum('bghtn,btgnd->btghd', p[..., :n], vs)
            + jnp.einsum('bghts,bsgd->btghd', p[..., n:], v_new))


def nsa_win_prompt(q, k_w, v_w):
    t = q.shape[1]
    pad = ((0, 0), (NSA_WINDOW, 0), (0, 0), (0, 0))
    kp, vp = jnp.pad(k_w, pad), jnp.pad(v_w, pad)
    span = NSA_WINDOW + Q_BLOCK

    def block(i):
        t0 = i * Q_BLOCK
        qq = lax.dynamic_slice_in_dim(q, t0, Q_BLOCK, axis=1)
        kk = lax.dynamic_slice_in_dim(kp, t0, span, axis=1)
        vv = lax.dynamic_slice_in_dim(vp, t0, span, axis=1)
        qpos = t0 + jnp.arange(Q_BLOCK)
        kpos = t0 - NSA_WINDOW + jnp.arange(span)
        mask = ((kpos[None, :] <= qpos[:, None]) & (kpos[None, :] > qpos[:, None] - NSA_WINDOW)
                & (kpos[None, :] >= 0))
        p = masked_softmax(jnp.einsum('bqghd,bkgd->bghqk', qq, kk) * NSA_SCALE, mask)
        return jnp.einsum('bghqk,bkgd->bqghd', p.astype(vv.dtype), vv)

    return jnp.moveaxis(lax.map(block, jnp.arange(t // Q_BLOCK)), 0, 1).reshape(q.shape)


def nsa_win_sample(q, win, buf):
    t = q.shape[1]
    wb = buf.shape[1]
    kv = jnp.concatenate([buf.astype(win.dtype), win], axis=1)
    qpos = PAST_LEN + jnp.arange(t)
    kpos = PAST_LEN - wb + jnp.arange(wb + t)
    mask = (kpos[None, :] <= qpos[:, None]) & (kpos[None, :] > qpos[:, None] - NSA_WINDOW)
    p = masked_softmax(jnp.einsum('btghd,bsgd->bghts', q, kv[:, :, 0]) * NSA_SCALE, mask)
    o = jnp.einsum('bghts,bsgd->btghd', p.astype(kv.dtype), kv[:, :, 1])
    return o, kv[:, -wb:]


def nsa_merge(gates, o_cmp, o_sel, o_win):
    o = gates[..., 0:1] * o_cmp + gates[..., 1:2] * o_sel + gates[..., 2:3] * o_win
    return o.reshape(o.shape[0], o.shape[1], NSA_HEADS * NSA_HD)


def nsa_prompt(h, w_in, b_gate, g_qk, w_cmp, w_out):
    t = h.shape[1]
    q, rows, win, gates = nsa_project(h, w_in, b_gate, g_qk)
    qpos = jnp.arange(t)
    k_c, v_c = nsa_compress(rows[:, :, :2], w_cmp, g_qk[1])
    o_cmp, imp = nsa_cmp_attn(q, k_c, v_c, qpos)
    idx, valid = nsa_select(imp, qpos)
    o_sel = nsa_sel_prompt(q, rows[:, :, 2], rows[:, :, 3], idx, valid)
    o_win = nsa_win_prompt(q, win[:, :, 0], win[:, :, 1])
    return nsa_merge(gates, o_cmp, o_sel, o_win) @ w_out, rows, win[:, -min(NSA_WINDOW, t):]


def nsa_sample(h, pool, li, page_table, win_buf, w_in, b_gate, g_qk, w_cmp, w_out):
    b, t, _ = h.shape
    q, rows, win, gates = nsa_project(h, w_in, b_gate, g_qk)
    qpos = PAST_LEN + jnp.arange(t)

    def page_cmp(p):
        return nsa_compress(pool[li, page_table[:, p], :, :2], w_cmp, g_qk[1])

    k_c, v_c = lax.map(page_cmp, jnp.arange(page_table.shape[1]))
    k_c = jnp.moveaxis(k_c, 0, 1).reshape(b, -1, NSA_GROUPS, NSA_HD)
    v_c = jnp.moveaxis(v_c, 0, 1).reshape(b, -1, NSA_GROUPS, NSA_HD)
    o_cmp, imp = nsa_cmp_attn(q, k_c, v_c, qpos)
    idx, valid = nsa_select(imp, qpos)
    o_sel = nsa_sel_sample(q, rows[:, :, 2], rows[:, :, 3], idx, valid, pool, li, page_table)
    o_win, new_buf = nsa_win_sample(q, win, win_buf)
    return nsa_merge(gates, o_cmp, o_sel, o_win) @ w_out, rows, new_buf


def kernel(x_prompt, x_sample, state_mlstm_C, state_mlstm_n, state_mlstm_m, cache_mla, cache_nsa,
           state_nsa_win, state_gdn_S, state_gdn_conv, page_table, c_prompt, c_sample,
           w_ada, b_ada, g_norm, w_ffn_up, w_ffn_down,
           w_ml_in, b_ml_gate, g_ml_hn, w_ml_out,
           w_mla_in, g_mla_qa, g_mla_kva, w_mla_uq, w_mla_uk, w_mla_uv, g_mla_qk, w_mla_out,
           w_nsa_in, b_nsa_gate, g_nsa_qk, w_nsa_cmp, w_nsa_out,
           w_gdn_in, w_gdn_conv, a_gdn_log, b_gdn_dt, g_gdn_out, w_gdn_out):
    bp = x_prompt.shape[0]
    xp, xs = x_prompt, x_sample
    ml_p, ml_s, mla_p, mla_s, nsa_p, nsa_s, gdn_p, gdn_s = [], [], [], [], [], [], [], []
    w_up_bf = w_ffn_up.astype(jnp.bfloat16)
    w_dn_bf = w_ffn_down.astype(jnp.bfloat16)
    for i in range(DEPTH):
        kind, li = i % N_MIXERS, i // N_MIXERS
        mod_p = adaln(c_prompt, w_ada[i], b_ada[i])
        mod_s = adaln(c_sample, w_ada[i], b_ada[i])
        xp = ffn_half_prompt(xp, g_norm[i, 0], mod_p, 0, w_up_bf[i, 0], w_dn_bf[i, 0])
        xs = ffn_half_sample(xs, g_norm[i, 0], mod_s, 0, w_up_bf[i, 0], w_dn_bf[i, 0])
        hp = modulate(xp, g_norm[i, 1], mod_p, 1)
        hs = modulate(xs, g_norm[i, 1], mod_s, 1)
        if kind == 0:
            wts = (w_ml_in[li], b_ml_gate[li], g_ml_hn[li], w_ml_out[li])
            op, c_p, n_p, m_p = mlstm_mixer(hp, *wts, jnp.zeros((bp, ML_HEADS, ML_DQK, ML_DV), jnp.float32),
                                            jnp.zeros((bp, ML_HEADS, ML_DQK), jnp.float32),
                                            jnp.zeros((bp, ML_HEADS), jnp.float32))
            os_, c_s, n_s, m_s = mlstm_mixer(hs, *wts, state_mlstm_C[li], state_mlstm_n[li], state_mlstm_m[li])
            ml_p.append((c_p, n_p, m_p))
            ml_s.append((c_s, n_s, m_s))
        elif kind == 1:
            wts = (w_mla_in[li], g_mla_qa[li], g_mla_kva[li], w_mla_uq[li], w_mla_uk[li], w_mla_uv[li],
                   g_mla_qk[li], w_mla_out[li])
            op, r_p = mla_prompt(hp, *wts)
            os_, r_s = mla_sample(hs, cache_mla, li, page_table, *wts)
            mla_p.append((r_p,))
            mla_s.append((r_s,))
        elif kind == 2:
            wts = (w_nsa_in[li], b_nsa_gate[li], g_nsa_qk[li], w_nsa_cmp[li], w_nsa_out[li])
            op, r_p, wb_p = nsa_prompt(hp, *wts)
            os_, r_s, wb_s = nsa_sample(hs, cache_nsa, li, page_table, state_nsa_win[li], *wts)
            nsa_p.append((r_p, wb_p))
            nsa_s.append((r_s, wb_s))
        else:
            wts = (w_gdn_in[li], w_gdn_conv[li], a_gdn_log[li], b_gdn_dt[li], g_gdn_out[li], w_gdn_out[li])
            op, s_p, cv_p = gdn_mixer(hp, *wts, jnp.zeros((bp, GDN_HEADS, GDN_DK, GDN_DV), jnp.float32),
                                      jnp.zeros((bp, GDN_CONV - 1, GDN_CONV_DIM), hp.dtype))
            os_, s_s, cv_s = gdn_mixer(hs, *wts, state_gdn_S[li], state_gdn_conv[li])
            gdn_p.append((s_p, cv_p))
            gdn_s.append((s_s, cv_s))
        xp = xp + mod_p[:, 1, 2][:, None] * op
        xs = xs + mod_s[:, 1, 2][:, None] * os_
        xp = ffn_half_prompt(xp, g_norm[i, 2], mod_p, 2, w_up_bf[i, 1], w_dn_bf[i, 1])
        xs = ffn_half_sample(xs, g_norm[i, 2], mod_s, 2, w_up_bf[i, 1], w_dn_bf[i, 1])
    return (xp, xs,
            _stack(ml_p, 0), _stack(ml_s, 0), _stack(ml_p, 1), _stack(ml_s, 1), _stack(ml_p, 2), _stack(ml_s, 2),
            _stack(mla_p, 0), _stack(mla_s, 0),
            _stack(nsa_p, 0), _stack(nsa_s, 0), _stack(nsa_p, 1), _stack(nsa_s, 1),
            _stack(gdn_p, 0), _stack(gdn_s, 0), _stack(gdn_p, 1), _stack(gdn_s, 1))
```

```python
import functools

import jax
import jax.numpy as jnp
from jax import lax
from jax.experimental import pallas as pl
from jax.experimental.pallas import tpu as pltpu

D_MODEL = 1024
BATCH = 16
SEQ = 2048
DEPTH = 4
DEC_BATCH = 128
DEC_SEQ = 4
PAST_LEN = 8192
PAGE_SIZE = 128

N_MIXERS = 4
D_FF = 2816
RMS_EPS = 1e-6
NEG = -1e30
Q_BLOCK = 128

ML_HEADS = 8
ML_DQK = 64
ML_DV = 128
ML_CHUNK = 64
ML_GATE_CAP = 15.0

MLA_HEADS = 16
MLA_NOPE = 64
MLA_ROPE = 32
MLA_VD = 64
MLA_KV_LORA = 256
MLA_Q_LORA = 384
MLA_LAT = MLA_KV_LORA + MLA_ROPE
MLA_SCALE = (MLA_NOPE + MLA_ROPE) ** -0.5
ROPE_THETA = 10000.0

NSA_HEADS = 16
NSA_GROUPS = 2
NSA_HPG = NSA_HEADS // NSA_GROUPS
NSA_HD = 64
NSA_BLOCK = 64
NSA_TOPK = 15
NSA_WINDOW = 512
NSA_QB = 32
NSA_SCALE = NSA_HD ** -0.5

GDN_HEADS = 8
GDN_DK = 128
GDN_DV = 128
GDN_CONV = 4
GDN_CHUNK = 64
GDN_CONV_DIM = GDN_HEADS * (2 * GDN_DK + GDN_DV)

VMEM_LIMIT_BYTES = 56 * 1024 * 1024
FFN_ROW_TILE = 512
FFN_COL_CHUNK = 256
GDN_HEADS_PER_STEP = 4
NSA_TQ = 128
NSA_TK = 256
_HI = lax.Precision.HIGHEST


def _ffn_body(x_ref, shift_ref, scale_ref, gate_ref, g_ref, wup_ref, wdn_ref, o_ref, acc_ref):
    x = x_ref[...]
    y = x * lax.rsqrt(jnp.mean(x * x, axis=-1, keepdims=True) + RMS_EPS) * g_ref[...]
    h = (y * (1.0 + scale_ref[...]) + shift_ref[...]).astype(jnp.bfloat16)
    for j in range(D_FF // FFN_COL_CHUNK):
        lo = j * FFN_COL_CHUNK
        a = jnp.dot(h, wup_ref[:, lo:lo + FFN_COL_CHUNK], preferred_element_type=jnp.float32)
        b = jnp.dot(h, wup_ref[:, D_FF + lo:D_FF + lo + FFN_COL_CHUNK], preferred_element_type=jnp.float32)
        act = (a * jax.nn.sigmoid(a) * b).astype(jnp.bfloat16)
        part = jnp.dot(act, wdn_ref[lo:lo + FFN_COL_CHUNK, :], preferred_element_type=jnp.float32)
        if j == 0:
            acc_ref[...] = part
        else:
            acc_ref[...] += part
    o_ref[...] = x + 0.5 * gate_ref[...] * acc_ref[...]


def _ffn_call(x2d, shift, scale, gate, g, w_up, w_down, rows_per_mod):
    n = x2d.shape[0]
    tm = min(FFN_ROW_TILE, n)
    assert n % tm == 0
    if rows_per_mod:
        assert rows_per_mod % tm == 0
        tiles_per_mod = rows_per_mod // tm
        mod_spec = pl.BlockSpec((None, 1, D_MODEL), lambda i: (i // tiles_per_mod, 0, 0))
    else:
        mod_spec = pl.BlockSpec((tm, D_MODEL), lambda i: (i, 0))
    resident = functools.partial(pl.BlockSpec, pipeline_mode=pl.Buffered(1))
    return pl.pallas_call(
        _ffn_body,
        out_shape=jax.ShapeDtypeStruct((n, D_MODEL), jnp.float32),
        grid=(n // tm,),
        in_specs=[
            pl.BlockSpec((tm, D_MODEL), lambda i: (i, 0)),
            mod_spec, mod_spec, mod_spec,
            resident((1, D_MODEL), lambda i: (0, 0)),
            resident((D_MODEL, 2 * D_FF), lambda i: (0, 0)),
            resident((D_FF, D_MODEL), lambda i: (0, 0)),
        ],
        out_specs=pl.BlockSpec((tm, D_MODEL), lambda i: (i, 0)),
        scratch_shapes=[pltpu.VMEM((tm, D_MODEL), jnp.float32)],
        compiler_params=pltpu.CompilerParams(dimension_semantics=("arbitrary",),
                                             vmem_limit_bytes=VMEM_LIMIT_BYTES),
        name="ffn_half",
    )(x2d, shift, scale, gate, g.reshape(1, D_MODEL), w_up, w_down)


def ffn_half_prompt(x, g, mod, j, w_up, w_down):
    b, t, d = x.shape
    m = [mod[:, j, k][:, None, :] for k in range(3)]
    return _ffn_call(x.reshape(b * t, d), m[0], m[1], m[2], g, w_up, w_down, t).reshape(b, t, d)


def ffn_half_sample(x, g, mod, j, w_up, w_down):
    b, t, d = x.shape
    m = [jnp.repeat(mod[:, j, k], t, axis=0) for k in range(3)]
    return _ffn_call(x.reshape(b * t, d), m[0], m[1], m[2], g, w_up, w_down, 0).reshape(b, t, d)


def rmsnorm(x, g):
    xf = x.astype(jnp.float32)
    y = xf * lax.rsqrt(jnp.mean(xf * xf, axis=-1, keepdims=True) + RMS_EPS)
    return (y * g.astype(jnp.float32)).astype(x.dtype)


def l2norm(x):
    xf = x.astype(jnp.float32)
    return xf * lax.rsqrt(jnp.sum(xf * xf, axis=-1, keepdims=True) + 1e-6)


def rope(x, pos):
    half = x.shape[-1] // 2
    freqs = ROPE_THETA ** (-jnp.arange(half, dtype=jnp.float32) / half)
    ang = pos[:, None] * freqs[None, :]
    ang = ang.reshape((ang.shape[0],) + (1,) * (x.ndim - 3) + (half,))
    cos, sin = jnp.cos(ang), jnp.sin(ang)
    xf = x.astype(jnp.float32)
    x1, x2 = xf[..., :half], xf[..., half:]
    return jnp.concatenate([x1 * cos - x2 * sin, x2 * cos + x1 * sin], axis=-1).astype(x.dtype)


def masked_softmax(s, mask):
    return jax.nn.softmax(jnp.where(mask, s.astype(jnp.float32), NEG), axis=-1) * mask


def adaln(c, w, b):
    return (jax.nn.silu(c) @ w + b).reshape(c.shape[0], 3, 3, c.shape[1])


def modulate(x, g, mod, j):
    return rmsnorm(x, g) * (1.0 + mod[:, j, 1][:, None]) + mod[:, j, 0][:, None]


def _chunks(a, chunk):
    b, h, t = a.shape[:3]
    return jnp.moveaxis(a.reshape((b, h, t // chunk, chunk) + a.shape[3:]), 2, 0)


def _unchunk(a):
    a = jnp.moveaxis(a, 0, 2)
    return a.reshape(a.shape[:2] + (a.shape[2] * a.shape[3],) + a.shape[4:])


def _stack(entries, j):
    return jnp.stack([e[j] for e in entries])


def mlstm_scan(q, k, v, ig, lf, C0, n0, m0, chunk):
    causal = jnp.tril(jnp.ones((chunk, chunk), bool))

    def step(carry, inp):
        C, n, m = carry
        qc, kc, vc, ic, fc = inp
        b = jnp.cumsum(fc, axis=-1)
        dmat = jnp.where(causal, b[..., :, None] - b[..., None, :] + ic[..., None, :], NEG)
        inter = b + m[..., None]
        mt = jnp.maximum(inter, jnp.max(dmat, axis=-1))
        w_intra = jnp.exp(dmat - mt[..., None])
        w_inter = jnp.exp(inter - mt)
        s = jnp.einsum('bhld,bhsd->bhls', qc, kc) * w_intra
        num = w_inter[..., None] * jnp.einsum('bhld,bhdv->bhlv', qc, C) + jnp.einsum('bhls,bhsv->bhlv', s, vc)
        den = w_inter * jnp.einsum('bhld,bhd->bhl', qc, n) + jnp.sum(s, axis=-1)
        hc = num / jnp.maximum(jnp.abs(den), jnp.exp(-mt))[..., None]
        m_new = mt[..., -1]
        w_state = jnp.exp(b[..., -1:] - b + ic - m_new[..., None])
        decay = jnp.exp(b[..., -1] + m - m_new)
        C_new = decay[..., None, None] * C + jnp.einsum('bhs,bhsd,bhsv->bhdv', w_state, kc, vc)
        n_new = decay[..., None] * n + jnp.einsum('bhs,bhsd->bhd', w_state, kc)
        return (C_new, n_new, m_new), hc

    xs = tuple(_chunks(a, chunk) for a in (q, k, v, ig, lf))
    (C, n, m), h = lax.scan(step, (C0, n0, m0), xs)
    return _unchunk(h), C, n, m


def mlstm_mixer(h, w_in, b_gate, g_hn, w_out, C0, n0, m0):
    b, t, _ = h.shape
    qk_w, v_w = ML_HEADS * ML_DQK, ML_HEADS * ML_DV
    q, k, v, o, gates = jnp.split(h @ w_in, [qk_w, 2 * qk_w, 2 * qk_w + v_w, 2 * qk_w + 2 * v_w], axis=-1)

    def heads(a, d):
        return a.reshape(b, t, ML_HEADS, d).transpose(0, 2, 1, 3).astype(jnp.float32)

    gates = ML_GATE_CAP * jnp.tanh((gates + b_gate).astype(jnp.float32) / ML_GATE_CAP)
    ig = gates[..., :ML_HEADS].transpose(0, 2, 1)
    lf = jax.nn.log_sigmoid(gates[..., ML_HEADS:]).transpose(0, 2, 1)
    hh, C, n, m = mlstm_scan(heads(q, ML_DQK) * ML_DQK ** -0.5, heads(k, ML_DQK), heads(v, ML_DV), ig, lf,
                             C0.astype(jnp.float32), n0.astype(jnp.float32), m0.astype(jnp.float32),
                             min(ML_CHUNK, t))
    hh = rmsnorm(hh.transpose(0, 2, 1, 3), g_hn).astype(h.dtype).reshape(b, t, v_w)
    return (jax.nn.sigmoid(o) * hh) @ w_out, C.astype(h.dtype), n.astype(h.dtype), m.astype(h.dtype)


def _dot_nt(a, b, **kw):
    return lax.dot_general(a, b, (((1,), (1,)), ((), ())), preferred_element_type=jnp.float32, **kw)


def _dot_tn(a, b, **kw):
    return lax.dot_general(a, b, (((0,), (0,)), ((), ())), preferred_element_type=jnp.float32, **kw)


def _gdn_body(q_ref, k_ref, v_ref, beta_ref, gcol_ref, grow_ref, s0_ref, o_ref, s_out_ref, s_scr):
    c = pl.program_id(2)
    L = GDN_CHUNK

    @pl.when(c == 0)
    def _():
        s_scr[...] = s0_ref[...]

    row = lax.broadcasted_iota(jnp.int32, (L, L), 0)
    col = lax.broadcasted_iota(jnp.int32, (L, L), 1)
    causal = col <= row
    strict = col < row
    eye = (col == row).astype(jnp.float32)
    b_cols = jnp.dot(causal.astype(jnp.float32), gcol_ref[...], precision=_HI, preferred_element_type=jnp.float32)
    b_rows = jnp.dot(grow_ref[...], (col >= row).astype(jnp.float32), precision=_HI,
                     preferred_element_type=jnp.float32)
    for h in range(GDN_HEADS_PER_STEP):
        lanes = slice(h * GDN_DK, (h + 1) * GDN_DK)
        qf = q_ref[:, lanes]
        kf = k_ref[:, lanes]
        q = qf * lax.rsqrt(jnp.sum(qf * qf, axis=-1, keepdims=True) + 1e-6) * GDN_DK ** -0.5
        k = kf * lax.rsqrt(jnp.sum(kf * kf, axis=-1, keepdims=True) + 1e-6)
        v = v_ref[:, lanes]
        beta = beta_ref[:, h:h + 1]
        bc = b_cols[:, h:h + 1]
        br = b_rows[h:h + 1, :]
        decay = jnp.exp(jnp.where(causal, bc - br, NEG))
        qb, kb = q.astype(jnp.bfloat16), k.astype(jnp.bfloat16)
        a_mat = jnp.where(strict, beta * _dot_nt(kb, kb) * decay, 0.0)
        inv = eye - a_mat
        pw = jnp.dot(a_mat, a_mat, precision=_HI, preferred_element_type=jnp.float32)
        for step in range(5):
            inv = inv + jnp.dot(inv, pw, precision=_HI, preferred_element_type=jnp.float32)
            if step < 4:
                pw = jnp.dot(pw, pw, precision=_HI, preferred_element_type=jnp.float32)
        s = s_scr[h]
        sb = s.astype(jnp.bfloat16)
        eb = jnp.exp(bc)
        rhs = beta * (v - eb * jnp.dot(kb, sb, preferred_element_type=jnp.float32))
        u = jnp.dot(inv, rhs, precision=_HI, preferred_element_type=jnp.float32)
        ub = u.astype(jnp.bfloat16)
        qk = (_dot_nt(qb, kb) * decay).astype(jnp.bfloat16)
        o_ref[:, lanes] = (eb * jnp.dot(qb, sb, preferred_element_type=jnp.float32)
                           + jnp.dot(qk, ub, preferred_element_type=jnp.float32))
        b_last = bc[L - 1:L, :]
        kw = (jnp.exp(b_last - bc) * k).astype(jnp.bfloat16)
        s_scr[h] = jnp.exp(b_last) * s + _dot_tn(kw, ub)

    @pl.when(c == pl.num_programs(2) - 1)
    def _():
        s_out_ref[...] = s_scr[...]


def gdn_scan_call(qkv, beta, g, s0):
    b, t, _ = qkv.shape
    hb = GDN_HEADS_PER_STEP
    ng = GDN_HEADS // hb
    L = GDN_CHUNK
    w = hb * GDN_DK

    def qkv_spec(part):
        return pl.BlockSpec((None, L, w), lambda i, j, c: (i, c, part * ng + j))

    def per_chunk(z):
        return z.reshape(b, t // L, L, ng, hb).transpose(0, 3, 1, 2, 4)

    col_spec = pl.BlockSpec((None, None, None, L, hb), lambda i, j, c: (i, j, c, 0, 0))
    state_spec = pl.BlockSpec((None, hb, GDN_DK, GDN_DV), lambda i, j, c: (i, j, 0, 0))
    return pl.pallas_call(
        _gdn_body,
        out_shape=(jax.ShapeDtypeStruct((b, t, GDN_HEADS * GDN_DV), jnp.float32),
                   jax.ShapeDtypeStruct((b, GDN_HEADS, GDN_DK, GDN_DV), jnp.float32)),
        grid=(b, ng, t // L),
        in_specs=[qkv_spec(0), qkv_spec(1), qkv_spec(2), col_spec, col_spec,
                  pl.BlockSpec((None, None, None, hb, L), lambda i, j, c: (i, j, c, 0, 0)),
                  state_spec],
        out_specs=(pl.BlockSpec((None, L, w), lambda i, j, c: (i, c, j)), state_spec),
        scratch_shapes=[pltpu.VMEM((hb, GDN_DK, GDN_DV), jnp.float32)],
        compiler_params=pltpu.CompilerParams(dimension_semantics=("arbitrary", "arbitrary", "arbitrary"),
                                             vmem_limit_bytes=VMEM_LIMIT_BYTES),
        name="gdn_scan",
    )(qkv, qkv, qkv, per_chunk(beta), per_chunk(g), per_chunk(g).swapaxes(3, 4), s0)


def gdn_mixer(h, w_in, w_conv, a_log, dt_bias, g_out, w_out, S0, conv0):
    b, t, _ = h.shape
    qkv, beta, a, gate = jnp.split(h @ w_in, [GDN_CONV_DIM, GDN_CONV_DIM + GDN_HEADS, GDN_CONV_DIM + 2 * GDN_HEADS], axis=-1)
    xc = jnp.concatenate([conv0.astype(qkv.dtype), qkv], axis=1)
    conv = lax.conv_general_dilated(xc, w_conv[:, None, :].astype(xc.dtype), (1,), 'VALID',
                                    dimension_numbers=('NWC', 'WIO', 'NWC'), feature_group_count=GDN_CONV_DIM)
    act = jax.nn.silu(conv)
    beta = jax.nn.sigmoid(beta.astype(jnp.float32))
    g = -jnp.exp(a_log.astype(jnp.float32)) * jax.nn.softplus((a + dt_bias).astype(jnp.float32))
    pad = (-t) % GDN_CHUNK
    if pad:
        act, beta, g = (jnp.pad(z, ((0, 0), (0, pad), (0, 0))) for z in (act, beta, g))
    o, S = gdn_scan_call(act, beta, g, S0.astype(jnp.float32))
    o = o[:, :t].reshape(b, t, GDN_HEADS, GDN_DV)
    o = rmsnorm(o, g_out).astype(h.dtype).reshape(b, t, GDN_HEADS * GDN_DV)
    return (o * jax.nn.silu(gate)) @ w_out, S.astype(h.dtype), xc[:, -(GDN_CONV - 1):]


def mla_project(h, pos, w_in, g_qa, g_kva, w_uq, g_qk):
    b, t, _ = h.shape
    cq, ckv, kr = jnp.split(h @ w_in, [MLA_Q_LORA, MLA_Q_LORA + MLA_KV_LORA], axis=-1)
    q = (rmsnorm(cq, g_qa) @ w_uq).reshape(b, t, MLA_HEADS, MLA_NOPE + MLA_ROPE)
    q_nope = rmsnorm(q[..., :MLA_NOPE], g_qk[0, :MLA_NOPE])
    q_rope = rope(rmsnorm(q[..., MLA_NOPE:], g_qk[0, MLA_NOPE:]), pos)
    k_rope = rope(rmsnorm(kr, g_qk[1, MLA_NOPE:]), pos)
    latent = jnp.concatenate([rmsnorm(ckv, g_kva), k_rope], axis=-1)
    return q_nope, q_rope, latent


def mla_scores(q_nope, q_rope, k_nope, k_rope):
    s = jnp.einsum('bthd,bshd->bhts', q_nope, k_nope) + jnp.einsum('bthr,bsr->bhts', q_rope, k_rope)
    return s.astype(jnp.float32) * MLA_SCALE


def mla_keys(latent, w_uk, g_kn):
    ckv = latent[..., :MLA_KV_LORA]
    return ckv, rmsnorm(jnp.einsum('bsc,chd->bshd', ckv, w_uk), g_kn), latent[..., MLA_KV_LORA:]


def mla_prompt(h, w_in, g_qa, g_kva, w_uq, w_uk, w_uv, g_qk, w_out):
    b, t, _ = h.shape
    q_nope, q_rope, latent = mla_project(h, jnp.arange(t, dtype=jnp.float32), w_in, g_qa, g_kva, w_uq, g_qk)
    ckv, k_nope, k_rope = mla_keys(latent, w_uk, g_qk[1, :MLA_NOPE])
    v = jnp.einsum('bsc,chd->bshd', ckv, w_uv)
    kpos = jnp.arange(t)

    def block(i):
        t0 = i * Q_BLOCK
        qn = lax.dynamic_slice_in_dim(q_nope, t0, Q_BLOCK, axis=1)
        qr = lax.dynamic_slice_in_dim(q_rope, t0, Q_BLOCK, axis=1)
        mask = kpos[None, :] <= (t0 + jnp.arange(Q_BLOCK))[:, None]
        p = masked_softmax(mla_scores(qn, qr, k_nope, k_rope), mask)
        return jnp.einsum('bhts,bshd->bthd', p.astype(v.dtype), v)

    o = jnp.moveaxis(lax.map(block, jnp.arange(t // Q_BLOCK)), 0, 1).reshape(b, t, MLA_HEADS * MLA_VD)
    return o @ w_out, latent


def mla_sample(h, pool, li, page_table, w_in, g_qa, g_kva, w_uq, w_uk, w_uv, g_qk, w_out):
    b, t, _ = h.shape
    q_nope, q_rope, latent = mla_project(h, PAST_LEN + jnp.arange(t, dtype=jnp.float32), w_in, g_qa, g_kva, w_uq, g_qk)
    g_kn = g_qk[1, :MLA_NOPE]

    def update(carry, s, ckv):
        m, l, acc = carry
        m_new = jnp.maximum(m, jnp.max(s, axis=-1))
        corr = jnp.exp(m - m_new)
        p = jnp.exp(s - m_new[..., None])
        return (m_new, l * corr + jnp.sum(p, axis=-1),
                acc * corr[..., None] + jnp.einsum('bhts,bsc->bhtc', p, ckv.astype(jnp.float32)))

    def page_step(carry, p):
        ckv, k_nope, k_rope = mla_keys(pool[li, page_table[:, p]], w_uk, g_kn)
        return update(carry, mla_scores(q_nope, q_rope, k_nope, k_rope), ckv), None

    init = (jnp.full((b, MLA_HEADS, t), NEG, jnp.float32), jnp.zeros((b, MLA_HEADS, t), jnp.float32),
            jnp.zeros((b, MLA_HEADS, t, MLA_KV_LORA), jnp.float32))
    carry, _ = lax.scan(page_step, init, jnp.arange(page_table.shape[1]))
    ckv, k_nope, k_rope = mla_keys(latent, w_uk, g_kn)
    s_new = jnp.where(jnp.tril(jnp.ones((t, t), bool)), mla_scores(q_nope, q_rope, k_nope, k_rope), NEG)
    _, l, acc = update(carry, s_new, ckv)
    o = jnp.einsum('bhtc,chd->bthd', acc / l[..., None], w_uv.astype(jnp.float32))
    return o.astype(h.dtype).reshape(b, t, MLA_HEADS * MLA_VD) @ w_out, latent


def nsa_project(h, w_in, b_gate, g_qk):
    b, t, _ = h.shape
    qw, kvw = NSA_HEADS * NSA_HD, 6 * NSA_GROUPS * NSA_HD
    q, kv, gates = jnp.split(h @ w_in, [qw, qw + kvw], axis=-1)
    q = rmsnorm(q.reshape(b, t, NSA_GROUPS, NSA_HPG, NSA_HD), g_qk[0])
    kv = kv.reshape(b, t, 6, NSA_GROUPS, NSA_HD)
    rows = jnp.stack([kv[:, :, 0], kv[:, :, 1], rmsnorm(kv[:, :, 2], g_qk[1]), kv[:, :, 3]], axis=2)
    win = jnp.stack([rmsnorm(kv[:, :, 4], g_qk[1]), kv[:, :, 5]], axis=2)
    gates = jax.nn.sigmoid(gates + b_gate).reshape(b, t, NSA_GROUPS, NSA_HPG, 3)
    return q, rows, win, gates


def nsa_compress(kv_rows, w_cmp, g_k):
    b, l = kv_rows.shape[:2]
    blk = kv_rows.reshape(b, l // NSA_BLOCK, NSA_BLOCK, 2, NSA_GROUPS, NSA_HD)
    c = jnp.einsum('bnpcgd,cpde->bncge', blk, w_cmp)
    return rmsnorm(c[:, :, 0], g_k), c[:, :, 1]


def nsa_cmp_attn(q, k_c, v_c, qpos):
    nb = k_c.shape[1]
    s = jnp.einsum('btghd,bngd->bghtn', q, k_c) * NSA_SCALE
    mask = ((jnp.arange(nb) + 1) * NSA_BLOCK - 1)[None, :] <= qpos[:, None]
    p = masked_softmax(s, mask)
    o = jnp.einsum('bghtn,bngd->btghd', p.astype(v_c.dtype), v_c)
    return o, jnp.sum(p, axis=2)


def nsa_select(imp, qpos):
    nb = imp.shape[-1]
    cand = jnp.arange(nb)[None, :] < (qpos // NSA_BLOCK)[:, None]
    val, idx = lax.top_k(jnp.where(cand, imp, -jnp.inf), min(NSA_TOPK, nb))
    return idx, jnp.isfinite(val)


def nsa_sel_prompt(q, k_s, v_s, idx, valid):
    b, t = q.shape[:2]
    nb = t // NSA_BLOCK
    kb = k_s.reshape(b, nb, NSA_BLOCK, NSA_GROUPS, NSA_HD).transpose(0, 3, 1, 2, 4)
    vb = v_s.reshape(b, nb, NSA_BLOCK, NSA_GROUPS, NSA_HD).transpose(0, 3, 1, 2, 4)
    bi = jnp.arange(b)[:, None, None, None]
    gi = jnp.arange(NSA_GROUPS)[None, None, :, None]
    idx, valid = idx.transpose(0, 2, 1, 3), valid.transpose(0, 2, 1, 3)

    def block(i):
        t0 = i * NSA_QB
        qpos = t0 + jnp.arange(NSA_QB)
        qq = lax.dynamic_slice_in_dim(q, t0, NSA_QB, axis=1)
        cur = jnp.broadcast_to((qpos // NSA_BLOCK)[None, :, None, None], (b, NSA_QB, NSA_GROUPS, 1))
        ii = jnp.concatenate([lax.dynamic_slice_in_dim(idx, t0, NSA_QB, axis=1), cur], axis=-1)
        ok = jnp.concatenate([lax.dynamic_slice_in_dim(valid, t0, NSA_QB, axis=1), jnp.ones(cur.shape, bool)], axis=-1)
        kpos = ii[..., None] * NSA_BLOCK + jnp.arange(NSA_BLOCK)
        mask = ok[..., None] & (kpos <= qpos[None, :, None, None, None])
        n = ii.shape[-1] * NSA_BLOCK
        ks, vs = kb[bi, gi, ii], vb[bi, gi, ii]
        s = jnp.einsum('btghd,btgkpd->bghtkp', qq, ks).reshape(b, NSA_GROUPS, NSA_HPG, NSA_QB, n) * NSA_SCALE
        mask = mask.transpose(0, 2, 1, 3, 4).reshape(b, NSA_GROUPS, 1, NSA_QB, n)
        p = masked_softmax(s, mask)
        return jnp.einsum('bghtn,btgnd->btghd', p.astype(vs.dtype), vs.reshape(b, NSA_QB, NSA_GROUPS, n, NSA_HD))

    return jnp.moveaxis(lax.map(block, jnp.arange(t // NSA_QB)), 0, 1).reshape(q.shape)


def nsa_sel_sample(q, k_new, v_new, idx, valid, pool, li, page_table):
    b, t = q.shape[:2]
    bpp = PAGE_SIZE // NSA_BLOCK
    kk = idx.shape[-1]
    n = kk * NSA_BLOCK
    ii, ok = idx.transpose(0, 2, 1, 3), valid.transpose(0, 2, 1, 3)
    phys = page_table[jnp.arange(b)[:, None, None, None], ii // bpp][..., None]
    rows = (ii % bpp)[..., None] * NSA_BLOCK + jnp.arange(NSA_BLOCK)
    gi = jnp.arange(NSA_GROUPS)[None, None, :, None, None]
    ks = pool[li, phys, rows, 2, gi].reshape(b, t, NSA_GROUPS, n, NSA_HD)
    vs = pool[li, phys, rows, 3, gi].reshape(b, t, NSA_GROUPS, n, NSA_HD)
    s = jnp.concatenate([jnp.einsum('btghd,btgnd->bghtn', q, ks), jnp.einsum('btghd,bsgd->bghts', q, k_new)], axis=-1)
    mask_past = jnp.repeat(ok, NSA_BLOCK, axis=-1).transpose(0, 2, 1, 3)[:, :, None]
    mask_new = jnp.broadcast_to(jnp.tril(jnp.ones((t, t), bool)), (b, NSA_GROUPS, 1, t, t))
    p = masked_softmax(s * NSA_SCALE, jnp.concatenate([mask_past, mask_new], axis=-1)).astype(v_new.dtype)
    return (jnp.einsum('bghtn,btgnd->btghd', p[..., :n], vs)
            + jnp.einsum('bghts,bsgd->btghd', p[..., n:], v_new))


def nsa_win_prompt(q, k_w, v_w):
    t = q.shape[1]
    pad = ((0, 0), (NSA_WINDOW, 0), (0, 0), (0, 0))
    kp, vp = jnp.pad(k_w, pad), jnp.pad(v_w, pad)
    span = NSA_WINDOW + Q_BLOCK

    def block(i):
        t0 = i * Q_BLOCK
        qq = lax.dynamic_slice_in_dim(q, t0, Q_BLOCK, axis=1)
        kk = lax.dynamic_slice_in_dim(kp, t0, span, axis=1)
        vv = lax.dynamic_slice_in_dim(vp, t0, span, axis=1)
        qpos = t0 + jnp.arange(Q_BLOCK)
        kpos = t0 - NSA_WINDOW + jnp.arange(span)
        mask = ((kpos[None, :] <= qpos[:, None]) & (kpos[None, :] > qpos[:, None] - NSA_WINDOW)
                & (kpos[None, :] >= 0))
        p = masked_softmax(jnp.einsum('bqghd,bkgd->bghqk', qq, kk) * NSA_SCALE, mask)
        return jnp.einsum('bghqk,bkgd->bqghd', p.astype(vv.dtype), vv)

    return jnp.moveaxis(lax.map(block, jnp.arange(t // Q_BLOCK)), 0, 1).reshape(q.shape)


def nsa_win_sample(q, win, buf):
    t = q.shape[1]
    wb = buf.shape[1]
    kv = jnp.concatenate([buf.astype(win.dtype), win], axis=1)
    qpos = PAST_LEN + jnp.arange(t)
    kpos = PAST_LEN - wb + jnp.arange(wb + t)
    mask = (kpos[None, :] <= qpos[:, None]) & (kpos[None, :] > qpos[:, None] - NSA_WINDOW)
    p = masked_softmax(jnp.einsum('btghd,bsgd->bghts', q, kv[:, :, 0]) * NSA_SCALE, mask)
    o = jnp.einsum('bghts,bsgd->btghd', p.astype(kv.dtype), kv[:, :, 1])
    return o, kv[:, -wb:]


def nsa_merge(gates, o_cmp, o_sel, o_win):
    o = gates[..., 0:1] * o_cmp + gates[..., 1:2] * o_sel + gates[..., 2:3] * o_win
    return o.reshape(o.shape[0], o.shape[1], NSA_HEADS * NSA_HD)


def _nsa_flash(q_all, k_scr, v_scr, lanes, kt_lo, kt_hi, mask_fn, m_scr, l_scr, acc_scr):
    rows = q_all.shape[0]
    m_scr[...] = jnp.full((rows, 1), NEG, jnp.float32)
    l_scr[...] = jnp.zeros((rows, 1), jnp.float32)
    acc_scr[...] = jnp.zeros((rows, NSA_HD), jnp.float32)

    def body(kt, carry):
        k0 = pl.multiple_of(kt * NSA_TK, NSA_TK)
        kk = k_scr[pl.ds(k0, NSA_TK), lanes]
        vv = v_scr[pl.ds(k0, NSA_TK), lanes]
        s = _dot_nt(q_all, kk)
        s = jnp.where(mask_fn(k0)[None], s.reshape(NSA_HPG, NSA_TQ, NSA_TK), 2.0 * NEG).reshape(rows, NSA_TK)
        m_old = m_scr[...]
        m_new = jnp.maximum(m_old, jnp.max(s, axis=-1, keepdims=True))
        corr = jnp.exp(m_old - m_new)
        p = jnp.exp(s - m_new)
        l_scr[...] = l_scr[...] * corr + jnp.sum(p, axis=-1, keepdims=True)
        acc_scr[...] = acc_scr[...] * corr + jnp.dot(p.astype(jnp.bfloat16), vv, preferred_element_type=jnp.float32)
        m_scr[...] = m_new
        return carry

    lax.fori_loop(kt_lo, kt_hi, body, 0)
    return acc_scr[...] / l_scr[...]


def _nsa_body(q_ref, rows_ref, win_ref, kc_ref, vc_ref, gates_ref, o_ref,
              ksel_scr, vsel_scr, kwin_scr, vwin_scr, m_scr, l_scr, acc_scr):
    qi = pl.program_id(1)
    nb = kc_ref.shape[0]
    gw = NSA_GROUPS * NSA_HD

    @pl.when(qi == 0)
    def _():
        ksel_scr[...] = rows_ref[:, 2 * gw:3 * gw].astype(jnp.bfloat16)
        vsel_scr[...] = rows_ref[:, 3 * gw:4 * gw].astype(jnp.bfloat16)
        kwin_scr[...] = win_ref[:, 0:gw].astype(jnp.bfloat16)
        vwin_scr[...] = win_ref[:, gw:2 * gw].astype(jnp.bfloat16)

    t0 = qi * NSA_TQ
    tpos = t0 + lax.broadcasted_iota(jnp.int32, (NSA_TQ, 1), 0)
    blk_t = tpos // NSA_BLOCK
    nidx = lax.broadcasted_iota(jnp.int32, (1, nb), 1)
    rows = NSA_HPG * NSA_TQ
    gq = NSA_HPG * NSA_HD
    for g in range(NSA_GROUPS):
        lanes = slice(g * NSA_HD, (g + 1) * NSA_HD)
        qg = q_ref[:, g * gq:(g + 1) * gq] * NSA_SCALE
        q_all = jnp.concatenate([qg[:, h * NSA_HD:(h + 1) * NSA_HD] for h in range(NSA_HPG)],
                                axis=0).astype(jnp.bfloat16)

        kc = kc_ref[:, lanes].astype(jnp.bfloat16)
        vc = vc_ref[:, lanes].astype(jnp.bfloat16)
        cmask = ((nidx + 1) * NSA_BLOCK - 1) <= tpos
        s = jnp.where(cmask[None], _dot_nt(q_all, kc).reshape(NSA_HPG, NSA_TQ, nb), NEG)
        e = jnp.exp(s - jnp.max(s, axis=-1, keepdims=True))
        p = e / jnp.sum(e, axis=-1, keepdims=True) * cmask[None].astype(jnp.float32)
        imp = jnp.sum(p, axis=0)
        o_cmp = jnp.dot(p.reshape(rows, nb).astype(jnp.bfloat16), vc, preferred_element_type=jnp.float32)

        rank = jnp.zeros((NSA_TQ, nb), jnp.int32)
        for m in range(nb):
            col = imp[:, m:m + 1]
            beats = (col > imp) | ((col == imp) & (m < nidx))
            rank = rank + jnp.where(beats & (m < blk_t), 1, 0)
        sel_f = (((nidx < blk_t) & (rank < NSA_TOPK)) | (nidx == blk_t)).astype(jnp.float32)
        blk_id = lax.broadcasted_iota(jnp.int32, (nb, NSA_TK), 0)

        def sel_mask(k0):
            spos = k0 + lax.broadcasted_iota(jnp.int32, (1, NSA_TK), 1)
            expand = (blk_id == spos // NSA_BLOCK).astype(jnp.float32)
            chosen = jnp.dot(sel_f, expand, preferred_element_type=jnp.float32) > 0.5
            return chosen & (spos <= tpos)

        def win_mask(k0):
            spos = k0 + lax.broadcasted_iota(jnp.int32, (1, NSA_TK), 1)
            return (spos <= tpos) & (spos > tpos - NSA_WINDOW)

        kt_hi = (t0 + NSA_TQ + NSA_TK - 1) // NSA_TK
        o_sel = _nsa_flash(q_all, ksel_scr, vsel_scr, lanes, 0, kt_hi, sel_mask, m_scr, l_scr, acc_scr)
        kt_lo = jnp.maximum(t0 - (NSA_WINDOW - 1), 0) // NSA_TK
        o_win = _nsa_flash(q_all, kwin_scr, vwin_scr, lanes, kt_lo, kt_hi, win_mask, m_scr, l_scr, acc_scr)

        outs = []
        for h in range(NSA_HPG):
            r = slice(h * NSA_TQ, (h + 1) * NSA_TQ)
            c = (g * NSA_HPG + h) * 3
            outs.append(gates_ref[:, c:c + 1] * o_cmp[r] + gates_ref[:, c + 1:c + 2] * o_sel[r]
                        + gates_ref[:, c + 2:c + 3] * o_win[r])
        o_ref[:, g * gq:(g + 1) * gq] = jnp.concatenate(outs, axis=1)


def nsa_prompt_attn(q, rows, win, k_c, v_c, gates):
    b, t, _ = q.shape
    nb = t // NSA_BLOCK
    assert t % NSA_TK == 0 and t % NSA_TQ == 0
    gw = NSA_GROUPS * NSA_HD
    qw = NSA_HEADS * NSA_HD
    return pl.pallas_call(
        _nsa_body,
        out_shape=jax.ShapeDtypeStruct((b, t, qw), jnp.float32),
        grid=(b, t // NSA_TQ),
        in_specs=[pl.BlockSpec((None, NSA_TQ, qw), lambda i, j: (i, j, 0)),
                  pl.BlockSpec((None, t, 4 * gw), lambda i, j: (i, 0, 0)),
                  pl.BlockSpec((None, t, 2 * gw), lambda i, j: (i, 0, 0)),
                  pl.BlockSpec((None, nb, gw), lambda i, j: (i, 0, 0)),
                  pl.BlockSpec((None, nb, gw), lambda i, j: (i, 0, 0)),
                  pl.BlockSpec((None, NSA_TQ, NSA_HEADS * 3), lambda i, j: (i, j, 0))],
        out_specs=pl.BlockSpec((None, NSA_TQ, qw), lambda i, j: (i, j, 0)),
        scratch_shapes=[pltpu.VMEM((t, gw), jnp.bfloat16)] * 4
                       + [pltpu.VMEM((NSA_HPG * NSA_TQ, 1), jnp.float32)] * 2
                       + [pltpu.VMEM((NSA_HPG * NSA_TQ, NSA_HD), jnp.float32)],
        compiler_params=pltpu.CompilerParams(dimension_semantics=("arbitrary", "arbitrary"),
                                             vmem_limit_bytes=VMEM_LIMIT_BYTES),
        name="nsa_prompt_attn",
    )(q, rows, win, k_c, v_c, gates)


def nsa_prompt(h, w_in, b_gate, g_qk, w_cmp, w_out):
    b, t, _ = h.shape
    q, rows, win, gates = nsa_project(h, w_in, b_gate, g_qk)
    k_c, v_c = nsa_compress(rows[:, :, :2], w_cmp, g_qk[1])
    nb = t // NSA_BLOCK
    o = nsa_prompt_attn(q.reshape(b, t, -1), rows.reshape(b, t, -1), win.reshape(b, t, -1),
                        k_c.reshape(b, nb, -1), v_c.reshape(b, nb, -1), gates.reshape(b, t, -1))
    return o @ w_out, rows, win[:, -min(NSA_WINDOW, t):]


def nsa_sample(h, pool, li, page_table, win_buf, w_in, b_gate, g_qk, w_cmp, w_out):
    b, t, _ = h.shape
    q, rows, win, gates = nsa_project(h, w_in, b_gate, g_qk)
    qpos = PAST_LEN + jnp.arange(t)

    def page_cmp(p):
        return nsa_compress(pool[li, page_table[:, p], :, :2], w_cmp, g_qk[1])

    k_c, v_c = lax.map(page_cmp, jnp.arange(page_table.shape[1]))
    k_c = jnp.moveaxis(k_c, 0, 1).reshape(b, -1, NSA_GROUPS, NSA_HD)
    v_c = jnp.moveaxis(v_c, 0, 1).reshape(b, -1, NSA_GROUPS, NSA_HD)
    o_cmp, imp = nsa_cmp_attn(q, k_c, v_c, qpos)
    idx, valid = nsa_select(imp, qpos)
    o_sel = nsa_sel_sample(q, rows[:, :, 2], rows[:, :, 3], idx, valid, pool, li, page_table)
    o_win, new_buf = nsa_win_sample(q, win, win_buf)
    return nsa_merge(gates, o_cmp, o_sel, o_win) @ w_out, rows, new_buf


def kernel(x_prompt, x_sample, state_mlstm_C, state_mlstm_n, state_mlstm_m, cache_mla, cache_nsa,
           state_nsa_win, state_gdn_S, state_gdn_conv, page_table, c_prompt, c_sample,
           w_ada, b_ada, g_norm, w_ffn_up, w_ffn_down,
           w_ml_in, b_ml_gate, g_ml_hn, w_ml_out,
           w_mla_in, g_mla_qa, g_mla_kva, w_mla_uq, w_mla_uk, w_mla_uv, g_mla_qk, w_mla_out,
           w_nsa_in, b_nsa_gate, g_nsa_qk, w_nsa_cmp, w_nsa_out,
           w_gdn_in, w_gdn_conv, a_gdn_log, b_gdn_dt, g_gdn_out, w_gdn_out):
    bp = x_prompt.shape[0]
    xp, xs = x_prompt, x_sample
    ml_p, ml_s, mla_p, mla_s, nsa_p, nsa_s, gdn_p, gdn_s = [], [], [], [], [], [], [], []
    w_up_bf = w_ffn_up.astype(jnp.bfloat16)
    w_dn_bf = w_ffn_down.astype(jnp.bfloat16)
    for i in range(DEPTH):
        kind, li = i % N_MIXERS, i // N_MIXERS
        mod_p = adaln(c_prompt, w_ada[i], b_ada[i])
        mod_s = adaln(c_sample, w_ada[i], b_ada[i])
        xp = ffn_half_prompt(xp, g_norm[i, 0], mod_p, 0, w_up_bf[i, 0], w_dn_bf[i, 0])
        xs = ffn_half_sample(xs, g_norm[i, 0], mod_s, 0, w_up_bf[i, 0], w_dn_bf[i, 0])
        hp = modulate(xp, g_norm[i, 1], mod_p, 1)
        hs = modulate(xs, g_norm[i, 1], mod_s, 1)
        if kind == 0:
            wts = (w_ml_in[li], b_ml_gate[li], g_ml_hn[li], w_ml_out[li])
            op, c_p, n_p, m_p = mlstm_mixer(hp, *wts, jnp.zeros((bp, ML_HEADS, ML_DQK, ML_DV), jnp.float32),
                                            jnp.zeros((bp, ML_HEADS, ML_DQK), jnp.float32),
                                            jnp.zeros((bp, ML_HEADS), jnp.float32))
            os_, c_s, n_s, m_s = mlstm_mixer(hs, *wts, state_mlstm_C[li], state_mlstm_n[li], state_mlstm_m[li])
            ml_p.append((c_p, n_p, m_p))
            ml_s.append((c_s, n_s, m_s))
        elif kind == 1:
            wts = (w_mla_in[li], g_mla_qa[li], g_mla_kva[li], w_mla_uq[li], w_mla_uk[li], w_mla_uv[li],
                   g_mla_qk[li], w_mla_out[li])
            op, r_p = mla_prompt(hp, *wts)
            os_, r_s = mla_sample(hs, cache_mla, li, page_table, *wts)
            mla_p.append((r_p,))
            mla_s.append((r_s,))
        elif kind == 2:
            wts = (w_nsa_in[li], b_nsa_gate[li], g_nsa_qk[li], w_nsa_cmp[li], w_nsa_out[li])
            op, r_p, wb_p = nsa_prompt(hp, *wts)
            os_, r_s, wb_s = nsa_sample(hs, cache_nsa, li, page_table, state_nsa_win[li], *wts)
            nsa_p.append((r_p, wb_p))
            nsa_s.append((r_s, wb_s))
        else:
            wts = (w_gdn_in[li], w_gdn_conv[li], a_gdn_log[li], b_gdn_dt[li], g_gdn_out[li], w_gdn_out[li])
            op, s_p, cv_p = gdn_mixer(hp, *wts, jnp.zeros((bp, GDN_HEADS, GDN_DK, GDN_DV), jnp.float32),
                                      jnp.zeros((bp, GDN_CONV - 1, GDN_CONV_DIM), hp.dtype))
            os_, s_s, cv_s = gdn_mixer(hs, *wts, state_gdn_S[li], state_gdn_conv[li])
            gdn_p.append((s_p, cv_p))
            gdn_s.append((s_s, cv_s))
        xp = xp + mod_p[:, 1, 2][:, None] * op
        xs = xs + mod_s[:, 1, 2][:, None] * os_
        xp = ffn_half_prompt(xp, g_norm[i, 2], mod_p, 2, w_up_bf[i, 1], w_dn_bf[i, 1])
        xs = ffn_half_sample(xs, g_norm[i, 2], mod_s, 2, w_up_bf[i, 1], w_dn_bf[i, 1])
    return (xp, xs,
            _stack(ml_p, 0), _stack(ml_s, 0), _stack(ml_p, 1), _stack(ml_s, 1), _stack(ml_p, 2), _stack(ml_s, 2),
            _stack(mla_p, 0), _stack(mla_s, 0),
            _stack(nsa_p, 0), _stack(nsa_s, 0), _stack(nsa_p, 1), _stack(nsa_s, 1),
            _stack(gdn_p, 0), _stack(gdn_s, 0), _stack(gdn_p, 1), _stack(gdn_s, 1))
```

```python
import functools

import jax
import jax.numpy as jnp
from jax import lax
from jax.experimental import pallas as pl
from jax.experimental.pallas import tpu as pltpu

D_MODEL = 1024
BATCH = 16
SEQ = 2048
DEPTH = 4
DEC_BATCH = 128
DEC_SEQ = 4
PAST_LEN = 8192
PAGE_SIZE = 128

N_MIXERS = 4
D_FF = 2816
RMS_EPS = 1e-6
NEG = -1e30
Q_BLOCK = 128

ML_HEADS = 8
ML_DQK = 64
ML_DV = 128
ML_CHUNK = 64
ML_GATE_CAP = 15.0

MLA_HEADS = 16
MLA_NOPE = 64
MLA_ROPE = 32
MLA_VD = 64
MLA_KV_LORA = 256
MLA_Q_LORA = 384
MLA_LAT = MLA_KV_LORA + MLA_ROPE
MLA_SCALE = (MLA_NOPE + MLA_ROPE) ** -0.5
ROPE_THETA = 10000.0

NSA_HEADS = 16
NSA_GROUPS = 2
NSA_HPG = NSA_HEADS // NSA_GROUPS
NSA_HD = 64
NSA_BLOCK = 64
NSA_TOPK = 15
NSA_WINDOW = 512
NSA_QB = 32
NSA_SCALE = NSA_HD ** -0.5

GDN_HEADS = 8
GDN_DK = 128
GDN_DV = 128
GDN_CONV = 4
GDN_CHUNK = 64
GDN_CONV_DIM = GDN_HEADS * (2 * GDN_DK + GDN_DV)

VMEM_LIMIT_BYTES = 56 * 1024 * 1024
FFN_ROW_TILE = 512
FFN_COL_CHUNK = 256
GDN_HEADS_PER_STEP = 4
LANES = 128
FLASH_TK = 256
NSA_TQ = 128
MLA_TQ = 512
MLA_PAGES_PER_STEP = 8
MLA_NEW_ROWS = 256
_HI = lax.Precision.HIGHEST


def _ffn_body(x_ref, shift_ref, scale_ref, gate_ref, g_ref, wup_ref, wdn_ref, o_ref, acc_ref):
    x = x_ref[...]
    y = x * lax.rsqrt(jnp.mean(x * x, axis=-1, keepdims=True) + RMS_EPS) * g_ref[...]
    h = (y * (1.0 + scale_ref[...]) + shift_ref[...]).astype(jnp.bfloat16)
    for j in range(D_FF // FFN_COL_CHUNK):
        lo = j * FFN_COL_CHUNK
        a = jnp.dot(h, wup_ref[:, lo:lo + FFN_COL_CHUNK], preferred_element_type=jnp.float32)
        b = jnp.dot(h, wup_ref[:, D_FF + lo:D_FF + lo + FFN_COL_CHUNK], preferred_element_type=jnp.float32)
        act = (a * jax.nn.sigmoid(a) * b).astype(jnp.bfloat16)
        part = jnp.dot(act, wdn_ref[lo:lo + FFN_COL_CHUNK, :], preferred_element_type=jnp.float32)
        if j == 0:
            acc_ref[...] = part
        else:
            acc_ref[...] += part
    o_ref[...] = x + 0.5 * gate_ref[...] * acc_ref[...]


def _ffn_call(x2d, shift, scale, gate, g, w_up, w_down, rows_per_mod):
    n = x2d.shape[0]
    tm = min(FFN_ROW_TILE, n)
    assert n % tm == 0
    if rows_per_mod:
        assert rows_per_mod % tm == 0
        tiles_per_mod = rows_per_mod // tm
        mod_spec = pl.BlockSpec((None, 1, D_MODEL), lambda i: (i // tiles_per_mod, 0, 0))
    else:
        mod_spec = pl.BlockSpec((tm, D_MODEL), lambda i: (i, 0))
    resident = functools.partial(pl.BlockSpec, pipeline_mode=pl.Buffered(1))
    return pl.pallas_call(
        _ffn_body,
        out_shape=jax.ShapeDtypeStruct((n, D_MODEL), jnp.float32),
        grid=(n // tm,),
        in_specs=[
            pl.BlockSpec((tm, D_MODEL), lambda i: (i, 0)),
            mod_spec, mod_spec, mod_spec,
            resident((1, D_MODEL), lambda i: (0, 0)),
            resident((D_MODEL, 2 * D_FF), lambda i: (0, 0)),
            resident((D_FF, D_MODEL), lambda i: (0, 0)),
        ],
        out_specs=pl.BlockSpec((tm, D_MODEL), lambda i: (i, 0)),
        scratch_shapes=[pltpu.VMEM((tm, D_MODEL), jnp.float32)],
        compiler_params=pltpu.CompilerParams(dimension_semantics=("arbitrary",),
                                             vmem_limit_bytes=VMEM_LIMIT_BYTES),
        name="ffn_half",
    )(x2d, shift, scale, gate, g.reshape(1, D_MODEL), w_up, w_down)


def ffn_half_prompt(x, g, mod, j, w_up, w_down):
    b, t, d = x.shape
    m = [mod[:, j, k][:, None, :] for k in range(3)]
    return _ffn_call(x.reshape(b * t, d), m[0], m[1], m[2], g, w_up, w_down, t).reshape(b, t, d)


def ffn_half_sample(x, g, mod, j, w_up, w_down):
    b, t, d = x.shape
    m = [jnp.repeat(mod[:, j, k], t, axis=0) for k in range(3)]
    return _ffn_call(x.reshape(b * t, d), m[0], m[1], m[2], g, w_up, w_down, 0).reshape(b, t, d)


def rmsnorm(x, g):
    xf = x.astype(jnp.float32)
    y = xf * lax.rsqrt(jnp.mean(xf * xf, axis=-1, keepdims=True) + RMS_EPS)
    return (y * g.astype(jnp.float32)).astype(x.dtype)


def l2norm(x):
    xf = x.astype(jnp.float32)
    return xf * lax.rsqrt(jnp.sum(xf * xf, axis=-1, keepdims=True) + 1e-6)


def rope(x, pos):
    half = x.shape[-1] // 2
    freqs = ROPE_THETA ** (-jnp.arange(half, dtype=jnp.float32) / half)
    ang = pos[:, None] * freqs[None, :]
    ang = ang.reshape((ang.shape[0],) + (1,) * (x.ndim - 3) + (half,))
    cos, sin = jnp.cos(ang), jnp.sin(ang)
    xf = x.astype(jnp.float32)
    x1, x2 = xf[..., :half], xf[..., half:]
    return jnp.concatenate([x1 * cos - x2 * sin, x2 * cos + x1 * sin], axis=-1).astype(x.dtype)


def masked_softmax(s, mask):
    return jax.nn.softmax(jnp.where(mask, s.astype(jnp.float32), NEG), axis=-1) * mask


def adaln(c, w, b):
    return (jax.nn.silu(c) @ w + b).reshape(c.shape[0], 3, 3, c.shape[1])


def modulate(x, g, mod, j):
    return rmsnorm(x, g) * (1.0 + mod[:, j, 1][:, None]) + mod[:, j, 0][:, None]


def _chunks(a, chunk):
    b, h, t = a.shape[:3]
    return jnp.moveaxis(a.reshape((b, h, t // chunk, chunk) + a.shape[3:]), 2, 0)


def _unchunk(a):
    a = jnp.moveaxis(a, 0, 2)
    return a.reshape(a.shape[:2] + (a.shape[2] * a.shape[3],) + a.shape[4:])


def _stack(entries, j):
    return jnp.stack([e[j] for e in entries])


def mlstm_scan(q, k, v, ig, lf, C0, n0, m0, chunk):
    causal = jnp.tril(jnp.ones((chunk, chunk), bool))

    def step(carry, inp):
        C, n, m = carry
        qc, kc, vc, ic, fc = inp
        b = jnp.cumsum(fc, axis=-1)
        dmat = jnp.where(causal, b[..., :, None] - b[..., None, :] + ic[..., None, :], NEG)
        inter = b + m[..., None]
        mt = jnp.maximum(inter, jnp.max(dmat, axis=-1))
        w_intra = jnp.exp(dmat - mt[..., None])
        w_inter = jnp.exp(inter - mt)
        s = jnp.einsum('bhld,bhsd->bhls', qc, kc) * w_intra
        num = w_inter[..., None] * jnp.einsum('bhld,bhdv->bhlv', qc, C) + jnp.einsum('bhls,bhsv->bhlv', s, vc)
        den = w_inter * jnp.einsum('bhld,bhd->bhl', qc, n) + jnp.sum(s, axis=-1)
        hc = num / jnp.maximum(jnp.abs(den), jnp.exp(-mt))[..., None]
        m_new = mt[..., -1]
        w_state = jnp.exp(b[..., -1:] - b + ic - m_new[..., None])
        decay = jnp.exp(b[..., -1] + m - m_new)
        C_new = decay[..., None, None] * C + jnp.einsum('bhs,bhsd,bhsv->bhdv', w_state, kc, vc)
        n_new = decay[..., None] * n + jnp.einsum('bhs,bhsd->bhd', w_state, kc)
        return (C_new, n_new, m_new), hc

    xs = tuple(_chunks(a, chunk) for a in (q, k, v, ig, lf))
    (C, n, m), h = lax.scan(step, (C0, n0, m0), xs)
    return _unchunk(h), C, n, m


def mlstm_mixer(h, w_in, b_gate, g_hn, w_out, C0, n0, m0):
    b, t, _ = h.shape
    qk_w, v_w = ML_HEADS * ML_DQK, ML_HEADS * ML_DV
    q, k, v, o, gates = jnp.split(h @ w_in, [qk_w, 2 * qk_w, 2 * qk_w + v_w, 2 * qk_w + 2 * v_w], axis=-1)

    def heads(a, d):
        return a.reshape(b, t, ML_HEADS, d).transpose(0, 2, 1, 3).astype(jnp.float32)

    gates = ML_GATE_CAP * jnp.tanh((gates + b_gate).astype(jnp.float32) / ML_GATE_CAP)
    ig = gates[..., :ML_HEADS].transpose(0, 2, 1)
    lf = jax.nn.log_sigmoid(gates[..., ML_HEADS:]).transpose(0, 2, 1)
    hh, C, n, m = mlstm_scan(heads(q, ML_DQK) * ML_DQK ** -0.5, heads(k, ML_DQK), heads(v, ML_DV), ig, lf,
                             C0.astype(jnp.float32), n0.astype(jnp.float32), m0.astype(jnp.float32),
                             min(ML_CHUNK, t))
    hh = rmsnorm(hh.transpose(0, 2, 1, 3), g_hn).astype(h.dtype).reshape(b, t, v_w)
    return (jax.nn.sigmoid(o) * hh) @ w_out, C.astype(h.dtype), n.astype(h.dtype), m.astype(h.dtype)


def _dot_nt(a, b, **kw):
    return lax.dot_general(a, b, (((1,), (1,)), ((), ())), preferred_element_type=jnp.float32, **kw)


def _dot_tn(a, b, **kw):
    return lax.dot_general(a, b, (((0,), (0,)), ((), ())), preferred_element_type=jnp.float32, **kw)


def _flash_init(m_scr, acc_scr):
    m_scr[...] = jnp.full(m_scr.shape, NEG, jnp.float32)
    acc_scr[...] = jnp.zeros(acc_scr.shape, jnp.float32)


def _flash_pass(q_all, k_ref, v_ref, kt_lo, kt_hi, mask_fn, heads, m_scr, acc_scr):
    rows = q_all.shape[0]
    tq = rows // heads

    def body(kt, carry):
        k0 = pl.multiple_of(kt * FLASH_TK, FLASH_TK)
        s = _dot_nt(q_all, k_ref[pl.ds(k0, FLASH_TK), :])
        if mask_fn is not None:
            s = jnp.where(mask_fn(k0)[None], s.reshape(heads, tq, FLASH_TK), 2.0 * NEG).reshape(rows, FLASH_TK)
        m_old = m_scr[...]
        m_new = jnp.maximum(m_old, jnp.max(s, axis=-1, keepdims=True))
        p = jnp.exp(s - jnp.concatenate([m_new] * (FLASH_TK // LANES), axis=1))
        acc_scr[...] = acc_scr[...] * jnp.exp(m_old - m_new) + jnp.dot(
            p.astype(jnp.bfloat16), v_ref[pl.ds(k0, FLASH_TK), :], preferred_element_type=jnp.float32)
        m_scr[...] = m_new
        return carry

    lax.fori_loop(kt_lo, kt_hi, body, 0)


def _flash_result(acc_scr, dv):
    acc = acc_scr[...]
    return acc[:, :dv] / acc[:, dv:dv + 1]


def _gdn_body(q_ref, k_ref, v_ref, beta_ref, gcol_ref, grow_ref, s0_ref, o_ref, s_out_ref, s_scr):
    c = pl.program_id(2)
    L = GDN_CHUNK

    @pl.when(c == 0)
    def _():
        s_scr[...] = s0_ref[...]

    row = lax.broadcasted_iota(jnp.int32, (L, L), 0)
    col = lax.broadcasted_iota(jnp.int32, (L, L), 1)
    causal = col <= row
    strict = col < row
    eye = (col == row).astype(jnp.float32)
    b_cols = jnp.dot(causal.astype(jnp.float32), gcol_ref[...], precision=_HI, preferred_element_type=jnp.float32)
    b_rows = jnp.dot(grow_ref[...], (col >= row).astype(jnp.float32), precision=_HI,
                     preferred_element_type=jnp.float32)
    for h in range(GDN_HEADS_PER_STEP):
        lanes = slice(h * GDN_DK, (h + 1) * GDN_DK)
        qf = q_ref[:, lanes]
        kf = k_ref[:, lanes]
        q = qf * lax.rsqrt(jnp.sum(qf * qf, axis=-1, keepdims=True) + 1e-6) * GDN_DK ** -0.5
        k = kf * lax.rsqrt(jnp.sum(kf * kf, axis=-1, keepdims=True) + 1e-6)
        v = v_ref[:, lanes]
        beta = beta_ref[:, h:h + 1]
        bc = b_cols[:, h:h + 1]
        br = b_rows[h:h + 1, :]
        decay = jnp.exp(jnp.where(causal, bc - br, NEG))
        qb, kb = q.astype(jnp.bfloat16), k.astype(jnp.bfloat16)
        a_mat = jnp.where(strict, beta * _dot_nt(kb, kb) * decay, 0.0)
        inv = eye - a_mat
        pw = jnp.dot(a_mat, a_mat, precision=_HI, preferred_element_type=jnp.float32)
        for step in range(5):
            inv = inv + jnp.dot(inv, pw, precision=_HI, preferred_element_type=jnp.float32)
            if step < 4:
                pw = jnp.dot(pw, pw, precision=_HI, preferred_element_type=jnp.float32)
        s = s_scr[h]
        sb = s.astype(jnp.bfloat16)
        eb = jnp.exp(bc)
        rhs = beta * (v - eb * jnp.dot(kb, sb, preferred_element_type=jnp.float32))
        u = jnp.dot(inv, rhs, precision=_HI, preferred_element_type=jnp.float32)
        ub = u.astype(jnp.bfloat16)
        qk = (_dot_nt(qb, kb) * decay).astype(jnp.bfloat16)
        o_ref[:, lanes] = (eb * jnp.dot(qb, sb, preferred_element_type=jnp.float32)
                           + jnp.dot(qk, ub, preferred_element_type=jnp.float32))
        b_last = bc[L - 1:L, :]
        kw = (jnp.exp(b_last - bc) * k).astype(jnp.bfloat16)
        s_scr[h] = jnp.exp(b_last) * s + _dot_tn(kw, ub)

    @pl.when(c == pl.num_programs(2) - 1)
    def _():
        s_out_ref[...] = s_scr[...]


def gdn_scan_call(qkv, beta, g, s0):
    b, t, _ = qkv.shape
    hb = GDN_HEADS_PER_STEP
    ng = GDN_HEADS // hb
    L = GDN_CHUNK
    w = hb * GDN_DK

    def qkv_spec(part):
        return pl.BlockSpec((None, L, w), lambda i, j, c: (i, c, part * ng + j))

    def per_chunk(z):
        return z.reshape(b, t // L, L, ng, hb).transpose(0, 3, 1, 2, 4)

    col_spec = pl.BlockSpec((None, None, None, L, hb), lambda i, j, c: (i, j, c, 0, 0))
    state_spec = pl.BlockSpec((None, hb, GDN_DK, GDN_DV), lambda i, j, c: (i, j, 0, 0))
    return pl.pallas_call(
        _gdn_body,
        out_shape=(jax.ShapeDtypeStruct((b, t, GDN_HEADS * GDN_DV), jnp.float32),
                   jax.ShapeDtypeStruct((b, GDN_HEADS, GDN_DK, GDN_DV), jnp.float32)),
        grid=(b, ng, t // L),
        in_specs=[qkv_spec(0), qkv_spec(1), qkv_spec(2), col_spec, col_spec,
                  pl.BlockSpec((None, None, None, hb, L), lambda i, j, c: (i, j, c, 0, 0)),
                  state_spec],
        out_specs=(pl.BlockSpec((None, L, w), lambda i, j, c: (i, c, j)), state_spec),
        scratch_shapes=[pltpu.VMEM((hb, GDN_DK, GDN_DV), jnp.float32)],
        compiler_params=pltpu.CompilerParams(dimension_semantics=("arbitrary", "arbitrary", "arbitrary"),
                                             vmem_limit_bytes=VMEM_LIMIT_BYTES),
        name="gdn_scan",
    )(qkv, qkv, qkv, per_chunk(beta), per_chunk(g), per_chunk(g).swapaxes(3, 4), s0)


def gdn_mixer(h, w_in, w_conv, a_log, dt_bias, g_out, w_out, S0, conv0):
    b, t, _ = h.shape
    qkv, beta, a, gate = jnp.split(h @ w_in, [GDN_CONV_DIM, GDN_CONV_DIM + GDN_HEADS, GDN_CONV_DIM + 2 * GDN_HEADS], axis=-1)
    xc = jnp.concatenate([conv0.astype(qkv.dtype), qkv], axis=1)
    conv = lax.conv_general_dilated(xc, w_conv[:, None, :].astype(xc.dtype), (1,), 'VALID',
                                    dimension_numbers=('NWC', 'WIO', 'NWC'), feature_group_count=GDN_CONV_DIM)
    act = jax.nn.silu(conv)
    beta = jax.nn.sigmoid(beta.astype(jnp.float32))
    g = -jnp.exp(a_log.astype(jnp.float32)) * jax.nn.softplus((a + dt_bias).astype(jnp.float32))
    pad = (-t) % GDN_CHUNK
    if pad:
        act, beta, g = (jnp.pad(z, ((0, 0), (0, pad), (0, 0))) for z in (act, beta, g))
    o, S = gdn_scan_call(act, beta, g, S0.astype(jnp.float32))
    o = o[:, :t].reshape(b, t, GDN_HEADS, GDN_DV)
    o = rmsnorm(o, g_out).astype(h.dtype).reshape(b, t, GDN_HEADS * GDN_DV)
    return (o * jax.nn.silu(gate)) @ w_out, S.astype(h.dtype), xc[:, -(GDN_CONV - 1):]


def mla_project(h, pos, w_in, g_qa, g_kva, w_uq, g_qk):
    b, t, _ = h.shape
    cq, ckv, kr = jnp.split(h @ w_in, [MLA_Q_LORA, MLA_Q_LORA + MLA_KV_LORA], axis=-1)
    q = (rmsnorm(cq, g_qa) @ w_uq).reshape(b, t, MLA_HEADS, MLA_NOPE + MLA_ROPE)
    q_nope = rmsnorm(q[..., :MLA_NOPE], g_qk[0, :MLA_NOPE])
    q_rope = rope(rmsnorm(q[..., MLA_NOPE:], g_qk[0, MLA_NOPE:]), pos)
    k_rope = rope(rmsnorm(kr, g_qk[1, MLA_NOPE:]), pos)
    latent = jnp.concatenate([rmsnorm(ckv, g_kva), k_rope], axis=-1)
    return q_nope, q_rope, latent


def mla_scores(q_nope, q_rope, k_nope, k_rope):
    s = jnp.einsum('bthd,bshd->bhts', q_nope, k_nope) + jnp.einsum('bthr,bsr->bhts', q_rope, k_rope)
    return s.astype(jnp.float32) * MLA_SCALE


def mla_keys(latent, w_uk, g_kn):
    ckv = latent[..., :MLA_KV_LORA]
    return ckv, rmsnorm(jnp.einsum('bsc,chd->bshd', ckv, w_uk), g_kn), latent[..., MLA_KV_LORA:]


def _mla_prompt_body(q_ref, k_ref, v_ref, o_ref, m_scr, acc_scr):
    qi = pl.program_id(2)
    t0 = qi * MLA_TQ
    tpos = t0 + lax.broadcasted_iota(jnp.int32, (MLA_TQ, 1), 0)

    def causal(k0):
        return (k0 + lax.broadcasted_iota(jnp.int32, (1, FLASH_TK), 1)) <= tpos

    n_full = t0 // FLASH_TK
    outs = []
    for h in range(2):
        _flash_init(m_scr, acc_scr)
        _flash_pass(q_ref[h], k_ref.at[h], v_ref.at[h], 0, n_full, None, 1, m_scr, acc_scr)
        _flash_pass(q_ref[h], k_ref.at[h], v_ref.at[h], n_full, n_full + MLA_TQ // FLASH_TK, causal, 1,
                    m_scr, acc_scr)
        outs.append(_flash_result(acc_scr, MLA_VD))
    o_ref[...] = jnp.concatenate(outs, axis=1)


def mla_prompt_attn(q, k, v):
    b, h, t, _ = q.shape
    assert t % MLA_TQ == 0 and MLA_TQ % FLASH_TK == 0 and h % 2 == 0
    seq_spec = pl.BlockSpec((None, 2, t, LANES), lambda i, j, k: (i, j, 0, 0))
    return pl.pallas_call(
        _mla_prompt_body,
        out_shape=jax.ShapeDtypeStruct((b, t, h * MLA_VD), jnp.float32),
        grid=(b, h // 2, t // MLA_TQ),
        in_specs=[pl.BlockSpec((None, 2, MLA_TQ, LANES), lambda i, j, k: (i, j, k, 0)), seq_spec, seq_spec],
        out_specs=pl.BlockSpec((None, MLA_TQ, 2 * MLA_VD), lambda i, j, k: (i, k, j)),
        scratch_shapes=[pltpu.VMEM((MLA_TQ, LANES), jnp.float32), pltpu.VMEM((MLA_TQ, LANES), jnp.float32)],
        compiler_params=pltpu.CompilerParams(dimension_semantics=("arbitrary", "arbitrary", "arbitrary"),
                                             vmem_limit_bytes=VMEM_LIMIT_BYTES),
        name="mla_prompt_attn",
    )(q, k, v)


def mla_prompt(h, w_in, g_qa, g_kva, w_uq, w_uk, w_uv, g_qk, w_out):
    b, t, _ = h.shape
    q_nope, q_rope, latent = mla_project(h, jnp.arange(t, dtype=jnp.float32), w_in, g_qa, g_kva, w_uq, g_qk)
    ckv, k_nope, k_rope = mla_keys(latent, w_uk, g_qk[1, :MLA_NOPE])
    v = jnp.einsum('bsc,chd->bshd', ckv, w_uv)

    def heads_first(x):
        return x.transpose(0, 2, 1, 3)

    pad = jnp.zeros((b, MLA_HEADS, t, LANES - MLA_NOPE - MLA_ROPE), jnp.float32)
    q = jnp.concatenate([heads_first(q_nope), heads_first(q_rope), pad], axis=-1) * MLA_SCALE
    k = jnp.concatenate([heads_first(k_nope), jnp.broadcast_to(k_rope[:, None], (b, MLA_HEADS, t, MLA_ROPE)), pad],
                        axis=-1)
    v1 = jnp.concatenate([heads_first(v), jnp.ones((b, MLA_HEADS, t, 1), jnp.float32),
                          jnp.zeros((b, MLA_HEADS, t, LANES - MLA_VD - 1), jnp.float32)], axis=-1)
    o = mla_prompt_attn(q.astype(jnp.bfloat16), k.astype(jnp.bfloat16), v1.astype(jnp.bfloat16))
    return o @ w_out, latent


def _mla_page_scores(lat, w_uk_ref, qbd, gexp, qr):
    ckv = lat[:, :MLA_KV_LORA].astype(jnp.bfloat16)
    kn = jnp.dot(ckv, w_uk_ref[...], preferred_element_type=jnp.float32)
    s_n = _dot_nt(kn.astype(jnp.bfloat16), qbd)
    ms = jnp.dot((kn * kn).astype(jnp.bfloat16), gexp, preferred_element_type=jnp.float32)
    s_r = _dot_nt(lat[:, MLA_KV_LORA:].astype(jnp.bfloat16), qr)
    return ckv, (s_n * lax.rsqrt(ms + RMS_EPS) + s_r) * MLA_SCALE


def _mla_sample_body(pt_ref, *refs):
    pp = MLA_PAGES_PER_STEP
    page_refs = refs[:pp]
    qn_ref, qr_ref, latn_ref, w_uk_ref, w_uv_ref, o_ref, ckv_scr, s_scr, qbd_scr = refs[pp:]
    j = pl.program_id(1)
    nj = pl.num_programs(1)
    nq = DEC_SEQ * MLA_HEADS
    hw = MLA_HEADS * MLA_NOPE
    n = pp * PAGE_SIZE
    n_past = nj * n

    @pl.when(j == 0)
    def _():
        r = lax.broadcasted_iota(jnp.int32, (nq, hw), 0)
        c = lax.broadcasted_iota(jnp.int32, (nq, hw), 1)
        q_rep = jnp.concatenate([qn_ref[...]] * MLA_HEADS, axis=1)
        qbd_scr[...] = jnp.where(r % MLA_HEADS == c // MLA_NOPE, q_rep, 0.0).astype(jnp.bfloat16)

    gr = lax.broadcasted_iota(jnp.int32, (hw, nq), 0)
    gc = lax.broadcasted_iota(jnp.int32, (hw, nq), 1)
    gexp = jnp.where(gr // MLA_NOPE == gc % MLA_HEADS, 1.0 / MLA_NOPE, 0.0).astype(jnp.bfloat16)
    qr = qr_ref[...].astype(jnp.bfloat16)
    lat = jnp.concatenate([p[...] for p in page_refs], axis=0)
    ckv, s = _mla_page_scores(lat, w_uk_ref, qbd_scr[...], gexp, qr)
    row0 = pl.multiple_of(j * n, n)
    ones_col = (lax.broadcasted_iota(jnp.int32, (1, LANES), 1) == 0).astype(jnp.bfloat16)
    ckv_scr[pl.ds(row0, n), :MLA_KV_LORA] = ckv
    ckv_scr[pl.ds(row0, n), MLA_KV_LORA:] = jnp.broadcast_to(ones_col, (n, LANES))
    s_scr[pl.ds(row0, n), :] = s

    @pl.when(j == nj - 1)
    def _():
        ckv_n, s_n = _mla_page_scores(latn_ref[...], w_uk_ref, qbd_scr[...], gexp, qr)
        kr = lax.broadcasted_iota(jnp.int32, (MLA_NEW_ROWS, nq), 0)
        kc = lax.broadcasted_iota(jnp.int32, (MLA_NEW_ROWS, nq), 1)
        s_scr[n_past:, :] = jnp.where(kr <= kc // MLA_HEADS, s_n, NEG)
        ckv_scr[n_past:, :MLA_KV_LORA] = ckv_n
        ckv_scr[n_past:, MLA_KV_LORA:] = jnp.broadcast_to(ones_col, (MLA_NEW_ROWS, LANES))
        sc = s_scr[...]
        p = jnp.exp(sc - jnp.max(sc, axis=0, keepdims=True)).astype(jnp.bfloat16)
        acc = _dot_tn(p, ckv_scr[...])
        o_lat = (acc[:, :MLA_KV_LORA] / acc[:, MLA_KV_LORA:MLA_KV_LORA + 1]).astype(jnp.bfloat16)
        z = jnp.dot(o_lat, w_uv_ref[...], preferred_element_type=jnp.float32)
        zr = lax.broadcasted_iota(jnp.int32, (MLA_HEADS, MLA_HEADS * MLA_VD), 0)
        zc = lax.broadcasted_iota(jnp.int32, (MLA_HEADS, MLA_HEADS * MLA_VD), 1)
        keep = zr == zc // MLA_VD
        o_ref[...] = jnp.concatenate(
            [jnp.sum(jnp.where(keep, z[t * MLA_HEADS:(t + 1) * MLA_HEADS], 0.0), axis=0, keepdims=True)
             for t in range(DEC_SEQ)], axis=0)


def mla_sample_attn(pool, li, page_table, qn, qr, latent, w_uk, w_uv):
    b, n_pages = page_table.shape
    pp = MLA_PAGES_PER_STEP
    assert n_pages % pp == 0 and latent.shape[1] == DEC_SEQ
    nq = DEC_SEQ * MLA_HEADS
    n_rows = n_pages * PAGE_SIZE + MLA_NEW_ROWS
    latn = jnp.pad(latent, ((0, 0), (0, MLA_NEW_ROWS - DEC_SEQ), (0, 0)))

    def page_spec(k):
        return pl.BlockSpec((None, None, PAGE_SIZE, MLA_LAT), lambda i, j, pt: (li, pt[i, j * pp + k], 0, 0))

    def whole(r, c):
        return pl.BlockSpec((r, c), lambda i, j, pt: (0, 0))

    def per_seq(r, c):
        return pl.BlockSpec((None, r, c), lambda i, j, pt: (i, 0, 0))

    grid_spec = pltpu.PrefetchScalarGridSpec(
        num_scalar_prefetch=1,
        grid=(b, n_pages // pp),
        in_specs=[page_spec(k) for k in range(pp)] + [
            per_seq(nq, MLA_NOPE), per_seq(nq, MLA_ROPE), per_seq(MLA_NEW_ROWS, MLA_LAT),
            whole(MLA_KV_LORA, MLA_HEADS * MLA_NOPE), whole(MLA_KV_LORA, MLA_HEADS * MLA_VD)],
        out_specs=per_seq(DEC_SEQ, MLA_HEADS * MLA_VD),
        scratch_shapes=[pltpu.VMEM((n_rows, MLA_KV_LORA + LANES), jnp.bfloat16),
                        pltpu.VMEM((n_rows, nq), jnp.float32),
                        pltpu.VMEM((nq, MLA_HEADS * MLA_NOPE), jnp.bfloat16)])
    return pl.pallas_call(
        _mla_sample_body,
        out_shape=jax.ShapeDtypeStruct((b, DEC_SEQ, MLA_HEADS * MLA_VD), jnp.float32),
        grid_spec=grid_spec,
        compiler_params=pltpu.CompilerParams(dimension_semantics=("arbitrary", "arbitrary"),
                                             vmem_limit_bytes=VMEM_LIMIT_BYTES),
        name="mla_sample_attn",
    )(page_table, *([pool] * pp), qn, qr, latn, w_uk, w_uv)


def mla_sample(h, pool, li, page_table, w_in, g_qa, g_kva, w_uq, w_uk, w_uv, g_qk, w_out):
    b, t, _ = h.shape
    q_nope, q_rope, latent = mla_project(h, PAST_LEN + jnp.arange(t, dtype=jnp.float32), w_in, g_qa, g_kva, w_uq, g_qk)
    qn = (q_nope * g_qk[1, :MLA_NOPE]).reshape(b, t * MLA_HEADS, MLA_NOPE)
    qr = q_rope.reshape(b, t * MLA_HEADS, MLA_ROPE)
    o = mla_sample_attn(pool, li, page_table, qn, qr, latent,
                        w_uk.reshape(MLA_KV_LORA, -1).astype(jnp.bfloat16),
                        w_uv.reshape(MLA_KV_LORA, -1).astype(jnp.bfloat16))
    return o @ w_out, latent


def nsa_project(h, w_in, b_gate, g_qk):
    b, t, _ = h.shape
    qw, kvw = NSA_HEADS * NSA_HD, 6 * NSA_GROUPS * NSA_HD
    q, kv, gates = jnp.split(h @ w_in, [qw, qw + kvw], axis=-1)
    q = rmsnorm(q.reshape(b, t, NSA_GROUPS, NSA_HPG, NSA_HD), g_qk[0])
    kv = kv.reshape(b, t, 6, NSA_GROUPS, NSA_HD)
    rows = jnp.stack([kv[:, :, 0], kv[:, :, 1], rmsnorm(kv[:, :, 2], g_qk[1]), kv[:, :, 3]], axis=2)
    win = jnp.stack([rmsnorm(kv[:, :, 4], g_qk[1]), kv[:, :, 5]], axis=2)
    gates = jax.nn.sigmoid(gates + b_gate).reshape(b, t, NSA_GROUPS, NSA_HPG, 3)
    return q, rows, win, gates


def nsa_compress(kv_rows, w_cmp, g_k):
    b, l = kv_rows.shape[:2]
    blk = kv_rows.reshape(b, l // NSA_BLOCK, NSA_BLOCK, 2, NSA_GROUPS, NSA_HD)
    c = jnp.einsum('bnpcgd,cpde->bncge', blk, w_cmp)
    return rmsnorm(c[:, :, 0], g_k), c[:, :, 1]


def nsa_cmp_attn(q, k_c, v_c, qpos):
    nb = k_c.shape[1]
    s = jnp.einsum('btghd,bngd->bghtn', q, k_c) * NSA_SCALE
    mask = ((jnp.arange(nb) + 1) * NSA_BLOCK - 1)[None, :] <= qpos[:, None]
    p = masked_softmax(s, mask)
    o = jnp.einsum('bghtn,bngd->btghd', p.astype(v_c.dtype), v_c)
    return o, jnp.sum(p, axis=2)


def nsa_select(imp, qpos):
    nb = imp.shape[-1]
    cand = jnp.arange(nb)[None, :] < (qpos // NSA_BLOCK)[:, None]
    val, idx = lax.top_k(jnp.where(cand, imp, -jnp.inf), min(NSA_TOPK, nb))
    return idx, jnp.isfinite(val)


def nsa_sel_prompt(q, k_s, v_s, idx, valid):
    b, t = q.shape[:2]
    nb = t // NSA_BLOCK
    kb = k_s.reshape(b, nb, NSA_BLOCK, NSA_GROUPS, NSA_HD).transpose(0, 3, 1, 2, 4)
    vb = v_s.reshape(b, nb, NSA_BLOCK, NSA_GROUPS, NSA_HD).transpose(0, 3, 1, 2, 4)
    bi = jnp.arange(b)[:, None, None, None]
    gi = jnp.arange(NSA_GROUPS)[None, None, :, None]
    idx, valid = idx.transpose(0, 2, 1, 3), valid.transpose(0, 2, 1, 3)

    def block(i):
        t0 = i * NSA_QB
        qpos = t0 + jnp.arange(NSA_QB)
        qq = lax.dynamic_slice_in_dim(q, t0, NSA_QB, axis=1)
        cur = jnp.broadcast_to((qpos // NSA_BLOCK)[None, :, None, None], (b, NSA_QB, NSA_GROUPS, 1))
        ii = jnp.concatenate([lax.dynamic_slice_in_dim(idx, t0, NSA_QB, axis=1), cur], axis=-1)
        ok = jnp.concatenate([lax.dynamic_slice_in_dim(valid, t0, NSA_QB, axis=1), jnp.ones(cur.shape, bool)], axis=-1)
        kpos = ii[..., None] * NSA_BLOCK + jnp.arange(NSA_BLOCK)
        mask = ok[..., None] & (kpos <= qpos[None, :, None, None, None])
        n = ii.shape[-1] * NSA_BLOCK
        ks, vs = kb[bi, gi, ii], vb[bi, gi, ii]
        s = jnp.einsum('btghd,btgkpd->bghtkp', qq, ks).reshape(b, NSA_GROUPS, NSA_HPG, NSA_QB, n) * NSA_SCALE
        mask = mask.transpose(0, 2, 1, 3, 4).reshape(b, NSA_GROUPS, 1, NSA_QB, n)
        p = masked_softmax(s, mask)
        return jnp.einsum('bghtn,btgnd->btghd', p.astype(vs.dtype), vs.reshape(b, NSA_QB, NSA_GROUPS, n, NSA_HD))

    return jnp.moveaxis(lax.map(block, jnp.arange(t // NSA_QB)), 0, 1).reshape(q.shape)


def nsa_sel_sample(q, k_new, v_new, idx, valid, pool, li, page_table):
    b, t = q.shape[:2]
    bpp = PAGE_SIZE // NSA_BLOCK
    kk = idx.shape[-1]
    n = kk * NSA_BLOCK
    ii, ok = idx.transpose(0, 2, 1, 3), valid.transpose(0, 2, 1, 3)
    phys = page_table[jnp.arange(b)[:, None, None, None], ii // bpp][..., None]
    rows = (ii % bpp)[..., None] * NSA_BLOCK + jnp.arange(NSA_BLOCK)
    gi = jnp.arange(NSA_GROUPS)[None, None, :, None, None]
    ks = pool[li, phys, rows, 2, gi].reshape(b, t, NSA_GROUPS, n, NSA_HD)
    vs = pool[li, phys, rows, 3, gi].reshape(b, t, NSA_GROUPS, n, NSA_HD)
    s = jnp.concatenate([jnp.einsum('btghd,btgnd->bghtn', q, ks), jnp.einsum('btghd,bsgd->bghts', q, k_new)], axis=-1)
    mask_past = jnp.repeat(ok, NSA_BLOCK, axis=-1).transpose(0, 2, 1, 3)[:, :, None]
    mask_new = jnp.broadcast_to(jnp.tril(jnp.ones((t, t), bool)), (b, NSA_GROUPS, 1, t, t))
    p = masked_softmax(s * NSA_SCALE, jnp.concatenate([mask_past, mask_new], axis=-1)).astype(v_new.dtype)
    return (jnp.einsum('bghtn,btgnd->btghd', p[..., :n], vs)
            + jnp.einsum('bghts,bsgd->btghd', p[..., n:], v_new))


def nsa_win_prompt(q, k_w, v_w):
    t = q.shape[1]
    pad = ((0, 0), (NSA_WINDOW, 0), (0, 0), (0, 0))
    kp, vp = jnp.pad(k_w, pad), jnp.pad(v_w, pad)
    span = NSA_WINDOW + Q_BLOCK

    def block(i):
        t0 = i * Q_BLOCK
        qq = lax.dynamic_slice_in_dim(q, t0, Q_BLOCK, axis=1)
        kk = lax.dynamic_slice_in_dim(kp, t0, span, axis=1)
        vv = lax.dynamic_slice_in_dim(vp, t0, span, axis=1)
        qpos = t0 + jnp.arange(Q_BLOCK)
        kpos = t0 - NSA_WINDOW + jnp.arange(span)
        mask = ((kpos[None, :] <= qpos[:, None]) & (kpos[None, :] > qpos[:, None] - NSA_WINDOW)
                & (kpos[None, :] >= 0))
        p = masked_softmax(jnp.einsum('bqghd,bkgd->bghqk', qq, kk) * NSA_SCALE, mask)
        return jnp.einsum('bghqk,bkgd->bqghd', p.astype(vv.dtype), vv)

    return jnp.moveaxis(lax.map(block, jnp.arange(t // Q_BLOCK)), 0, 1).reshape(q.shape)


def nsa_win_sample(q, win, buf):
    t = q.shape[1]
    wb = buf.shape[1]
    kv = jnp.concatenate([buf.astype(win.dtype), win], axis=1)
    qpos = PAST_LEN + jnp.arange(t)
    kpos = PAST_LEN - wb + jnp.arange(wb + t)
    mask = (kpos[None, :] <= qpos[:, None]) & (kpos[None, :] > qpos[:, None] - NSA_WINDOW)
    p = masked_softmax(jnp.einsum('btghd,bsgd->bghts', q, kv[:, :, 0]) * NSA_SCALE, mask)
    o = jnp.einsum('bghts,bsgd->btghd', p.astype(kv.dtype), kv[:, :, 1])
    return o, kv[:, -wb:]


def nsa_merge(gates, o_cmp, o_sel, o_win):
    o = gates[..., 0:1] * o_cmp + gates[..., 1:2] * o_sel + gates[..., 2:3] * o_win
    return o.reshape(o.shape[0], o.shape[1], NSA_HEADS * NSA_HD)


def _nsa_body(q_ref, rows_ref, win_ref, kc_ref, vc_ref, gates_ref, o_ref, kv_scr, m_scr, acc_scr):
    qi = pl.program_id(1)
    nb = kc_ref.shape[0]
    gw = NSA_GROUPS * NSA_HD

    @pl.when(qi == 0)
    def _():
        lane = lax.broadcasted_iota(jnp.int32, (1, LANES), 1)
        one_col = (lane == NSA_HD).astype(jnp.float32)
        srcs = (rows_ref[:, 2 * gw:3 * gw], rows_ref[:, 3 * gw:4 * gw], win_ref[:, 0:gw], win_ref[:, gw:2 * gw])
        for i, x in enumerate(srcs):
            for g in range(NSA_GROUPS):
                xg = x if g == 0 else pltpu.roll(x, NSA_HD, 1)
                kv_scr[i * NSA_GROUPS + g] = jnp.where(lane < NSA_HD, xg, one_col if i % 2 else 0.0).astype(jnp.bfloat16)

    t0 = qi * NSA_TQ
    tpos = t0 + lax.broadcasted_iota(jnp.int32, (NSA_TQ, 1), 0)
    blk_t = tpos // NSA_BLOCK
    nidx = lax.broadcasted_iota(jnp.int32, (1, nb), 1)
    rows = NSA_HPG * NSA_TQ
    gq = NSA_HPG * NSA_HD
    zpad = jnp.zeros((NSA_TQ, LANES - NSA_HD), jnp.float32)
    for g in range(NSA_GROUPS):
        lanes = slice(g * NSA_HD, (g + 1) * NSA_HD)
        qg = q_ref[:, g * gq:(g + 1) * gq] * NSA_SCALE
        q_all = jnp.concatenate([jnp.concatenate([qg[:, h * NSA_HD:(h + 1) * NSA_HD], zpad], axis=1)
                                 for h in range(NSA_HPG)], axis=0).astype(jnp.bfloat16)

        kc = jnp.concatenate([kc_ref[:, lanes], jnp.zeros((nb, LANES - NSA_HD), jnp.float32)],
                             axis=1).astype(jnp.bfloat16)
        vc = vc_ref[:, lanes].astype(jnp.bfloat16)
        cmask = ((nidx + 1) * NSA_BLOCK - 1) <= tpos
        s = jnp.where(cmask[None], _dot_nt(q_all, kc).reshape(NSA_HPG, NSA_TQ, nb), NEG)
        e = jnp.exp(s - jnp.max(s, axis=-1, keepdims=True))
        p = e / jnp.sum(e, axis=-1, keepdims=True) * cmask[None].astype(jnp.float32)
        imp = jnp.sum(p, axis=0)
        o_cmp = jnp.dot(p.reshape(rows, nb).astype(jnp.bfloat16), vc, preferred_element_type=jnp.float32)

        rank = jnp.zeros((NSA_TQ, nb), jnp.int32)
        for m in range(nb):
            col = imp[:, m:m + 1]
            beats = (col > imp) | ((col == imp) & (m < nidx))
            rank = rank + jnp.where(beats & (m < blk_t), 1, 0)
        sel_f = (((nidx < blk_t) & (rank < NSA_TOPK)) | (nidx == blk_t)).astype(jnp.float32)
        blk_id = lax.broadcasted_iota(jnp.int32, (nb, FLASH_TK), 0)

        def sel_mask(k0):
            spos = k0 + lax.broadcasted_iota(jnp.int32, (1, FLASH_TK), 1)
            expand = (blk_id == spos // NSA_BLOCK).astype(jnp.float32)
            chosen = jnp.dot(sel_f, expand, preferred_element_type=jnp.float32) > 0.5
            return chosen & (spos <= tpos)

        def win_mask(k0):
            spos = k0 + lax.broadcasted_iota(jnp.int32, (1, FLASH_TK), 1)
            return (spos <= tpos) & (spos > tpos - NSA_WINDOW)

        kt_hi = (t0 + NSA_TQ + FLASH_TK - 1) // FLASH_TK
        _flash_init(m_scr, acc_scr)
        _flash_pass(q_all, kv_scr.at[g], kv_scr.at[NSA_GROUPS + g], 0, kt_hi, sel_mask, NSA_HPG, m_scr, acc_scr)
        o_sel = _flash_result(acc_scr, NSA_HD)
        kt_lo = jnp.maximum(t0 - (NSA_WINDOW - 1), 0) // FLASH_TK
        _flash_init(m_scr, acc_scr)
        _flash_pass(q_all, kv_scr.at[2 * NSA_GROUPS + g], kv_scr.at[3 * NSA_GROUPS + g], kt_lo, kt_hi, win_mask,
                    NSA_HPG, m_scr, acc_scr)
        o_win = _flash_result(acc_scr, NSA_HD)

        outs = []
        for h in range(NSA_HPG):
            r = slice(h * NSA_TQ, (h + 1) * NSA_TQ)
            c = (g * NSA_HPG + h) * 3
            outs.append(gates_ref[:, c:c + 1] * o_cmp[r] + gates_ref[:, c + 1:c + 2] * o_sel[r]
                        + gates_ref[:, c + 2:c + 3] * o_win[r])
        o_ref[:, g * gq:(g + 1) * gq] = jnp.concatenate(outs, axis=1)


def nsa_prompt_attn(q, rows, win, k_c, v_c, gates):
    b, t, _ = q.shape
    nb = t // NSA_BLOCK
    assert t % FLASH_TK == 0 and t % NSA_TQ == 0
    gw = NSA_GROUPS * NSA_HD
    qw = NSA_HEADS * NSA_HD
    return pl.pallas_call(
        _nsa_body,
        out_shape=jax.ShapeDtypeStruct((b, t, qw), jnp.float32),
        grid=(b, t // NSA_TQ),
        in_specs=[pl.BlockSpec((None, NSA_TQ, qw), lambda i, j: (i, j, 0)),
                  pl.BlockSpec((None, t, 4 * gw), lambda i, j: (i, 0, 0)),
                  pl.BlockSpec((None, t, 2 * gw), lambda i, j: (i, 0, 0)),
                  pl.BlockSpec((None, nb, gw), lambda i, j: (i, 0, 0)),
                  pl.BlockSpec((None, nb, gw), lambda i, j: (i, 0, 0)),
                  pl.BlockSpec((None, NSA_TQ, NSA_HEADS * 3), lambda i, j: (i, j, 0))],
        out_specs=pl.BlockSpec((None, NSA_TQ, qw), lambda i, j: (i, j, 0)),
        scratch_shapes=[pltpu.VMEM((4 * NSA_GROUPS, t, LANES), jnp.bfloat16),
                        pltpu.VMEM((NSA_HPG * NSA_TQ, LANES), jnp.float32),
                        pltpu.VMEM((NSA_HPG * NSA_TQ, LANES), jnp.float32)],
        compiler_params=pltpu.CompilerParams(dimension_semantics=("arbitrary", "arbitrary"),
                                             vmem_limit_bytes=VMEM_LIMIT_BYTES),
        name="nsa_prompt_attn",
    )(q, rows, win, k_c, v_c, gates)


def nsa_prompt(h, w_in, b_gate, g_qk, w_cmp, w_out):
    b, t, _ = h.shape
    q, rows, win, gates = nsa_project(h, w_in, b_gate, g_qk)
    k_c, v_c = nsa_compress(rows[:, :, :2], w_cmp, g_qk[1])
    nb = t // NSA_BLOCK
    o = nsa_prompt_attn(q.reshape(b, t, -1), rows.reshape(b, t, -1), win.reshape(b, t, -1),
                        k_c.reshape(b, nb, -1), v_c.reshape(b, nb, -1), gates.reshape(b, t, -1))
    return o @ w_out, rows, win[:, -min(NSA_WINDOW, t):]


def nsa_sample(h, pool, li, page_table, win_buf, w_in, b_gate, g_qk, w_cmp, w_out):
    b, t, _ = h.shape
    q, rows, win, gates = nsa_project(h, w_in, b_gate, g_qk)
    qpos = PAST_LEN + jnp.arange(t)

    def page_cmp(p):
        return nsa_compress(pool[li, page_table[:, p], :, :2], w_cmp, g_qk[1])

    k_c, v_c = lax.map(page_cmp, jnp.arange(page_table.shape[1]))
    k_c = jnp.moveaxis(k_c, 0, 1).reshape(b, -1, NSA_GROUPS, NSA_HD)
    v_c = jnp.moveaxis(v_c, 0, 1).reshape(b, -1, NSA_GROUPS, NSA_HD)
    o_cmp, imp = nsa_cmp_attn(q, k_c, v_c, qpos)
    idx, valid = nsa_select(imp, qpos)
    o_sel = nsa_sel_sample(q, rows[:, :, 2], rows[:, :, 3], idx, valid, pool, li, page_table)
    o_win, new_buf = nsa_win_sample(q, win, win_buf)
    return nsa_merge(gates, o_cmp, o_sel, o_win) @ w_out, rows, new_buf


def kernel(x_prompt, x_sample, state_mlstm_C, state_mlstm_n, state_mlstm_m, cache_mla, cache_nsa,
           state_nsa_win, state_gdn_S, state_gdn_conv, page_table, c_prompt, c_sample,
           w_ada, b_ada, g_norm, w_ffn_up, w_ffn_down,
           w_ml_in, b_ml_gate, g_ml_hn, w_ml_out,
           w_mla_in, g_mla_qa, g_mla_kva, w_mla_uq, w_mla_uk, w_mla_uv, g_mla_qk, w_mla_out,
           w_nsa_in, b_nsa_gate, g_nsa_qk, w_nsa_cmp, w_nsa_out,
           w_gdn_in, w_gdn_conv, a_gdn_log, b_gdn_dt, g_gdn_out, w_gdn_out):
    bp = x_prompt.shape[0]
    xp, xs = x_prompt, x_sample
    ml_p, ml_s, mla_p, mla_s, nsa_p, nsa_s, gdn_p, gdn_s = [], [], [], [], [], [], [], []
    w_up_bf = w_ffn_up.astype(jnp.bfloat16)
    w_dn_bf = w_ffn_down.astype(jnp.bfloat16)
    for i in range(DEPTH):
        kind, li = i % N_MIXERS, i // N_MIXERS
        mod_p = adaln(c_prompt, w_ada[i], b_ada[i])
        mod_s = adaln(c_sample, w_ada[i], b_ada[i])
        xp = ffn_half_prompt(xp, g_norm[i, 0], mod_p, 0, w_up_bf[i, 0], w_dn_bf[i, 0])
        xs = ffn_half_sample(xs, g_norm[i, 0], mod_s, 0, w_up_bf[i, 0], w_dn_bf[i, 0])
        hp = modulate(xp, g_norm[i, 1], mod_p, 1)
        hs = modulate(xs, g_norm[i, 1], mod_s, 1)
        if kind == 0:
            wts = (w_ml_in[li], b_ml_gate[li], g_ml_hn[li], w_ml_out[li])
            op, c_p, n_p, m_p = mlstm_mixer(hp, *wts, jnp.zeros((bp, ML_HEADS, ML_DQK, ML_DV), jnp.float32),
                                            jnp.zeros((bp, ML_HEADS, ML_DQK), jnp.float32),
                                            jnp.zeros((bp, ML_HEADS), jnp.float32))
            os_, c_s, n_s, m_s = mlstm_mixer(hs, *wts, state_mlstm_C[li], state_mlstm_n[li], state_mlstm_m[li])
            ml_p.append((c_p, n_p, m_p))
            ml_s.append((c_s, n_s, m_s))
        elif kind == 1:
            wts = (w_mla_in[li], g_mla_qa[li], g_mla_kva[li], w_mla_uq[li], w_mla_uk[li], w_mla_uv[li],
                   g_mla_qk[li], w_mla_out[li])
            op, r_p = mla_prompt(hp, *wts)
            os_, r_s = mla_sample(hs, cache_mla, li, page_table, *wts)
            mla_p.append((r_p,))
            mla_s.append((r_s,))
        elif kind == 2:
            wts = (w_nsa_in[li], b_nsa_gate[li], g_nsa_qk[li], w_nsa_cmp[li], w_nsa_out[li])
            op, r_p, wb_p = nsa_prompt(hp, *wts)
            os_, r_s, wb_s = nsa_sample(hs, cache_nsa, li, page_table, state_nsa_win[li], *wts)
            nsa_p.append((r_p, wb_p))
            nsa_s.append((r_s, wb_s))
        else:
            wts = (w_gdn_in[li], w_gdn_conv[li], a_gdn_log[li], b_gdn_dt[li], g_gdn_out[li], w_gdn_out[li])
            op, s_p, cv_p = gdn_mixer(hp, *wts, jnp.zeros((bp, GDN_HEADS, GDN_DK, GDN_DV), jnp.float32),
                                      jnp.zeros((bp, GDN_CONV - 1, GDN_CONV_DIM), hp.dtype))
            os_, s_s, cv_s = gdn_mixer(hs, *wts, state_gdn_S[li], state_gdn_conv[li])
            gdn_p.append((s_p, cv_p))
            gdn_s.append((s_s, cv_s))
        xp = xp + mod_p[:, 1, 2][:, None] * op
        xs = xs + mod_s[:, 1, 2][:, None] * os_
        xp = ffn_half_prompt(xp, g_norm[i, 2], mod_p, 2, w_up_bf[i, 1], w_dn_bf[i, 1])
        xs = ffn_half_sample(xs, g_norm[i, 2], mod_s, 2, w_up_bf[i, 1], w_dn_bf[i, 1])
    return (xp, xs,
            _stack(ml_p, 0), _stack(ml_s, 0), _stack(ml_p, 1), _stack(ml_s, 1), _stack(ml_p, 2), _stack(ml_s, 2),
            _stack(mla_p, 0), _stack(mla_s, 0),
            _stack(nsa_p, 0), _stack(nsa_s, 0), _stack(nsa_p, 1), _stack(nsa_s, 1),
            _stack(gdn_p, 0), _stack(gdn_s, 0), _stack(gdn_p, 1), _stack(gdn_s, 1))
```

```python
import functools

import jax
import jax.numpy as jnp
from jax import lax
from jax.experimental import pallas as pl
from jax.experimental.pallas import tpu as pltpu

D_MODEL = 1024
BATCH = 16
SEQ = 2048
DEPTH = 4
DEC_BATCH = 128
DEC_SEQ = 4
PAST_LEN = 8192
PAGE_SIZE = 128

N_MIXERS = 4
D_FF = 2816
RMS_EPS = 1e-6
NEG = -1e30
Q_BLOCK = 128

ML_HEADS = 8
ML_DQK = 64
ML_DV = 128
ML_CHUNK = 64
ML_GATE_CAP = 15.0

MLA_HEADS = 16
MLA_NOPE = 64
MLA_ROPE = 32
MLA_VD = 64
MLA_KV_LORA = 256
MLA_Q_LORA = 384
MLA_LAT = MLA_KV_LORA + MLA_ROPE
MLA_SCALE = (MLA_NOPE + MLA_ROPE) ** -0.5
ROPE_THETA = 10000.0

NSA_HEADS = 16
NSA_GROUPS = 2
NSA_HPG = NSA_HEADS // NSA_GROUPS
NSA_HD = 64
NSA_BLOCK = 64
NSA_TOPK = 15
NSA_WINDOW = 512
NSA_QB = 32
NSA_SCALE = NSA_HD ** -0.5

GDN_HEADS = 8
GDN_DK = 128
GDN_DV = 128
GDN_CONV = 4
GDN_CHUNK = 64
GDN_CONV_DIM = GDN_HEADS * (2 * GDN_DK + GDN_DV)

VMEM_LIMIT_BYTES = 56 * 1024 * 1024
FFN_ROW_TILE = 512
FFN_COL_CHUNK = 256
GDN_HEADS_PER_STEP = 4
LANES = 128
FLASH_TK = 256
NSA_TQ = 128
NSA_SEL_PAGES = 8
MLA_TQ = 512
MLA_PAGES_PER_STEP = 8
MLA_NEW_ROWS = 256
_HI = lax.Precision.HIGHEST


def _ffn_body(x_ref, shift_ref, scale_ref, gate_ref, g_ref, wup_ref, wdn_ref, o_ref, acc_ref):
    x = x_ref[...]
    y = x * lax.rsqrt(jnp.mean(x * x, axis=-1, keepdims=True) + RMS_EPS) * g_ref[...]
    h = (y * (1.0 + scale_ref[...]) + shift_ref[...]).astype(jnp.bfloat16)
    for j in range(D_FF // FFN_COL_CHUNK):
        lo = j * FFN_COL_CHUNK
        a = jnp.dot(h, wup_ref[:, lo:lo + FFN_COL_CHUNK], preferred_element_type=jnp.float32)
        b = jnp.dot(h, wup_ref[:, D_FF + lo:D_FF + lo + FFN_COL_CHUNK], preferred_element_type=jnp.float32)
        act = (a * jax.nn.sigmoid(a) * b).astype(jnp.bfloat16)
        part = jnp.dot(act, wdn_ref[lo:lo + FFN_COL_CHUNK, :], preferred_element_type=jnp.float32)
        if j == 0:
            acc_ref[...] = part
        else:
            acc_ref[...] += part
    o_ref[...] = x + 0.5 * gate_ref[...] * acc_ref[...]


def _ffn_call(x2d, shift, scale, gate, g, w_up, w_down, rows_per_mod):
    n = x2d.shape[0]
    tm = min(FFN_ROW_TILE, n)
    assert n % tm == 0
    if rows_per_mod:
        assert rows_per_mod % tm == 0
        tiles_per_mod = rows_per_mod // tm
        mod_spec = pl.BlockSpec((None, 1, D_MODEL), lambda i: (i // tiles_per_mod, 0, 0))
    else:
        mod_spec = pl.BlockSpec((tm, D_MODEL), lambda i: (i, 0))
    resident = functools.partial(pl.BlockSpec, pipeline_mode=pl.Buffered(1))
    return pl.pallas_call(
        _ffn_body,
        out_shape=jax.ShapeDtypeStruct((n, D_MODEL), jnp.float32),
        grid=(n // tm,),
        in_specs=[
            pl.BlockSpec((tm, D_MODEL), lambda i: (i, 0)),
            mod_spec, mod_spec, mod_spec,
            resident((1, D_MODEL), lambda i: (0, 0)),
            resident((D_MODEL, 2 * D_FF), lambda i: (0, 0)),
            resident((D_FF, D_MODEL), lambda i: (0, 0)),
        ],
        out_specs=pl.BlockSpec((tm, D_MODEL), lambda i: (i, 0)),
        scratch_shapes=[pltpu.VMEM((tm, D_MODEL), jnp.float32)],
        compiler_params=pltpu.CompilerParams(dimension_semantics=("arbitrary",),
                                             vmem_limit_bytes=VMEM_LIMIT_BYTES),
        name="ffn_half",
    )(x2d, shift, scale, gate, g.reshape(1, D_MODEL), w_up, w_down)


def ffn_half_prompt(x, g, mod, j, w_up, w_down):
    b, t, d = x.shape
    m = [mod[:, j, k][:, None, :] for k in range(3)]
    return _ffn_call(x.reshape(b * t, d), m[0], m[1], m[2], g, w_up, w_down, t).reshape(b, t, d)


def ffn_half_sample(x, g, mod, j, w_up, w_down):
    b, t, d = x.shape
    m = [jnp.repeat(mod[:, j, k], t, axis=0) for k in range(3)]
    return _ffn_call(x.reshape(b * t, d), m[0], m[1], m[2], g, w_up, w_down, 0).reshape(b, t, d)


def rmsnorm(x, g):
    xf = x.astype(jnp.float32)
    y = xf * lax.rsqrt(jnp.mean(xf * xf, axis=-1, keepdims=True) + RMS_EPS)
    return (y * g.astype(jnp.float32)).astype(x.dtype)


def l2norm(x):
    xf = x.astype(jnp.float32)
    return xf * lax.rsqrt(jnp.sum(xf * xf, axis=-1, keepdims=True) + 1e-6)


def rope(x, pos):
    half = x.shape[-1] // 2
    freqs = ROPE_THETA ** (-jnp.arange(half, dtype=jnp.float32) / half)
    ang = pos[:, None] * freqs[None, :]
    ang = ang.reshape((ang.shape[0],) + (1,) * (x.ndim - 3) + (half,))
    cos, sin = jnp.cos(ang), jnp.sin(ang)
    xf = x.astype(jnp.float32)
    x1, x2 = xf[..., :half], xf[..., half:]
    return jnp.concatenate([x1 * cos - x2 * sin, x2 * cos + x1 * sin], axis=-1).astype(x.dtype)


def masked_softmax(s, mask):
    return jax.nn.softmax(jnp.where(mask, s.astype(jnp.float32), NEG), axis=-1) * mask


def adaln(c, w, b):
    return (jax.nn.silu(c) @ w + b).reshape(c.shape[0], 3, 3, c.shape[1])


def modulate(x, g, mod, j):
    return rmsnorm(x, g) * (1.0 + mod[:, j, 1][:, None]) + mod[:, j, 0][:, None]


def _chunks(a, chunk):
    b, h, t = a.shape[:3]
    return jnp.moveaxis(a.reshape((b, h, t // chunk, chunk) + a.shape[3:]), 2, 0)


def _unchunk(a):
    a = jnp.moveaxis(a, 0, 2)
    return a.reshape(a.shape[:2] + (a.shape[2] * a.shape[3],) + a.shape[4:])


def _stack(entries, j):
    return jnp.stack([e[j] for e in entries])


def mlstm_scan(q, k, v, ig, lf, C0, n0, m0, chunk):
    causal = jnp.tril(jnp.ones((chunk, chunk), bool))

    def step(carry, inp):
        C, n, m = carry
        qc, kc, vc, ic, fc = inp
        b = jnp.cumsum(fc, axis=-1)
        dmat = jnp.where(causal, b[..., :, None] - b[..., None, :] + ic[..., None, :], NEG)
        inter = b + m[..., None]
        mt = jnp.maximum(inter, jnp.max(dmat, axis=-1))
        w_intra = jnp.exp(dmat - mt[..., None])
        w_inter = jnp.exp(inter - mt)
        s = jnp.einsum('bhld,bhsd->bhls', qc, kc) * w_intra
        num = w_inter[..., None] * jnp.einsum('bhld,bhdv->bhlv', qc, C) + jnp.einsum('bhls,bhsv->bhlv', s, vc)
        den = w_inter * jnp.einsum('bhld,bhd->bhl', qc, n) + jnp.sum(s, axis=-1)
        hc = num / jnp.maximum(jnp.abs(den), jnp.exp(-mt))[..., None]
        m_new = mt[..., -1]
        w_state = jnp.exp(b[..., -1:] - b + ic - m_new[..., None])
        decay = jnp.exp(b[..., -1] + m - m_new)
        C_new = decay[..., None, None] * C + jnp.einsum('bhs,bhsd,bhsv->bhdv', w_state, kc, vc)
        n_new = decay[..., None] * n + jnp.einsum('bhs,bhsd->bhd', w_state, kc)
        return (C_new, n_new, m_new), hc

    xs = tuple(_chunks(a, chunk) for a in (q, k, v, ig, lf))
    (C, n, m), h = lax.scan(step, (C0, n0, m0), xs)
    return _unchunk(h), C, n, m


def mlstm_mixer(h, w_in, b_gate, g_hn, w_out, C0, n0, m0):
    b, t, _ = h.shape
    qk_w, v_w = ML_HEADS * ML_DQK, ML_HEADS * ML_DV
    q, k, v, o, gates = jnp.split(h @ w_in, [qk_w, 2 * qk_w, 2 * qk_w + v_w, 2 * qk_w + 2 * v_w], axis=-1)

    def heads(a, d):
        return a.reshape(b, t, ML_HEADS, d).transpose(0, 2, 1, 3).astype(jnp.float32)

    gates = ML_GATE_CAP * jnp.tanh((gates + b_gate).astype(jnp.float32) / ML_GATE_CAP)
    ig = gates[..., :ML_HEADS].transpose(0, 2, 1)
    lf = jax.nn.log_sigmoid(gates[..., ML_HEADS:]).transpose(0, 2, 1)
    hh, C, n, m = mlstm_scan(heads(q, ML_DQK) * ML_DQK ** -0.5, heads(k, ML_DQK), heads(v, ML_DV), ig, lf,
                             C0.astype(jnp.float32), n0.astype(jnp.float32), m0.astype(jnp.float32),
                             min(ML_CHUNK, t))
    hh = rmsnorm(hh.transpose(0, 2, 1, 3), g_hn).astype(h.dtype).reshape(b, t, v_w)
    return (jax.nn.sigmoid(o) * hh) @ w_out, C.astype(h.dtype), n.astype(h.dtype), m.astype(h.dtype)


def _dot_nt(a, b, **kw):
    return lax.dot_general(a, b, (((1,), (1,)), ((), ())), preferred_element_type=jnp.float32, **kw)


def _dot_tn(a, b, **kw):
    return lax.dot_general(a, b, (((0,), (0,)), ((), ())), preferred_element_type=jnp.float32, **kw)


def _flash_init(m_scr, acc_scr):
    m_scr[...] = jnp.full(m_scr.shape, NEG, jnp.float32)
    acc_scr[...] = jnp.zeros(acc_scr.shape, jnp.float32)


def _flash_pass(q_all, k_ref, v_ref, kt_lo, kt_hi, mask_fn, heads, m_scr, acc_scr):
    rows = q_all.shape[0]
    tq = rows // heads

    def body(kt, carry):
        k0 = pl.multiple_of(kt * FLASH_TK, FLASH_TK)
        s = _dot_nt(q_all, k_ref[pl.ds(k0, FLASH_TK), :])
        if mask_fn is not None:
            s = jnp.where(mask_fn(k0)[None], s.reshape(heads, tq, FLASH_TK), 2.0 * NEG).reshape(rows, FLASH_TK)
        m_old = m_scr[...]
        m_new = jnp.maximum(m_old, jnp.max(s, axis=-1, keepdims=True))
        p = jnp.exp(s - jnp.concatenate([m_new] * (FLASH_TK // LANES), axis=1))
        acc_scr[...] = acc_scr[...] * jnp.exp(m_old - m_new) + jnp.dot(
            p.astype(jnp.bfloat16), v_ref[pl.ds(k0, FLASH_TK), :], preferred_element_type=jnp.float32)
        m_scr[...] = m_new
        return carry

    lax.fori_loop(kt_lo, kt_hi, body, 0)


def _flash_result(acc_scr, dv):
    acc = acc_scr[...]
    return acc[:, :dv] / acc[:, dv:dv + 1]


def _gdn_body(q_ref, k_ref, v_ref, beta_ref, gcol_ref, grow_ref, s0_ref, o_ref, s_out_ref, s_scr):
    c = pl.program_id(2)
    L = GDN_CHUNK

    @pl.when(c == 0)
    def _():
        s_scr[...] = s0_ref[...]

    row = lax.broadcasted_iota(jnp.int32, (L, L), 0)
    col = lax.broadcasted_iota(jnp.int32, (L, L), 1)
    causal = col <= row
    strict = col < row
    eye = (col == row).astype(jnp.float32)
    b_cols = jnp.dot(causal.astype(jnp.float32), gcol_ref[...], precision=_HI, preferred_element_type=jnp.float32)
    b_rows = jnp.dot(grow_ref[...], (col >= row).astype(jnp.float32), precision=_HI,
                     preferred_element_type=jnp.float32)
    for h in range(GDN_HEADS_PER_STEP):
        lanes = slice(h * GDN_DK, (h + 1) * GDN_DK)
        qf = q_ref[:, lanes]
        kf = k_ref[:, lanes]
        q = qf * lax.rsqrt(jnp.sum(qf * qf, axis=-1, keepdims=True) + 1e-6) * GDN_DK ** -0.5
        k = kf * lax.rsqrt(jnp.sum(kf * kf, axis=-1, keepdims=True) + 1e-6)
        v = v_ref[:, lanes]
        beta = beta_ref[:, h:h + 1]
        bc = b_cols[:, h:h + 1]
        br = b_rows[h:h + 1, :]
        decay = jnp.exp(jnp.where(causal, bc - br, NEG))
        qb, kb = q.astype(jnp.bfloat16), k.astype(jnp.bfloat16)
        a_mat = jnp.where(strict, beta * _dot_nt(kb, kb) * decay, 0.0)
        inv = eye - a_mat
        pw = jnp.dot(a_mat, a_mat, precision=_HI, preferred_element_type=jnp.float32)
        for step in range(5):
            inv = inv + jnp.dot(inv, pw, precision=_HI, preferred_element_type=jnp.float32)
            if step < 4:
                pw = jnp.dot(pw, pw, precision=_HI, preferred_element_type=jnp.float32)
        s = s_scr[h]
        sb = s.astype(jnp.bfloat16)
        eb = jnp.exp(bc)
        rhs = beta * (v - eb * jnp.dot(kb, sb, preferred_element_type=jnp.float32))
        u = jnp.dot(inv, rhs, precision=_HI, preferred_element_type=jnp.float32)
        ub = u.astype(jnp.bfloat16)
        qk = (_dot_nt(qb, kb) * decay).astype(jnp.bfloat16)
        o_ref[:, lanes] = (eb * jnp.dot(qb, sb, preferred_element_type=jnp.float32)
                           + jnp.dot(qk, ub, preferred_element_type=jnp.float32))
        b_last = bc[L - 1:L, :]
        kw = (jnp.exp(b_last - bc) * k).astype(jnp.bfloat16)
        s_scr[h] = jnp.exp(b_last) * s + _dot_tn(kw, ub)

    @pl.when(c == pl.num_programs(2) - 1)
    def _():
        s_out_ref[...] = s_scr[...]


def gdn_scan_call(qkv, beta, g, s0):
    b, t, _ = qkv.shape
    hb = GDN_HEADS_PER_STEP
    ng = GDN_HEADS // hb
    L = GDN_CHUNK
    w = hb * GDN_DK

    def qkv_spec(part):
        return pl.BlockSpec((None, L, w), lambda i, j, c: (i, c, part * ng + j))

    def per_chunk(z):
        return z.reshape(b, t // L, L, ng, hb).transpose(0, 3, 1, 2, 4)

    col_spec = pl.BlockSpec((None, None, None, L, hb), lambda i, j, c: (i, j, c, 0, 0))
    state_spec = pl.BlockSpec((None, hb, GDN_DK, GDN_DV), lambda i, j, c: (i, j, 0, 0))
    return pl.pallas_call(
        _gdn_body,
        out_shape=(jax.ShapeDtypeStruct((b, t, GDN_HEADS * GDN_DV), jnp.float32),
                   jax.ShapeDtypeStruct((b, GDN_HEADS, GDN_DK, GDN_DV), jnp.float32)),
        grid=(b, ng, t // L),
        in_specs=[qkv_spec(0), qkv_spec(1), qkv_spec(2), col_spec, col_spec,
                  pl.BlockSpec((None, None, None, hb, L), lambda i, j, c: (i, j, c, 0, 0)),
                  state_spec],
        out_specs=(pl.BlockSpec((None, L, w), lambda i, j, c: (i, c, j)), state_spec),
        scratch_shapes=[pltpu.VMEM((hb, GDN_DK, GDN_DV), jnp.float32)],
        compiler_params=pltpu.CompilerParams(dimension_semantics=("arbitrary", "arbitrary", "arbitrary"),
                                             vmem_limit_bytes=VMEM_LIMIT_BYTES),
        name="gdn_scan",
    )(qkv, qkv, qkv, per_chunk(beta), per_chunk(g), per_chunk(g).swapaxes(3, 4), s0)


def gdn_mixer(h, w_in, w_conv, a_log, dt_bias, g_out, w_out, S0, conv0):
    b, t, _ = h.shape
    qkv, beta, a, gate = jnp.split(h @ w_in, [GDN_CONV_DIM, GDN_CONV_DIM + GDN_HEADS, GDN_CONV_DIM + 2 * GDN_HEADS], axis=-1)
    xc = jnp.concatenate([conv0.astype(qkv.dtype), qkv], axis=1)
    conv = lax.conv_general_dilated(xc, w_conv[:, None, :].astype(xc.dtype), (1,), 'VALID',
                                    dimension_numbers=('NWC', 'WIO', 'NWC'), feature_group_count=GDN_CONV_DIM)
    act = jax.nn.silu(conv)
    beta = jax.nn.sigmoid(beta.astype(jnp.float32))
    g = -jnp.exp(a_log.astype(jnp.float32)) * jax.nn.softplus((a + dt_bias).astype(jnp.float32))
    pad = (-t) % GDN_CHUNK
    if pad:
        act, beta, g = (jnp.pad(z, ((0, 0), (0, pad), (0, 0))) for z in (act, beta, g))
    o, S = gdn_scan_call(act, beta, g, S0.astype(jnp.float32))
    o = o[:, :t].reshape(b, t, GDN_HEADS, GDN_DV)
    o = rmsnorm(o, g_out).astype(h.dtype).reshape(b, t, GDN_HEADS * GDN_DV)
    return (o * jax.nn.silu(gate)) @ w_out, S.astype(h.dtype), xc[:, -(GDN_CONV - 1):]


def mla_project(h, pos, w_in, g_qa, g_kva, w_uq, g_qk):
    b, t, _ = h.shape
    cq, ckv, kr = jnp.split(h @ w_in, [MLA_Q_LORA, MLA_Q_LORA + MLA_KV_LORA], axis=-1)
    q = (rmsnorm(cq, g_qa) @ w_uq).reshape(b, t, MLA_HEADS, MLA_NOPE + MLA_ROPE)
    q_nope = rmsnorm(q[..., :MLA_NOPE], g_qk[0, :MLA_NOPE])
    q_rope = rope(rmsnorm(q[..., MLA_NOPE:], g_qk[0, MLA_NOPE:]), pos)
    k_rope = rope(rmsnorm(kr, g_qk[1, MLA_NOPE:]), pos)
    latent = jnp.concatenate([rmsnorm(ckv, g_kva), k_rope], axis=-1)
    return q_nope, q_rope, latent


def mla_scores(q_nope, q_rope, k_nope, k_rope):
    s = jnp.einsum('bthd,bshd->bhts', q_nope, k_nope) + jnp.einsum('bthr,bsr->bhts', q_rope, k_rope)
    return s.astype(jnp.float32) * MLA_SCALE


def mla_keys(latent, w_uk, g_kn):
    ckv = latent[..., :MLA_KV_LORA]
    return ckv, rmsnorm(jnp.einsum('bsc,chd->bshd', ckv, w_uk), g_kn), latent[..., MLA_KV_LORA:]


def _mla_prompt_body(q_ref, k_ref, v_ref, o_ref, m_scr, acc_scr):
    qi = pl.program_id(2)
    t0 = qi * MLA_TQ
    tpos = t0 + lax.broadcasted_iota(jnp.int32, (MLA_TQ, 1), 0)

    def causal(k0):
        return (k0 + lax.broadcasted_iota(jnp.int32, (1, FLASH_TK), 1)) <= tpos

    n_full = t0 // FLASH_TK
    outs = []
    for h in range(2):
        _flash_init(m_scr, acc_scr)
        _flash_pass(q_ref[h], k_ref.at[h], v_ref.at[h], 0, n_full, None, 1, m_scr, acc_scr)
        _flash_pass(q_ref[h], k_ref.at[h], v_ref.at[h], n_full, n_full + MLA_TQ // FLASH_TK, causal, 1,
                    m_scr, acc_scr)
        outs.append(_flash_result(acc_scr, MLA_VD))
    o_ref[...] = jnp.concatenate(outs, axis=1)


def mla_prompt_attn(q, k, v):
    b, h, t, _ = q.shape
    assert t % MLA_TQ == 0 and MLA_TQ % FLASH_TK == 0 and h % 2 == 0
    seq_spec = pl.BlockSpec((None, 2, t, LANES), lambda i, j, k: (i, j, 0, 0))
    return pl.pallas_call(
        _mla_prompt_body,
        out_shape=jax.ShapeDtypeStruct((b, t, h * MLA_VD), jnp.float32),
        grid=(b, h // 2, t // MLA_TQ),
        in_specs=[pl.BlockSpec((None, 2, MLA_TQ, LANES), lambda i, j, k: (i, j, k, 0)), seq_spec, seq_spec],
        out_specs=pl.BlockSpec((None, MLA_TQ, 2 * MLA_VD), lambda i, j, k: (i, k, j)),
        scratch_shapes=[pltpu.VMEM((MLA_TQ, LANES), jnp.float32), pltpu.VMEM((MLA_TQ, LANES), jnp.float32)],
        compiler_params=pltpu.CompilerParams(dimension_semantics=("arbitrary", "arbitrary", "arbitrary"),
                                             vmem_limit_bytes=VMEM_LIMIT_BYTES),
        name="mla_prompt_attn",
    )(q, k, v)


def mla_prompt(h, w_in, g_qa, g_kva, w_uq, w_uk, w_uv, g_qk, w_out):
    b, t, _ = h.shape
    q_nope, q_rope, latent = mla_project(h, jnp.arange(t, dtype=jnp.float32), w_in, g_qa, g_kva, w_uq, g_qk)
    ckv, k_nope, k_rope = mla_keys(latent, w_uk, g_qk[1, :MLA_NOPE])
    v = jnp.einsum('bsc,chd->bshd', ckv, w_uv)

    def heads_first(x):
        return x.transpose(0, 2, 1, 3)

    pad = jnp.zeros((b, MLA_HEADS, t, LANES - MLA_NOPE - MLA_ROPE), jnp.float32)
    q = jnp.concatenate([heads_first(q_nope), heads_first(q_rope), pad], axis=-1) * MLA_SCALE
    k = jnp.concatenate([heads_first(k_nope), jnp.broadcast_to(k_rope[:, None], (b, MLA_HEADS, t, MLA_ROPE)), pad],
                        axis=-1)
    v1 = jnp.concatenate([heads_first(v), jnp.ones((b, MLA_HEADS, t, 1), jnp.float32),
                          jnp.zeros((b, MLA_HEADS, t, LANES - MLA_VD - 1), jnp.float32)], axis=-1)
    o = mla_prompt_attn(q.astype(jnp.bfloat16), k.astype(jnp.bfloat16), v1.astype(jnp.bfloat16))
    return o @ w_out, latent


def _mla_page_scores(lat, w_uk_ref, qbd, gexp, qr):
    ckv = lat[:, :MLA_KV_LORA].astype(jnp.bfloat16)
    kn = jnp.dot(ckv, w_uk_ref[...], preferred_element_type=jnp.float32)
    s_n = _dot_nt(kn.astype(jnp.bfloat16), qbd)
    ms = jnp.dot((kn * kn).astype(jnp.bfloat16), gexp, preferred_element_type=jnp.float32)
    s_r = _dot_nt(lat[:, MLA_KV_LORA:].astype(jnp.bfloat16), qr)
    return ckv, (s_n * lax.rsqrt(ms + RMS_EPS) + s_r) * MLA_SCALE


def _mla_page_rows(page_ref):
    x = page_ref[...]
    tail = x[MLA_LAT - LANES:, :].T
    return jnp.concatenate([x[:LANES, :].T, x[LANES:2 * LANES, :].T, tail[:, LANES - MLA_ROPE:]], axis=1)


def _mla_sample_body(pt_ref, *refs):
    pp = MLA_PAGES_PER_STEP
    page_refs = refs[:pp]
    qn_ref, qr_ref, latn_ref, w_uk_ref, w_uv_ref, o_ref, ckv_scr, s_scr, qbd_scr = refs[pp:]
    j = pl.program_id(1)
    nj = pl.num_programs(1)
    nq = DEC_SEQ * MLA_HEADS
    hw = MLA_HEADS * MLA_NOPE
    n = pp * PAGE_SIZE
    n_past = nj * n

    @pl.when(j == 0)
    def _():
        r = lax.broadcasted_iota(jnp.int32, (nq, hw), 0)
        c = lax.broadcasted_iota(jnp.int32, (nq, hw), 1)
        q_rep = jnp.concatenate([qn_ref[...]] * MLA_HEADS, axis=1)
        qbd_scr[...] = jnp.where(r % MLA_HEADS == c // MLA_NOPE, q_rep, 0.0).astype(jnp.bfloat16)

    gr = lax.broadcasted_iota(jnp.int32, (hw, nq), 0)
    gc = lax.broadcasted_iota(jnp.int32, (hw, nq), 1)
    gexp = jnp.where(gr // MLA_NOPE == gc % MLA_HEADS, 1.0 / MLA_NOPE, 0.0).astype(jnp.bfloat16)
    qr = qr_ref[...].astype(jnp.bfloat16)
    lat = jnp.concatenate([_mla_page_rows(p) for p in page_refs], axis=0)
    ckv, s = _mla_page_scores(lat, w_uk_ref, qbd_scr[...], gexp, qr)
    row0 = pl.multiple_of(j * n, n)
    ones_col = (lax.broadcasted_iota(jnp.int32, (1, LANES), 1) == 0).astype(jnp.bfloat16)
    ckv_scr[pl.ds(row0, n), :MLA_KV_LORA] = ckv
    ckv_scr[pl.ds(row0, n), MLA_KV_LORA:] = jnp.broadcast_to(ones_col, (n, LANES))
    s_scr[pl.ds(row0, n), :] = s

    @pl.when(j == nj - 1)
    def _():
        ckv_n, s_n = _mla_page_scores(latn_ref[...], w_uk_ref, qbd_scr[...], gexp, qr)
        kr = lax.broadcasted_iota(jnp.int32, (MLA_NEW_ROWS, nq), 0)
        kc = lax.broadcasted_iota(jnp.int32, (MLA_NEW_ROWS, nq), 1)
        s_scr[n_past:, :] = jnp.where(kr <= kc // MLA_HEADS, s_n, NEG)
        ckv_scr[n_past:, :MLA_KV_LORA] = ckv_n
        ckv_scr[n_past:, MLA_KV_LORA:] = jnp.broadcast_to(ones_col, (MLA_NEW_ROWS, LANES))
        sc = s_scr[...]
        p = jnp.exp(sc - jnp.max(sc, axis=0, keepdims=True)).astype(jnp.bfloat16)
        acc = _dot_tn(p, ckv_scr[...])
        o_lat = (acc[:, :MLA_KV_LORA] / acc[:, MLA_KV_LORA:MLA_KV_LORA + 1]).astype(jnp.bfloat16)
        z = jnp.dot(o_lat, w_uv_ref[...], preferred_element_type=jnp.float32)
        zr = lax.broadcasted_iota(jnp.int32, (MLA_HEADS, MLA_HEADS * MLA_VD), 0)
        zc = lax.broadcasted_iota(jnp.int32, (MLA_HEADS, MLA_HEADS * MLA_VD), 1)
        keep = zr == zc // MLA_VD
        o_ref[...] = jnp.concatenate(
            [jnp.sum(jnp.where(keep, z[t * MLA_HEADS:(t + 1) * MLA_HEADS], 0.0), axis=0, keepdims=True)
             for t in range(DEC_SEQ)], axis=0)


def mla_sample_attn(pool, li, page_table, qn, qr, latent, w_uk, w_uv):
    b, n_pages = page_table.shape
    pp = MLA_PAGES_PER_STEP
    assert n_pages % pp == 0 and latent.shape[1] == DEC_SEQ
    nq = DEC_SEQ * MLA_HEADS
    n_rows = n_pages * PAGE_SIZE + MLA_NEW_ROWS
    latn = jnp.pad(latent, ((0, 0), (0, MLA_NEW_ROWS - DEC_SEQ), (0, 0)))

    def page_spec(k):
        return pl.BlockSpec((None, None, MLA_LAT, PAGE_SIZE), lambda i, j, pt: (li, pt[i, j * pp + k], 0, 0))

    def whole(r, c):
        return pl.BlockSpec((r, c), lambda i, j, pt: (0, 0))

    def per_seq(r, c):
        return pl.BlockSpec((None, r, c), lambda i, j, pt: (i, 0, 0))

    grid_spec = pltpu.PrefetchScalarGridSpec(
        num_scalar_prefetch=1,
        grid=(b, n_pages // pp),
        in_specs=[page_spec(k) for k in range(pp)] + [
            per_seq(nq, MLA_NOPE), per_seq(nq, MLA_ROPE), per_seq(MLA_NEW_ROWS, MLA_LAT),
            whole(MLA_KV_LORA, MLA_HEADS * MLA_NOPE), whole(MLA_KV_LORA, MLA_HEADS * MLA_VD)],
        out_specs=per_seq(DEC_SEQ, MLA_HEADS * MLA_VD),
        scratch_shapes=[pltpu.VMEM((n_rows, MLA_KV_LORA + LANES), jnp.bfloat16),
                        pltpu.VMEM((n_rows, nq), jnp.float32),
                        pltpu.VMEM((nq, MLA_HEADS * MLA_NOPE), jnp.bfloat16)])
    return pl.pallas_call(
        _mla_sample_body,
        out_shape=jax.ShapeDtypeStruct((b, DEC_SEQ, MLA_HEADS * MLA_VD), jnp.float32),
        grid_spec=grid_spec,
        compiler_params=pltpu.CompilerParams(dimension_semantics=("arbitrary", "arbitrary"),
                                             vmem_limit_bytes=VMEM_LIMIT_BYTES),
        name="mla_sample_attn",
    )(page_table, *([pool] * pp), qn, qr, latn, w_uk, w_uv)


def mla_sample(h, pool, li, page_table, w_in, g_qa, g_kva, w_uq, w_uk, w_uv, g_qk, w_out):
    b, t, _ = h.shape
    q_nope, q_rope, latent = mla_project(h, PAST_LEN + jnp.arange(t, dtype=jnp.float32), w_in, g_qa, g_kva, w_uq, g_qk)
    qn = (q_nope * g_qk[1, :MLA_NOPE]).reshape(b, t * MLA_HEADS, MLA_NOPE)
    qr = q_rope.reshape(b, t * MLA_HEADS, MLA_ROPE)
    o = mla_sample_attn(pool.transpose(0, 1, 3, 2), li, page_table, qn, qr, latent,
                        w_uk.reshape(MLA_KV_LORA, -1).astype(jnp.bfloat16),
                        w_uv.reshape(MLA_KV_LORA, -1).astype(jnp.bfloat16))
    return o @ w_out, latent


def nsa_project(h, w_in, b_gate, g_qk):
    b, t, _ = h.shape
    qw, kvw = NSA_HEADS * NSA_HD, 6 * NSA_GROUPS * NSA_HD
    q, kv, gates = jnp.split(h @ w_in, [qw, qw + kvw], axis=-1)
    q = rmsnorm(q.reshape(b, t, NSA_GROUPS, NSA_HPG, NSA_HD), g_qk[0])
    kv = kv.reshape(b, t, 6, NSA_GROUPS, NSA_HD)
    rows = jnp.stack([kv[:, :, 0], kv[:, :, 1], rmsnorm(kv[:, :, 2], g_qk[1]), kv[:, :, 3]], axis=2)
    win = jnp.stack([rmsnorm(kv[:, :, 4], g_qk[1]), kv[:, :, 5]], axis=2)
    gates = jax.nn.sigmoid(gates + b_gate).reshape(b, t, NSA_GROUPS, NSA_HPG, 3)
    return q, rows, win, gates


def nsa_compress(kv_rows, w_cmp, g_k):
    b, l = kv_rows.shape[:2]
    blk = kv_rows.reshape(b, l // NSA_BLOCK, NSA_BLOCK, 2, NSA_GROUPS, NSA_HD)
    c = jnp.einsum('bnpcgd,cpde->bncge', blk, w_cmp)
    return rmsnorm(c[:, :, 0], g_k), c[:, :, 1]


def nsa_cmp_attn(q, k_c, v_c, qpos):
    nb = k_c.shape[1]
    s = jnp.einsum('btghd,bngd->bghtn', q, k_c) * NSA_SCALE
    mask = ((jnp.arange(nb) + 1) * NSA_BLOCK - 1)[None, :] <= qpos[:, None]
    p = masked_softmax(s, mask)
    o = jnp.einsum('bghtn,bngd->btghd', p.astype(v_c.dtype), v_c)
    return o, jnp.sum(p, axis=2)


def nsa_select(imp, qpos):
    nb = imp.shape[-1]
    cand = jnp.arange(nb)[None, :] < (qpos // NSA_BLOCK)[:, None]
    val, idx = lax.top_k(jnp.where(cand, imp, -jnp.inf), min(NSA_TOPK, nb))
    return idx, jnp.isfinite(val)


def nsa_sel_prompt(q, k_s, v_s, idx, valid):
    b, t = q.shape[:2]
    nb = t // NSA_BLOCK
    kb = k_s.reshape(b, nb, NSA_BLOCK, NSA_GROUPS, NSA_HD).transpose(0, 3, 1, 2, 4)
    vb = v_s.reshape(b, nb, NSA_BLOCK, NSA_GROUPS, NSA_HD).transpose(0, 3, 1, 2, 4)
    bi = jnp.arange(b)[:, None, None, None]
    gi = jnp.arange(NSA_GROUPS)[None, None, :, None]
    idx, valid = idx.transpose(0, 2, 1, 3), valid.transpose(0, 2, 1, 3)

    def block(i):
        t0 = i * NSA_QB
        qpos = t0 + jnp.arange(NSA_QB)
        qq = lax.dynamic_slice_in_dim(q, t0, NSA_QB, axis=1)
        cur = jnp.broadcast_to((qpos // NSA_BLOCK)[None, :, None, None], (b, NSA_QB, NSA_GROUPS, 1))
        ii = jnp.concatenate([lax.dynamic_slice_in_dim(idx, t0, NSA_QB, axis=1), cur], axis=-1)
        ok = jnp.concatenate([lax.dynamic_slice_in_dim(valid, t0, NSA_QB, axis=1), jnp.ones(cur.shape, bool)], axis=-1)
        kpos = ii[..., None] * NSA_BLOCK + jnp.arange(NSA_BLOCK)
        mask = ok[..., None] & (kpos <= qpos[None, :, None, None, None])
        n = ii.shape[-1] * NSA_BLOCK
        ks, vs = kb[bi, gi, ii], vb[bi, gi, ii]
        s = jnp.einsum('btghd,btgkpd->bghtkp', qq, ks).reshape(b, NSA_GROUPS, NSA_HPG, NSA_QB, n) * NSA_SCALE
        mask = mask.transpose(0, 2, 1, 3, 4).reshape(b, NSA_GROUPS, 1, NSA_QB, n)
        p = masked_softmax(s, mask)
        return jnp.einsum('bghtn,btgnd->btghd', p.astype(vs.dtype), vs.reshape(b, NSA_QB, NSA_GROUPS, n, NSA_HD))

    return jnp.moveaxis(lax.map(block, jnp.arange(t // NSA_QB)), 0, 1).reshape(q.shape)


def nsa_sel_sample(q, k_new, v_new, idx, valid, pool, li, page_table):
    b, t = q.shape[:2]
    bpp = PAGE_SIZE // NSA_BLOCK
    kk = idx.shape[-1]
    n = kk * NSA_BLOCK
    ii, ok = idx.transpose(0, 2, 1, 3), valid.transpose(0, 2, 1, 3)
    phys = page_table[jnp.arange(b)[:, None, None, None], ii // bpp][..., None]
    rows = (ii % bpp)[..., None] * NSA_BLOCK + jnp.arange(NSA_BLOCK)
    gi = jnp.arange(NSA_GROUPS)[None, None, :, None, None]
    ks = pool[li, phys, rows, 2, gi].reshape(b, t, NSA_GROUPS, n, NSA_HD)
    vs = pool[li, phys, rows, 3, gi].reshape(b, t, NSA_GROUPS, n, NSA_HD)
    s = jnp.concatenate([jnp.einsum('btghd,btgnd->bghtn', q, ks), jnp.einsum('btghd,bsgd->bghts', q, k_new)], axis=-1)
    mask_past = jnp.repeat(ok, NSA_BLOCK, axis=-1).transpose(0, 2, 1, 3)[:, :, None]
    mask_new = jnp.broadcast_to(jnp.tril(jnp.ones((t, t), bool)), (b, NSA_GROUPS, 1, t, t))
    p = masked_softmax(s * NSA_SCALE, jnp.concatenate([mask_past, mask_new], axis=-1)).astype(v_new.dtype)
    return (jnp.einsum('bghtn,btgnd->btghd', p[..., :n], vs)
            + jnp.einsum('bghts,bsgd->btghd', p[..., n:], v_new))


def nsa_win_prompt(q, k_w, v_w):
    t = q.shape[1]
    pad = ((0, 0), (NSA_WINDOW, 0), (0, 0), (0, 0))
    kp, vp = jnp.pad(k_w, pad), jnp.pad(v_w, pad)
    span = NSA_WINDOW + Q_BLOCK

    def block(i):
        t0 = i * Q_BLOCK
        qq = lax.dynamic_slice_in_dim(q, t0, Q_BLOCK, axis=1)
        kk = lax.dynamic_slice_in_dim(kp, t0, span, axis=1)
        vv = lax.dynamic_slice_in_dim(vp, t0, span, axis=1)
        qpos = t0 + jnp.arange(Q_BLOCK)
        kpos = t0 - NSA_WINDOW + jnp.arange(span)
        mask = ((kpos[None, :] <= qpos[:, None]) & (kpos[None, :] > qpos[:, None] - NSA_WINDOW)
                & (kpos[None, :] >= 0))
        p = masked_softmax(jnp.einsum('bqghd,bkgd->bghqk', qq, kk) * NSA_SCALE, mask)
        return jnp.einsum('bghqk,bkgd->bqghd', p.astype(vv.dtype), vv)

    return jnp.moveaxis(lax.map(block, jnp.arange(t // Q_BLOCK)), 0, 1).reshape(q.shape)


def nsa_win_sample(q, win, buf):
    t = q.shape[1]
    wb = buf.shape[1]
    kv = jnp.concatenate([buf.astype(win.dtype), win], axis=1)
    qpos = PAST_LEN + jnp.arange(t)
    kpos = PAST_LEN - wb + jnp.arange(wb + t)
    mask = (kpos[None, :] <= qpos[:, None]) & (kpos[None, :] > qpos[:, None] - NSA_WINDOW)
    p = masked_softmax(jnp.einsum('btghd,bsgd->bghts', q, kv[:, :, 0]) * NSA_SCALE, mask)
    o = jnp.einsum('bghts,bsgd->btghd', p.astype(kv.dtype), kv[:, :, 1])
    return o, kv[:, -wb:]


def nsa_merge(gates, o_cmp, o_sel, o_win):
    o = gates[..., 0:1] * o_cmp + gates[..., 1:2] * o_sel + gates[..., 2:3] * o_win
    return o.reshape(o.shape[0], o.shape[1], NSA_HEADS * NSA_HD)


def _nsa_body(q_ref, rows_ref, win_ref, kc_ref, vc_ref, gates_ref, o_ref, kv_scr, m_scr, acc_scr):
    qi = pl.program_id(1)
    nb = kc_ref.shape[0]
    gw = NSA_GROUPS * NSA_HD

    @pl.when(qi == 0)
    def _():
        lane = lax.broadcasted_iota(jnp.int32, (1, LANES), 1)
        one_col = (lane == NSA_HD).astype(jnp.float32)
        srcs = (rows_ref[:, 2 * gw:3 * gw], rows_ref[:, 3 * gw:4 * gw], win_ref[:, 0:gw], win_ref[:, gw:2 * gw])
        for i, x in enumerate(srcs):
            for g in range(NSA_GROUPS):
                xg = x if g == 0 else pltpu.roll(x, NSA_HD, 1)
                kv_scr[i * NSA_GROUPS + g] = jnp.where(lane < NSA_HD, xg, one_col if i % 2 else 0.0).astype(jnp.bfloat16)

    t0 = qi * NSA_TQ
    tpos = t0 + lax.broadcasted_iota(jnp.int32, (NSA_TQ, 1), 0)
    blk_t = tpos // NSA_BLOCK
    nidx = lax.broadcasted_iota(jnp.int32, (1, nb), 1)
    rows = NSA_HPG * NSA_TQ
    gq = NSA_HPG * NSA_HD
    zpad = jnp.zeros((NSA_TQ, LANES - NSA_HD), jnp.float32)
    for g in range(NSA_GROUPS):
        lanes = slice(g * NSA_HD, (g + 1) * NSA_HD)
        qg = q_ref[:, g * gq:(g + 1) * gq] * NSA_SCALE
        q_all = jnp.concatenate([jnp.concatenate([qg[:, h * NSA_HD:(h + 1) * NSA_HD], zpad], axis=1)
                                 for h in range(NSA_HPG)], axis=0).astype(jnp.bfloat16)

        kc = jnp.concatenate([kc_ref[:, lanes], jnp.zeros((nb, LANES - NSA_HD), jnp.float32)],
                             axis=1).astype(jnp.bfloat16)
        vc = vc_ref[:, lanes].astype(jnp.bfloat16)
        cmask = ((nidx + 1) * NSA_BLOCK - 1) <= tpos
        s = jnp.where(cmask[None], _dot_nt(q_all, kc).reshape(NSA_HPG, NSA_TQ, nb), NEG)
        e = jnp.exp(s - jnp.max(s, axis=-1, keepdims=True))
        p = e / jnp.sum(e, axis=-1, keepdims=True) * cmask[None].astype(jnp.float32)
        imp = jnp.sum(p, axis=0)
        o_cmp = jnp.dot(p.reshape(rows, nb).astype(jnp.bfloat16), vc, preferred_element_type=jnp.float32)

        rank = jnp.zeros((NSA_TQ, nb), jnp.int32)
        for m in range(nb):
            col = imp[:, m:m + 1]
            beats = (col > imp) | ((col == imp) & (m < nidx))
            rank = rank + jnp.where(beats & (m < blk_t), 1, 0)
        sel_f = (((nidx < blk_t) & (rank < NSA_TOPK)) | (nidx == blk_t)).astype(jnp.float32)
        blk_id = lax.broadcasted_iota(jnp.int32, (nb, FLASH_TK), 0)

        def sel_mask(k0):
            spos = k0 + lax.broadcasted_iota(jnp.int32, (1, FLASH_TK), 1)
            expand = (blk_id == spos // NSA_BLOCK).astype(jnp.float32)
            chosen = jnp.dot(sel_f, expand, preferred_element_type=jnp.float32) > 0.5
            return chosen & (spos <= tpos)

        def win_mask(k0):
            spos = k0 + lax.broadcasted_iota(jnp.int32, (1, FLASH_TK), 1)
            return (spos <= tpos) & (spos > tpos - NSA_WINDOW)

        kt_hi = (t0 + NSA_TQ + FLASH_TK - 1) // FLASH_TK
        _flash_init(m_scr, acc_scr)
        _flash_pass(q_all, kv_scr.at[g], kv_scr.at[NSA_GROUPS + g], 0, kt_hi, sel_mask, NSA_HPG, m_scr, acc_scr)
        o_sel = _flash_result(acc_scr, NSA_HD)
        kt_lo = jnp.maximum(t0 - (NSA_WINDOW - 1), 0) // FLASH_TK
        _flash_init(m_scr, acc_scr)
        _flash_pass(q_all, kv_scr.at[2 * NSA_GROUPS + g], kv_scr.at[3 * NSA_GROUPS + g], kt_lo, kt_hi, win_mask,
                    NSA_HPG, m_scr, acc_scr)
        o_win = _flash_result(acc_scr, NSA_HD)

        outs = []
        for h in range(NSA_HPG):
            r = slice(h * NSA_TQ, (h + 1) * NSA_TQ)
            c = (g * NSA_HPG + h) * 3
            outs.append(gates_ref[:, c:c + 1] * o_cmp[r] + gates_ref[:, c + 1:c + 2] * o_sel[r]
                        + gates_ref[:, c + 2:c + 3] * o_win[r])
        o_ref[:, g * gq:(g + 1) * gq] = jnp.concatenate(outs, axis=1)


def nsa_prompt_attn(q, rows, win, k_c, v_c, gates):
    b, t, _ = q.shape
    nb = t // NSA_BLOCK
    assert t % FLASH_TK == 0 and t % NSA_TQ == 0
    gw = NSA_GROUPS * NSA_HD
    qw = NSA_HEADS * NSA_HD
    return pl.pallas_call(
        _nsa_body,
        out_shape=jax.ShapeDtypeStruct((b, t, qw), jnp.float32),
        grid=(b, t // NSA_TQ),
        in_specs=[pl.BlockSpec((None, NSA_TQ, qw), lambda i, j: (i, j, 0)),
                  pl.BlockSpec((None, t, 4 * gw), lambda i, j: (i, 0, 0)),
                  pl.BlockSpec((None, t, 2 * gw), lambda i, j: (i, 0, 0)),
                  pl.BlockSpec((None, nb, gw), lambda i, j: (i, 0, 0)),
                  pl.BlockSpec((None, nb, gw), lambda i, j: (i, 0, 0)),
                  pl.BlockSpec((None, NSA_TQ, NSA_HEADS * 3), lambda i, j: (i, j, 0))],
        out_specs=pl.BlockSpec((None, NSA_TQ, qw), lambda i, j: (i, j, 0)),
        scratch_shapes=[pltpu.VMEM((4 * NSA_GROUPS, t, LANES), jnp.bfloat16),
                        pltpu.VMEM((NSA_HPG * NSA_TQ, LANES), jnp.float32),
                        pltpu.VMEM((NSA_HPG * NSA_TQ, LANES), jnp.float32)],
        compiler_params=pltpu.CompilerParams(dimension_semantics=("arbitrary", "arbitrary"),
                                             vmem_limit_bytes=VMEM_LIMIT_BYTES),
        name="nsa_prompt_attn",
    )(q, rows, win, k_c, v_c, gates)


def nsa_prompt(h, w_in, b_gate, g_qk, w_cmp, w_out):
    b, t, _ = h.shape
    q, rows, win, gates = nsa_project(h, w_in, b_gate, g_qk)
    k_c, v_c = nsa_compress(rows[:, :, :2], w_cmp, g_qk[1])
    nb = t // NSA_BLOCK
    o = nsa_prompt_attn(q.reshape(b, t, -1), rows.reshape(b, t, -1), win.reshape(b, t, -1),
                        k_c.reshape(b, nb, -1), v_c.reshape(b, nb, -1), gates.reshape(b, t, -1))
    return o @ w_out, rows, win[:, -min(NSA_WINDOW, t):]


def _masked_softmax_rows(s, mask):
    sm = jnp.where(mask, s, NEG)
    e = jnp.exp(sm - jnp.max(sm, axis=-1, keepdims=True))
    return e / jnp.sum(e, axis=-1, keepdims=True) * mask.astype(jnp.float32)


def _nsa_sample_body(pt_ref, *refs):
    n_pages = len(refs) - 15
    pages = refs[:n_pages]
    (q_ref, knew_ref, vnew_ref, win_ref, kwnew_ref, vwnew_ref, gates_ref, wcmp_ref, gk_ref,
     o_ref, cmp_scr, kv_scr, sel_scr, ocmp_scr, s_scr) = refs[n_pages:]
    ph = pl.program_id(1)
    gd = NSA_GROUPS * NSA_HD
    half = 2 * gd
    rows = DEC_SEQ * NSA_HPG
    n_past = n_pages * PAGE_SIZE
    trow = lax.broadcasted_iota(jnp.int32, (rows, 1), 0) // NSA_HPG
    lane = lax.broadcasted_iota(jnp.int32, (1, LANES), 1)
    own_mask = (lane <= trow) & (lane < DEC_SEQ)

    @pl.when(ph == 0)
    def _():
        for p in range(n_pages):
            cmp_scr[p * half:(p + 1) * half, :] = pages[p][...]
        for c in range(2):
            acc = jnp.zeros((NSA_GROUPS * n_pages, LANES), jnp.float32)
            for d in range(NSA_HD):
                lhs = jnp.concatenate([cmp_scr[pl.ds((c * NSA_GROUPS + g) * NSA_HD + d, n_pages, stride=half), :]
                                       for g in range(NSA_GROUPS)], axis=0)
                acc = acc + jnp.dot(lhs.astype(jnp.bfloat16), wcmp_ref[c, d], preferred_element_type=jnp.float32)
            kv_scr[c] = acc
        blk_t = (PAST_LEN + lax.broadcasted_iota(jnp.int32, (DEC_SEQ, 1), 0)) // NSA_BLOCK
        blk = 2 * (lane % n_pages) + lane // n_pages
        cmask = ((blk + 1) * NSA_BLOCK - 1) <= (PAST_LEN + trow)
        for g in range(NSA_GROUPS):
            qg = (q_ref[g] * NSA_SCALE).astype(jnp.bfloat16)
            kc = kv_scr[0, g * n_pages:(g + 1) * n_pages, :]
            vc = kv_scr[1, g * n_pages:(g + 1) * n_pages, :]
            s_parts = []
            for n in range(2):
                x = kc[:, n * NSA_HD:(n + 1) * NSA_HD]
                kn = x * lax.rsqrt(jnp.mean(x * x, axis=-1, keepdims=True) + RMS_EPS) * gk_ref[...]
                s_parts.append(_dot_nt(qg, kn.astype(jnp.bfloat16)))
            p = _masked_softmax_rows(jnp.concatenate(s_parts, axis=1), cmask)
            imp = jnp.sum(p.reshape(DEC_SEQ, NSA_HPG, LANES), axis=1)
            pb = p.astype(jnp.bfloat16)
            ocmp_scr[g] = (
                jnp.dot(pb[:, :n_pages], vc[:, :NSA_HD].astype(jnp.bfloat16), preferred_element_type=jnp.float32)
                + jnp.dot(pb[:, n_pages:], vc[:, NSA_HD:].astype(jnp.bfloat16), preferred_element_type=jnp.float32))
            rank = jnp.zeros((DEC_SEQ, LANES), jnp.int32)
            for m in range(2 * n_pages):
                bm = 2 * (m % n_pages) + m // n_pages
                col = imp[:, m:m + 1]
                beats = (col > imp) | ((col == imp) & (bm < blk))
                rank = rank + jnp.where(beats & (bm < blk_t), 1, 0)
            sel_scr[g] = ((blk < blk_t) & (rank < NSA_TOPK)).astype(jnp.float32)

    @pl.when(ph == 1)
    def _():
        span = NSA_SEL_PAGES * PAGE_SIZE
        for g in range(NSA_GROUPS):
            qg = (q_ref[g] * NSA_SCALE).astype(jnp.bfloat16)
            sel_rows = jnp.broadcast_to(sel_scr[g][:, None, :], (DEC_SEQ, NSA_HPG, LANES)).reshape(rows, LANES)
            krows = slice(g * NSA_HD, (g + 1) * NSA_HD)
            vrows = slice(gd + g * NSA_HD, gd + (g + 1) * NSA_HD)
            for ch in range(n_pages // NSA_SEL_PAGES):
                ps = range(ch * NSA_SEL_PAGES, (ch + 1) * NSA_SEL_PAGES)
                k_t = jnp.concatenate([pages[p][krows, :] for p in ps], axis=1).astype(jnp.bfloat16)
                s = jnp.dot(qg, k_t, preferred_element_type=jnp.float32)
                chosen = jnp.concatenate(
                    [jnp.where(lane < NSA_BLOCK, sel_rows[:, p:p + 1], sel_rows[:, n_pages + p:n_pages + p + 1])
                     for p in ps], axis=1) > 0.5
                s_scr[:, ch * span:(ch + 1) * span] = jnp.where(chosen, s, NEG)
            s_new = jnp.dot(qg, knew_ref[g].astype(jnp.bfloat16), preferred_element_type=jnp.float32)
            s_scr[:, n_past:] = jnp.where(own_mask, s_new, NEG)
            sa = s_scr[...]
            e = jnp.exp(sa - jnp.max(sa, axis=-1, keepdims=True))
            pb = (e / jnp.sum(e, axis=-1, keepdims=True)).astype(jnp.bfloat16)
            o_sel = jnp.dot(pb[:, n_past:], vnew_ref[g].astype(jnp.bfloat16), preferred_element_type=jnp.float32)
            for ch in range(n_pages // NSA_SEL_PAGES):
                ps = range(ch * NSA_SEL_PAGES, (ch + 1) * NSA_SEL_PAGES)
                v_t = jnp.concatenate([pages[p][vrows, :] for p in ps], axis=1).astype(jnp.bfloat16)
                o_sel = o_sel + _dot_nt(pb[:, ch * span:(ch + 1) * span], v_t)

            wb = win_ref.shape[1]
            wpos = PAST_LEN - wb + lax.broadcasted_iota(jnp.int32, (1, wb), 1)
            sw = jnp.dot(qg, win_ref[krows, :].astype(jnp.bfloat16), preferred_element_type=jnp.float32)
            sw_new = jnp.dot(qg, kwnew_ref[g].astype(jnp.bfloat16), preferred_element_type=jnp.float32)
            wmask = jnp.concatenate([wpos > (PAST_LEN + trow - NSA_WINDOW), jnp.broadcast_to(own_mask, (rows, LANES))],
                                    axis=1)
            pw = _masked_softmax_rows(jnp.concatenate([sw, sw_new], axis=1), wmask).astype(jnp.bfloat16)
            o_win = (_dot_nt(pw[:, :wb], win_ref[vrows, :].astype(jnp.bfloat16))
                     + jnp.dot(pw[:, wb:], vwnew_ref[g].astype(jnp.bfloat16), preferred_element_type=jnp.float32))
            gt = gates_ref[g]
            o_ref[g] = gt[:, 0:1] * ocmp_scr[g] + gt[:, 1:2] * o_sel + gt[:, 2:3] * o_win


def nsa_sample_attn(pool_t, li, page_table, q, k_new, v_new, win_t, kw_new, vw_new, gates, w_cmp_t, g_k):
    b, n_pages = page_table.shape
    assert n_pages % NSA_SEL_PAGES == 0 and 2 * n_pages == LANES and PAGE_SIZE == 2 * NSA_BLOCK
    gd = NSA_GROUPS * NSA_HD
    rows = DEC_SEQ * NSA_HPG
    wb = win_t.shape[-1]

    def page_spec(p):
        return pl.BlockSpec((None, None, 2 * gd, PAGE_SIZE), lambda i, ph, pt: (li, pt[i, p], ph, 0))

    def per_seq(*shape):
        return pl.BlockSpec((None,) + shape, lambda i, ph, pt: (i,) + (0,) * len(shape))

    grid_spec = pltpu.PrefetchScalarGridSpec(
        num_scalar_prefetch=1,
        grid=(b, 2),
        in_specs=[page_spec(p) for p in range(n_pages)] + [
            per_seq(NSA_GROUPS, rows, NSA_HD), per_seq(NSA_GROUPS, NSA_HD, LANES), per_seq(NSA_GROUPS, LANES, NSA_HD),
            per_seq(2 * gd, wb), per_seq(NSA_GROUPS, NSA_HD, LANES), per_seq(NSA_GROUPS, LANES, NSA_HD),
            per_seq(NSA_GROUPS, rows, 3),
            pl.BlockSpec((2, NSA_HD, LANES, LANES), lambda i, ph, pt: (0, 0, 0, 0)),
            pl.BlockSpec((1, NSA_HD), lambda i, ph, pt: (0, 0))],
        out_specs=per_seq(NSA_GROUPS, rows, NSA_HD),
        scratch_shapes=[pltpu.VMEM((n_pages * 2 * gd, PAGE_SIZE), jnp.float32),
                        pltpu.VMEM((2, NSA_GROUPS * n_pages, LANES), jnp.float32),
                        pltpu.VMEM((NSA_GROUPS, DEC_SEQ, LANES), jnp.float32),
                        pltpu.VMEM((NSA_GROUPS, rows, NSA_HD), jnp.float32),
                        pltpu.VMEM((rows, n_pages * PAGE_SIZE + LANES), jnp.float32)])
    return pl.pallas_call(
        _nsa_sample_body,
        out_shape=jax.ShapeDtypeStruct((b, NSA_GROUPS, rows, NSA_HD), jnp.float32),
        grid_spec=grid_spec,
        compiler_params=pltpu.CompilerParams(dimension_semantics=("arbitrary", "arbitrary"),
                                             vmem_limit_bytes=VMEM_LIMIT_BYTES),
        name="nsa_sample_attn",
    )(page_table, *([pool_t] * n_pages), q, k_new, v_new, win_t, kw_new, vw_new, gates, w_cmp_t, g_k)


def nsa_cmp_block_weights(w_cmp):
    w = w_cmp.transpose(0, 2, 1, 3)
    z = jnp.zeros_like(w)
    return jnp.concatenate([jnp.concatenate([w, z], axis=-1), jnp.concatenate([z, w], axis=-1)],
                           axis=-2).astype(jnp.bfloat16)


def nsa_sample(h, pool, li, page_table, win_buf, w_in, b_gate, g_qk, w_cmp, w_out):
    b, t, _ = h.shape
    q, rows, win, gates = nsa_project(h, w_in, b_gate, g_qk)
    pool_t = pool.transpose(0, 1, 3, 4, 5, 2).reshape(pool.shape[0], pool.shape[1], 4 * NSA_GROUPS * NSA_HD, PAGE_SIZE)
    win_t = win_buf.transpose(0, 2, 3, 4, 1).reshape(b, 2 * NSA_GROUPS * NSA_HD, win_buf.shape[1])

    def keys_t(x):
        return jnp.pad(x.transpose(0, 2, 3, 1), ((0, 0), (0, 0), (0, 0), (0, LANES - t)))

    def vals(x):
        return jnp.pad(x.transpose(0, 2, 1, 3), ((0, 0), (0, 0), (0, LANES - t), (0, 0)))

    o = nsa_sample_attn(pool_t, li, page_table,
                        q.transpose(0, 2, 1, 3, 4).reshape(b, NSA_GROUPS, t * NSA_HPG, NSA_HD),
                        keys_t(rows[:, :, 2]), vals(rows[:, :, 3]), win_t, keys_t(win[:, :, 0]), vals(win[:, :, 1]),
                        gates.transpose(0, 2, 1, 3, 4).reshape(b, NSA_GROUPS, t * NSA_HPG, 3),
                        nsa_cmp_block_weights(w_cmp), g_qk[1].reshape(1, NSA_HD))
    o = o.reshape(b, NSA_GROUPS, t, NSA_HPG, NSA_HD).transpose(0, 2, 1, 3, 4).reshape(b, t, NSA_HEADS * NSA_HD)
    new_buf = jnp.concatenate([win_buf.astype(win.dtype), win], axis=1)[:, -win_buf.shape[1]:]
    return o @ w_out, rows, new_buf


def kernel(x_prompt, x_sample, state_mlstm_C, state_mlstm_n, state_mlstm_m, cache_mla, cache_nsa,
           state_nsa_win, state_gdn_S, state_gdn_conv, page_table, c_prompt, c_sample,
           w_ada, b_ada, g_norm, w_ffn_up, w_ffn_down,
           w_ml_in, b_ml_gate, g_ml_hn, w_ml_out,
           w_mla_in, g_mla_qa, g_mla_kva, w_mla_uq, w_mla_uk, w_mla_uv, g_mla_qk, w_mla_out,
           w_nsa_in, b_nsa_gate, g_nsa_qk, w_nsa_cmp, w_nsa_out,
           w_gdn_in, w_gdn_conv, a_gdn_log, b_gdn_dt, g_gdn_out, w_gdn_out):
    bp = x_prompt.shape[0]
    xp, xs = x_prompt, x_sample
    ml_p, ml_s, mla_p, mla_s, nsa_p, nsa_s, gdn_p, gdn_s = [], [], [], [], [], [], [], []
    w_up_bf = w_ffn_up.astype(jnp.bfloat16)
    w_dn_bf = w_ffn_down.astype(jnp.bfloat16)
    for i in range(DEPTH):
        kind, li = i % N_MIXERS, i // N_MIXERS
        mod_p = adaln(c_prompt, w_ada[i], b_ada[i])
        mod_s = adaln(c_sample, w_ada[i], b_ada[i])
        xp = ffn_half_prompt(xp, g_norm[i, 0], mod_p, 0, w_up_bf[i, 0], w_dn_bf[i, 0])
        xs = ffn_half_sample(xs, g_norm[i, 0], mod_s, 0, w_up_bf[i, 0], w_dn_bf[i, 0])
        hp = modulate(xp, g_norm[i, 1], mod_p, 1)
        hs = modulate(xs, g_norm[i, 1], mod_s, 1)
        if kind == 0:
            wts = (w_ml_in[li], b_ml_gate[li], g_ml_hn[li], w_ml_out[li])
            op, c_p, n_p, m_p = mlstm_mixer(hp, *wts, jnp.zeros((bp, ML_HEADS, ML_DQK, ML_DV), jnp.float32),
                                            jnp.zeros((bp, ML_HEADS, ML_DQK), jnp.float32),
                                            jnp.zeros((bp, ML_HEADS), jnp.float32))
            os_, c_s, n_s, m_s = mlstm_mixer(hs, *wts, state_mlstm_C[li], state_mlstm_n[li], state_mlstm_m[li])
            ml_p.append((c_p, n_p, m_p))
            ml_s.append((c_s, n_s, m_s))
        elif kind == 1:
            wts = (w_mla_in[li], g_mla_qa[li], g_mla_kva[li], w_mla_uq[li], w_mla_uk[li], w_mla_uv[li],
                   g_mla_qk[li], w_mla_out[li])
            op, r_p = mla_prompt(hp, *wts)
            os_, r_s = mla_sample(hs, cache_mla, li, page_table, *wts)
            mla_p.append((r_p,))
            mla_s.append((r_s,))
        elif kind == 2:
            wts = (w_nsa_in[li], b_nsa_gate[li], g_nsa_qk[li], w_nsa_cmp[li], w_nsa_out[li])
            op, r_p, wb_p = nsa_prompt(hp, *wts)
            os_, r_s, wb_s = nsa_sample(hs, cache_nsa, li, page_table, state_nsa_win[li], *wts)
            nsa_p.append((r_p, wb_p))
            nsa_s.append((r_s, wb_s))
        else:
            wts = (w_gdn_in[li], w_gdn_conv[li], a_gdn_log[li], b_gdn_dt[li], g_gdn_out[li], w_gdn_out[li])
            op, s_p, cv_p = gdn_mixer(hp, *wts, jnp.zeros((bp, GDN_HEADS, GDN_DK, GDN_DV), jnp.float32),
                                      jnp.zeros((bp, GDN_CONV - 1, GDN_CONV_DIM), hp.dtype))
            os_, s_s, cv_s = gdn_mixer(hs, *wts, state_gdn_S[li], state_gdn_conv[li])
            gdn_p.append((s_p, cv_p))
            gdn_s.append((s_s, cv_s))
        xp = xp + mod_p[:, 1, 2][:, None] * op
        xs = xs + mod_s[:, 1, 2][:, None] * os_
        xp = ffn_half_prompt(xp, g_norm[i, 2], mod_p, 2, w_up_bf[i, 1], w_dn_bf[i, 1])
        xs = ffn_half_sample(xs, g_norm[i, 2], mod_s, 2, w_up_bf[i, 1], w_dn_bf[i, 1])
    return (xp, xs,
            _stack(ml_p, 0), _stack(ml_s, 0), _stack(ml_p, 1), _stack(ml_s, 1), _stack(ml_p, 2), _stack(ml_s, 2),
            _stack(mla_p, 0), _stack(mla_s, 0),
            _stack(nsa_p, 0), _stack(nsa_s, 0), _stack(nsa_p, 1), _stack(nsa_s, 1),
            _stack(gdn_p, 0), _stack(gdn_s, 0), _stack(gdn_p, 1), _stack(gdn_s, 1))
```

```python
import functools

import jax
import jax.numpy as jnp
from jax import lax
from jax.experimental import pallas as pl
from jax.experimental.pallas import tpu as pltpu

D_MODEL = 1024
BATCH = 16
SEQ = 2048
DEPTH = 4
DEC_BATCH = 128
DEC_SEQ = 4
PAST_LEN = 8192
PAGE_SIZE = 128

N_MIXERS = 4
D_FF = 2816
RMS_EPS = 1e-6
NEG = -1e30
Q_BLOCK = 128

ML_HEADS = 8
ML_DQK = 64
ML_DV = 128
ML_CHUNK = 64
ML_GATE_CAP = 15.0

MLA_HEADS = 16
MLA_NOPE = 64
MLA_ROPE = 32
MLA_VD = 64
MLA_KV_LORA = 256
MLA_Q_LORA = 384
MLA_LAT = MLA_KV_LORA + MLA_ROPE
MLA_SCALE = (MLA_NOPE + MLA_ROPE) ** -0.5
ROPE_THETA = 10000.0

NSA_HEADS = 16
NSA_GROUPS = 2
NSA_HPG = NSA_HEADS // NSA_GROUPS
NSA_HD = 64
NSA_BLOCK = 64
NSA_TOPK = 15
NSA_WINDOW = 512
NSA_QB = 32
NSA_SCALE = NSA_HD ** -0.5

GDN_HEADS = 8
GDN_DK = 128
GDN_DV = 128
GDN_CONV = 4
GDN_CHUNK = 64
GDN_CONV_DIM = GDN_HEADS * (2 * GDN_DK + GDN_DV)

VMEM_LIMIT_BYTES = 56 * 1024 * 1024
FFN_ROW_TILE = 512
FFN_COL_CHUNK = 256
GDN_HEADS_PER_STEP = 8
GDN_INV_BLOCK = 4
LANES = 128
FLASH_TK = 256
NSA_TQ = 128
NSA_SEL_PAGES = 8
MLA_TQ = 512
MLA_PAGES_PER_STEP = 8
MLA_NEW_ROWS = 256
_HI = lax.Precision.HIGHEST


def _ffn_body(x_ref, shift_ref, scale_ref, gate_ref, g_ref, wup_ref, wdn_ref, o_ref, acc_ref):
    x = x_ref[...]
    y = x * lax.rsqrt(jnp.mean(x * x, axis=-1, keepdims=True) + RMS_EPS) * g_ref[...]
    h = (y * (1.0 + scale_ref[...]) + shift_ref[...]).astype(jnp.bfloat16)
    for j in range(D_FF // FFN_COL_CHUNK):
        lo = j * FFN_COL_CHUNK
        a = jnp.dot(h, wup_ref[:, lo:lo + FFN_COL_CHUNK], preferred_element_type=jnp.float32)
        b = jnp.dot(h, wup_ref[:, D_FF + lo:D_FF + lo + FFN_COL_CHUNK], preferred_element_type=jnp.float32)
        act = (a * jax.nn.sigmoid(a) * b).astype(jnp.bfloat16)
        part = jnp.dot(act, wdn_ref[lo:lo + FFN_COL_CHUNK, :], preferred_element_type=jnp.float32)
        if j == 0:
            acc_ref[...] = part
        else:
            acc_ref[...] += part
    o_ref[...] = x + 0.5 * gate_ref[...] * acc_ref[...]


def _ffn_call(x2d, shift, scale, gate, g, w_up, w_down, rows_per_mod):
    n = x2d.shape[0]
    tm = min(FFN_ROW_TILE, n)
    assert n % tm == 0
    if rows_per_mod:
        assert rows_per_mod % tm == 0
        tiles_per_mod = rows_per_mod // tm
        mod_spec = pl.BlockSpec((None, 1, D_MODEL), lambda i: (i // tiles_per_mod, 0, 0))
    else:
        mod_spec = pl.BlockSpec((tm, D_MODEL), lambda i: (i, 0))
    resident = functools.partial(pl.BlockSpec, pipeline_mode=pl.Buffered(1))
    return pl.pallas_call(
        _ffn_body,
        out_shape=jax.ShapeDtypeStruct((n, D_MODEL), jnp.float32),
        grid=(n // tm,),
        in_specs=[
            pl.BlockSpec((tm, D_MODEL), lambda i: (i, 0)),
            mod_spec, mod_spec, mod_spec,
            resident((1, D_MODEL), lambda i: (0, 0)),
            resident((D_MODEL, 2 * D_FF), lambda i: (0, 0)),
            resident((D_FF, D_MODEL), lambda i: (0, 0)),
        ],
        out_specs=pl.BlockSpec((tm, D_MODEL), lambda i: (i, 0)),
        scratch_shapes=[pltpu.VMEM((tm, D_MODEL), jnp.float32)],
        compiler_params=pltpu.CompilerParams(dimension_semantics=("arbitrary",),
                                             vmem_limit_bytes=VMEM_LIMIT_BYTES),
        name="ffn_half",
    )(x2d, shift, scale, gate, g.reshape(1, D_MODEL), w_up, w_down)


def ffn_half_prompt(x, g, mod, j, w_up, w_down):
    b, t, d = x.shape
    m = [mod[:, j, k][:, None, :] for k in range(3)]
    return _ffn_call(x.reshape(b * t, d), m[0], m[1], m[2], g, w_up, w_down, t).reshape(b, t, d)


def ffn_half_sample(x, g, mod, j, w_up, w_down):
    b, t, d = x.shape
    m = [jnp.repeat(mod[:, j, k], t, axis=0) for k in range(3)]
    return _ffn_call(x.reshape(b * t, d), m[0], m[1], m[2], g, w_up, w_down, 0).reshape(b, t, d)


def rmsnorm(x, g):
    xf = x.astype(jnp.float32)
    y = xf * lax.rsqrt(jnp.mean(xf * xf, axis=-1, keepdims=True) + RMS_EPS)
    return (y * g.astype(jnp.float32)).astype(x.dtype)


def l2norm(x):
    xf = x.astype(jnp.float32)
    return xf * lax.rsqrt(jnp.sum(xf * xf, axis=-1, keepdims=True) + 1e-6)


def rope(x, pos):
    half = x.shape[-1] // 2
    freqs = ROPE_THETA ** (-jnp.arange(half, dtype=jnp.float32) / half)
    ang = pos[:, None] * freqs[None, :]
    ang = ang.reshape((ang.shape[0],) + (1,) * (x.ndim - 3) + (half,))
    cos, sin = jnp.cos(ang), jnp.sin(ang)
    xf = x.astype(jnp.float32)
    x1, x2 = xf[..., :half], xf[..., half:]
    return jnp.concatenate([x1 * cos - x2 * sin, x2 * cos + x1 * sin], axis=-1).astype(x.dtype)


def masked_softmax(s, mask):
    return jax.nn.softmax(jnp.where(mask, s.astype(jnp.float32), NEG), axis=-1) * mask


def adaln(c, w, b):
    return (jax.nn.silu(c) @ w + b).reshape(c.shape[0], 3, 3, c.shape[1])


def modulate(x, g, mod, j):
    return rmsnorm(x, g) * (1.0 + mod[:, j, 1][:, None]) + mod[:, j, 0][:, None]


def _chunks(a, chunk):
    b, h, t = a.shape[:3]
    return jnp.moveaxis(a.reshape((b, h, t // chunk, chunk) + a.shape[3:]), 2, 0)


def _unchunk(a):
    a = jnp.moveaxis(a, 0, 2)
    return a.reshape(a.shape[:2] + (a.shape[2] * a.shape[3],) + a.shape[4:])


def _stack(entries, j):
    return jnp.stack([e[j] for e in entries])


def _mlstm_body(q_ref, k_ref, v_ref, icol_ref, irow_ref, fcol_ref, frow_ref, ghn_ref, c0_ref, m0_ref,
                o_ref, c_out_ref, m_out_ref, c_scr, m_scr):
    c = pl.program_id(1)
    L = ML_CHUNK

    @pl.when(c == 0)
    def _():
        c_scr[...] = c0_ref[...]
        m_scr[...] = m0_ref[...]

    row = lax.broadcasted_iota(jnp.int32, (L, L), 0)
    col = lax.broadcasted_iota(jnp.int32, (L, L), 1)
    causal = col <= row
    b_cols = jnp.dot(causal.astype(jnp.float32), fcol_ref[...], precision=_HI, preferred_element_type=jnp.float32)
    b_rows = jnp.dot(frow_ref[...], (col >= row).astype(jnp.float32), precision=_HI,
                     preferred_element_type=jnp.float32)

    def per_head(f):
        return jnp.stack([f(h) for h in range(ML_HEADS)], axis=0)

    q = per_head(lambda h: q_ref[:, h * ML_DQK:(h + 1) * ML_DQK]) * ML_DQK ** -0.5
    k = per_head(lambda h: k_ref[:, h * ML_DQK:(h + 1) * ML_DQK])
    one_col = (lax.broadcasted_iota(jnp.int32, (L, LANES), 1) == 0).astype(jnp.float32)
    v1 = per_head(lambda h: jnp.concatenate([v_ref[:, h * ML_DV:(h + 1) * ML_DV], one_col], axis=1)).astype(jnp.bfloat16)
    bc = per_head(lambda h: b_cols[:, h:h + 1])
    br = per_head(lambda h: b_rows[h:h + 1, :])
    ic = per_head(lambda h: icol_ref[:, h:h + 1])
    ir = per_head(lambda h: irow_ref[h:h + 1, :])
    m_old = m_scr[:, 0:1, 0:1]
    dmat = jnp.where(causal[None], bc - br + ir, NEG)
    inter = bc + m_old
    mt = jnp.maximum(inter, jnp.max(dmat, axis=-1, keepdims=True))
    w_inter = jnp.exp(inter - mt)
    qb, kb = q.astype(jnp.bfloat16), k.astype(jnp.bfloat16)
    s = _bdot_nt(qb, kb) * jnp.exp(dmat - mt)
    cx = c_scr[...]
    nd = w_inter * _bdot(qb, cx.astype(jnp.bfloat16)) + _bdot(s.astype(jnp.bfloat16), v1)
    hc = nd[:, :, :ML_DV] / jnp.maximum(jnp.abs(nd[:, :, ML_DV:ML_DV + 1]), jnp.exp(-mt))
    hn = hc * lax.rsqrt(jnp.mean(hc * hc, axis=-1, keepdims=True) + RMS_EPS)
    for h in range(ML_HEADS):
        o_ref[:, h * ML_DV:(h + 1) * ML_DV] = hn[h] * ghn_ref[h:h + 1, :]
    b_last = bc[:, L - 1:L, :]
    m_new = mt[:, L - 1:L, :]
    kw = (jnp.exp(b_last - bc + ic - m_new) * k).astype(jnp.bfloat16)
    c_scr[...] = jnp.exp(b_last + m_old - m_new) * cx + _bdot_tn(kw, v1)
    m_scr[...] = jnp.broadcast_to(m_new, m_scr.shape)

    @pl.when(c == pl.num_programs(1) - 1)
    def _():
        c_out_ref[...] = c_scr[...]
        m_out_ref[...] = m_scr[...]


def mlstm_scan_call(zp, ig, lf, g_hn, c0, n0, m0):
    b, t, _ = zp.shape
    L = ML_CHUNK
    nc = t // L
    qw = ML_HEADS * ML_DQK
    vw = ML_HEADS * ML_DV

    def cols(z):
        return z.reshape(b, nc, L, ML_HEADS)

    def rows(z):
        return z.reshape(b, nc, L, ML_HEADS).swapaxes(2, 3)

    cx0 = jnp.concatenate([c0, n0[..., None], jnp.zeros(c0.shape[:-1] + (LANES - 1,), jnp.float32)], axis=-1)
    mx0 = jnp.broadcast_to(m0[:, :, None, None], (b, ML_HEADS, 8, LANES))
    col_spec = pl.BlockSpec((None, None, L, ML_HEADS), lambda i, c: (i, c, 0, 0))
    row_spec = pl.BlockSpec((None, None, ML_HEADS, L), lambda i, c: (i, c, 0, 0))
    c_spec = pl.BlockSpec((None, ML_HEADS, ML_DQK, ML_DV + LANES), lambda i, c: (i, 0, 0, 0))
    m_spec = pl.BlockSpec((None, ML_HEADS, 8, LANES), lambda i, c: (i, 0, 0, 0))
    o, cx, mx = pl.pallas_call(
        _mlstm_body,
        out_shape=(jax.ShapeDtypeStruct((b, t, vw), jnp.float32),
                   jax.ShapeDtypeStruct((b, ML_HEADS, ML_DQK, ML_DV + LANES), jnp.float32),
                   jax.ShapeDtypeStruct((b, ML_HEADS, 8, LANES), jnp.float32)),
        grid=(b, nc),
        in_specs=[pl.BlockSpec((None, L, qw), lambda i, c: (i, c, 0)),
                  pl.BlockSpec((None, L, qw), lambda i, c: (i, c, 1)),
                  pl.BlockSpec((None, L, vw), lambda i, c: (i, c, 1)),
                  col_spec, row_spec, col_spec, row_spec,
                  pl.BlockSpec((ML_HEADS, ML_DV), lambda i, c: (0, 0)), c_spec, m_spec],
        out_specs=(pl.BlockSpec((None, L, vw), lambda i, c: (i, c, 0)), c_spec, m_spec),
        scratch_shapes=[pltpu.VMEM((ML_HEADS, ML_DQK, ML_DV + LANES), jnp.float32),
                        pltpu.VMEM((ML_HEADS, 8, LANES), jnp.float32)],
        compiler_params=pltpu.CompilerParams(dimension_semantics=("arbitrary", "arbitrary"),
                                             vmem_limit_bytes=VMEM_LIMIT_BYTES),
        name="mlstm_scan",
    )(zp, zp, zp, cols(ig), rows(ig), cols(lf), rows(lf), g_hn, cx0, mx0)
    return o, cx[..., :ML_DV], cx[..., ML_DV], mx[:, :, 0, 0]


def mlstm_mixer(h, w_in, b_gate, g_hn, w_out, C0, n0, m0):
    b, t, _ = h.shape
    qk_w, v_w = ML_HEADS * ML_DQK, ML_HEADS * ML_DV
    assert 2 * qk_w == v_w
    zp = h @ w_in
    o = zp[..., 2 * qk_w + v_w:2 * qk_w + 2 * v_w]
    gates = ML_GATE_CAP * jnp.tanh((zp[..., 2 * qk_w + 2 * v_w:] + b_gate).astype(jnp.float32) / ML_GATE_CAP)
    ig = gates[..., :ML_HEADS]
    lf = jax.nn.log_sigmoid(gates[..., ML_HEADS:])
    pad = (-t) % ML_CHUNK
    if pad:
        zp = jnp.pad(zp, ((0, 0), (0, pad), (0, 0)))
        ig = jnp.pad(ig, ((0, 0), (0, pad), (0, 0)), constant_values=NEG)
        lf = jnp.pad(lf, ((0, 0), (0, pad), (0, 0)))
    hh, C, n, m = mlstm_scan_call(zp, ig, lf, g_hn, C0.astype(jnp.float32), n0.astype(jnp.float32),
                                  m0.astype(jnp.float32))
    return (jax.nn.sigmoid(o) * hh[:, :t]) @ w_out, C.astype(h.dtype), n.astype(h.dtype), m.astype(h.dtype)


def _dot_nt(a, b, **kw):
    return lax.dot_general(a, b, (((1,), (1,)), ((), ())), preferred_element_type=jnp.float32, **kw)


def _dot_tn(a, b, **kw):
    return lax.dot_general(a, b, (((0,), (0,)), ((), ())), preferred_element_type=jnp.float32, **kw)


def _bdot(a, b, **kw):
    return lax.dot_general(a, b, (((2,), (1,)), ((0,), (0,))), preferred_element_type=jnp.float32, **kw)


def _bdot_nt(a, b, **kw):
    return lax.dot_general(a, b, (((2,), (2,)), ((0,), (0,))), preferred_element_type=jnp.float32, **kw)


def _bdot_tn(a, b, **kw):
    return lax.dot_general(a, b, (((1,), (1,)), ((0,), (0,))), preferred_element_type=jnp.float32, **kw)


def _flash_init(m_scr, acc_scr):
    m_scr[...] = jnp.full(m_scr.shape, NEG, jnp.float32)
    acc_scr[...] = jnp.zeros(acc_scr.shape, jnp.float32)


def _flash_pass(q_all, k_ref, v_ref, kt_lo, kt_hi, mask_fn, heads, m_scr, acc_scr):
    rows = q_all.shape[0]
    tq = rows // heads

    def body(kt, carry):
        k0 = pl.multiple_of(kt * FLASH_TK, FLASH_TK)
        s = _dot_nt(q_all, k_ref[pl.ds(k0, FLASH_TK), :])
        if mask_fn is not None:
            s = jnp.where(mask_fn(k0)[None], s.reshape(heads, tq, FLASH_TK), 2.0 * NEG).reshape(rows, FLASH_TK)
        m_old = m_scr[...]
        m_new = jnp.maximum(m_old, jnp.max(s, axis=-1, keepdims=True))
        p = jnp.exp(s - jnp.concatenate([m_new] * (FLASH_TK // LANES), axis=1))
        acc_scr[...] = acc_scr[...] * jnp.exp(m_old - m_new) + jnp.dot(
            p.astype(jnp.bfloat16), v_ref[pl.ds(k0, FLASH_TK), :], preferred_element_type=jnp.float32)
        m_scr[...] = m_new
        return carry

    lax.fori_loop(kt_lo, kt_hi, body, 0)


def _flash_result(acc_scr, dv):
    acc = acc_scr[...]
    return acc[:, :dv] / acc[:, dv:dv + 1]


def _gdn_body(q_ref, k_ref, v_ref, beta_ref, gcol_ref, grow_ref, s0_ref, o_ref, s_out_ref, s_scr):
    c = pl.program_id(2)
    L = GDN_CHUNK

    @pl.when(c == 0)
    def _():
        s_scr[...] = s0_ref[...]

    row = lax.broadcasted_iota(jnp.int32, (L, L), 0)
    col = lax.broadcasted_iota(jnp.int32, (L, L), 1)
    causal = col <= row
    strict = col < row
    eye = (col == row).astype(jnp.float32)
    b_cols = jnp.dot(causal.astype(jnp.float32), gcol_ref[...], precision=_HI, preferred_element_type=jnp.float32)
    b_rows = jnp.dot(grow_ref[...], (col >= row).astype(jnp.float32), precision=_HI,
                     preferred_element_type=jnp.float32)
    def per_head(f):
        return jnp.stack([f(h, slice(h * GDN_DK, (h + 1) * GDN_DK)) for h in range(GDN_HEADS_PER_STEP)], axis=0)

    qf = per_head(lambda h, lanes: q_ref[:, lanes])
    kf = per_head(lambda h, lanes: k_ref[:, lanes])
    v = per_head(lambda h, lanes: v_ref[:, lanes])
    beta = per_head(lambda h, lanes: beta_ref[:, h:h + 1])
    bc = per_head(lambda h, lanes: b_cols[:, h:h + 1])
    br = per_head(lambda h, lanes: b_rows[h:h + 1, :])
    q = qf * lax.rsqrt(jnp.sum(qf * qf, axis=-1, keepdims=True) + 1e-6) * GDN_DK ** -0.5
    k = kf * lax.rsqrt(jnp.sum(kf * kf, axis=-1, keepdims=True) + 1e-6)
    decay = jnp.exp(jnp.where(causal[None], bc - br, NEG))
    qb, kb = q.astype(jnp.bfloat16), k.astype(jnp.bfloat16)
    a_mat = jnp.where(strict[None], beta * _bdot_nt(kb, kb) * decay, 0.0)
    blk = GDN_INV_BLOCK
    a_diag = jnp.where((row // blk == col // blk)[None], a_mat, 0.0)
    a_diag2 = _bdot(a_diag, a_diag, precision=_HI)
    inv = eye[None] - a_diag + a_diag2 - _bdot(a_diag, a_diag2, precision=_HI)
    while blk < L:
        join = (row // (2 * blk) == col // (2 * blk)) & (row // blk == col // blk + 1)
        lower = jnp.where(join[None], a_mat, 0.0)
        inv = inv - _bdot(_bdot(inv, lower, precision=_HI), inv, precision=_HI)
        blk *= 2
    s = s_scr[...]
    sb = s.astype(jnp.bfloat16)
    eb = jnp.exp(bc)
    rhs = beta * (v - eb * _bdot(kb, sb))
    ub = _bdot(inv, rhs, precision=_HI).astype(jnp.bfloat16)
    qk = (_bdot_nt(qb, kb) * decay).astype(jnp.bfloat16)
    o = eb * _bdot(qb, sb) + _bdot(qk, ub)
    for h in range(GDN_HEADS_PER_STEP):
        o_ref[:, h * GDN_DV:(h + 1) * GDN_DV] = o[h]
    b_last = bc[:, L - 1:L, :]
    kw = (jnp.exp(b_last - bc) * k).astype(jnp.bfloat16)
    s_scr[...] = jnp.exp(b_last) * s + _bdot_tn(kw, ub)

    @pl.when(c == pl.num_programs(2) - 1)
    def _():
        s_out_ref[...] = s_scr[...]


def gdn_scan_call(qkv, beta, g, s0):
    b, t, _ = qkv.shape
    hb = GDN_HEADS_PER_STEP
    ng = GDN_HEADS // hb
    L = GDN_CHUNK
    w = hb * GDN_DK

    def qkv_spec(part):
        return pl.BlockSpec((None, L, w), lambda i, j, c: (i, c, part * ng + j))

    def per_chunk(z):
        return z.reshape(b, t // L, L, ng, hb).transpose(0, 3, 1, 2, 4)

    col_spec = pl.BlockSpec((None, None, None, L, hb), lambda i, j, c: (i, j, c, 0, 0))
    state_spec = pl.BlockSpec((None, hb, GDN_DK, GDN_DV), lambda i, j, c: (i, j, 0, 0))
    return pl.pallas_call(
        _gdn_body,
        out_shape=(jax.ShapeDtypeStruct((b, t, GDN_HEADS * GDN_DV), jnp.float32),
                   jax.ShapeDtypeStruct((b, GDN_HEADS, GDN_DK, GDN_DV), jnp.float32)),
        grid=(b, ng, t // L),
        in_specs=[qkv_spec(0), qkv_spec(1), qkv_spec(2), col_spec, col_spec,
                  pl.BlockSpec((None, None, None, hb, L), lambda i, j, c: (i, j, c, 0, 0)),
                  state_spec],
        out_specs=(pl.BlockSpec((None, L, w), lambda i, j, c: (i, c, j)), state_spec),
        scratch_shapes=[pltpu.VMEM((hb, GDN_DK, GDN_DV), jnp.float32)],
        compiler_params=pltpu.CompilerParams(dimension_semantics=("arbitrary", "arbitrary", "arbitrary"),
                                             vmem_limit_bytes=VMEM_LIMIT_BYTES),
        name="gdn_scan",
    )(qkv, qkv, qkv, per_chunk(beta), per_chunk(g), per_chunk(g).swapaxes(3, 4), s0)


def gdn_mixer(h, w_in, w_conv, a_log, dt_bias, g_out, w_out, S0, conv0):
    b, t, _ = h.shape
    qkv, beta, a, gate = jnp.split(h @ w_in, [GDN_CONV_DIM, GDN_CONV_DIM + GDN_HEADS, GDN_CONV_DIM + 2 * GDN_HEADS], axis=-1)
    xc = jnp.concatenate([conv0.astype(qkv.dtype), qkv], axis=1)
    conv = lax.conv_general_dilated(xc, w_conv[:, None, :].astype(xc.dtype), (1,), 'VALID',
                                    dimension_numbers=('NWC', 'WIO', 'NWC'), feature_group_count=GDN_CONV_DIM)
    act = jax.nn.silu(conv)
    beta = jax.nn.sigmoid(beta.astype(jnp.float32))
    g = -jnp.exp(a_log.astype(jnp.float32)) * jax.nn.softplus((a + dt_bias).astype(jnp.float32))
    pad = (-t) % GDN_CHUNK
    if pad:
        act, beta, g = (jnp.pad(z, ((0, 0), (0, pad), (0, 0))) for z in (act, beta, g))
    o, S = gdn_scan_call(act, beta, g, S0.astype(jnp.float32))
    o = o[:, :t].reshape(b, t, GDN_HEADS, GDN_DV)
    o = rmsnorm(o, g_out).astype(h.dtype).reshape(b, t, GDN_HEADS * GDN_DV)
    return (o * jax.nn.silu(gate)) @ w_out, S.astype(h.dtype), xc[:, -(GDN_CONV - 1):]


def mla_project(h, pos, w_in, g_qa, g_kva, w_uq, g_qk):
    b, t, _ = h.shape
    cq, ckv, kr = jnp.split(h @ w_in, [MLA_Q_LORA, MLA_Q_LORA + MLA_KV_LORA], axis=-1)
    q = (rmsnorm(cq, g_qa) @ w_uq).reshape(b, t, MLA_HEADS, MLA_NOPE + MLA_ROPE)
    q_nope = rmsnorm(q[..., :MLA_NOPE], g_qk[0, :MLA_NOPE])
    q_rope = rope(rmsnorm(q[..., MLA_NOPE:], g_qk[0, MLA_NOPE:]), pos)
    k_rope = rope(rmsnorm(kr, g_qk[1, MLA_NOPE:]), pos)
    latent = jnp.concatenate([rmsnorm(ckv, g_kva), k_rope], axis=-1)
    return q_nope, q_rope, latent


def mla_scores(q_nope, q_rope, k_nope, k_rope):
    s = jnp.einsum('bthd,bshd->bhts', q_nope, k_nope) + jnp.einsum('bthr,bsr->bhts', q_rope, k_rope)
    return s.astype(jnp.float32) * MLA_SCALE


def mla_keys(latent, w_uk, g_kn):
    ckv = latent[..., :MLA_KV_LORA]
    return ckv, rmsnorm(jnp.einsum('bsc,chd->bshd', ckv, w_uk), g_kn), latent[..., MLA_KV_LORA:]


def _mla_prompt_body(q_ref, k_ref, v_ref, o_ref, m_scr, acc_scr):
    qi = pl.program_id(2)
    t0 = qi * MLA_TQ
    tpos = t0 + lax.broadcasted_iota(jnp.int32, (MLA_TQ, 1), 0)

    def causal(k0):
        return (k0 + lax.broadcasted_iota(jnp.int32, (1, FLASH_TK), 1)) <= tpos

    n_full = t0 // FLASH_TK
    outs = []
    for h in range(2):
        _flash_init(m_scr, acc_scr)
        _flash_pass(q_ref[h], k_ref.at[h], v_ref.at[h], 0, n_full, None, 1, m_scr, acc_scr)
        _flash_pass(q_ref[h], k_ref.at[h], v_ref.at[h], n_full, n_full + MLA_TQ // FLASH_TK, causal, 1,
                    m_scr, acc_scr)
        outs.append(_flash_result(acc_scr, MLA_VD))
    o_ref[...] = jnp.concatenate(outs, axis=1)


def mla_prompt_attn(q, k, v):
    b, h, t, _ = q.shape
    assert t % MLA_TQ == 0 and MLA_TQ % FLASH_TK == 0 and h % 2 == 0
    seq_spec = pl.BlockSpec((None, 2, t, LANES), lambda i, j, k: (i, j, 0, 0))
    return pl.pallas_call(
        _mla_prompt_body,
        out_shape=jax.ShapeDtypeStruct((b, t, h * MLA_VD), jnp.float32),
        grid=(b, h // 2, t // MLA_TQ),
        in_specs=[pl.BlockSpec((None, 2, MLA_TQ, LANES), lambda i, j, k: (i, j, k, 0)), seq_spec, seq_spec],
        out_specs=pl.BlockSpec((None, MLA_TQ, 2 * MLA_VD), lambda i, j, k: (i, k, j)),
        scratch_shapes=[pltpu.VMEM((MLA_TQ, LANES), jnp.float32), pltpu.VMEM((MLA_TQ, LANES), jnp.float32)],
        compiler_params=pltpu.CompilerParams(dimension_semantics=("arbitrary", "arbitrary", "arbitrary"),
                                             vmem_limit_bytes=VMEM_LIMIT_BYTES),
        name="mla_prompt_attn",
    )(q, k, v)


def mla_prompt(h, w_in, g_qa, g_kva, w_uq, w_uk, w_uv, g_qk, w_out):
    b, t, _ = h.shape
    q_nope, q_rope, latent = mla_project(h, jnp.arange(t, dtype=jnp.float32), w_in, g_qa, g_kva, w_uq, g_qk)
    ckv, k_nope, k_rope = mla_keys(latent, w_uk, g_qk[1, :MLA_NOPE])
    v = jnp.einsum('bsc,chd->bshd', ckv, w_uv)

    def heads_first(x):
        return x.transpose(0, 2, 1, 3)

    pad = jnp.zeros((b, MLA_HEADS, t, LANES - MLA_NOPE - MLA_ROPE), jnp.float32)
    q = jnp.concatenate([heads_first(q_nope), heads_first(q_rope), pad], axis=-1) * MLA_SCALE
    k = jnp.concatenate([heads_first(k_nope), jnp.broadcast_to(k_rope[:, None], (b, MLA_HEADS, t, MLA_ROPE)), pad],
                        axis=-1)
    v1 = jnp.concatenate([heads_first(v), jnp.ones((b, MLA_HEADS, t, 1), jnp.float32),
                          jnp.zeros((b, MLA_HEADS, t, LANES - MLA_VD - 1), jnp.float32)], axis=-1)
    o = mla_prompt_attn(q.astype(jnp.bfloat16), k.astype(jnp.bfloat16), v1.astype(jnp.bfloat16))
    return o @ w_out, latent


def _mla_page_scores(lat, w_uk_ref, qbd, gexp, qr):
    ckv = lat[:, :MLA_KV_LORA].astype(jnp.bfloat16)
    kn = jnp.dot(ckv, w_uk_ref[...], preferred_element_type=jnp.float32)
    s_n = _dot_nt(kn.astype(jnp.bfloat16), qbd)
    ms = jnp.dot((kn * kn).astype(jnp.bfloat16), gexp, preferred_element_type=jnp.float32)
    s_r = _dot_nt(lat[:, MLA_KV_LORA:].astype(jnp.bfloat16), qr)
    return ckv, (s_n * lax.rsqrt(ms + RMS_EPS) + s_r) * MLA_SCALE


def _mla_page_rows(page_ref):
    x = page_ref[...]
    tail = x[MLA_LAT - LANES:, :].T
    return jnp.concatenate([x[:LANES, :].T, x[LANES:2 * LANES, :].T, tail[:, LANES - MLA_ROPE:]], axis=1)


def _mla_sample_body(pt_ref, *refs):
    pp = MLA_PAGES_PER_STEP
    page_refs = refs[:pp]
    qn_ref, qr_ref, latn_ref, w_uk_ref, w_uv_ref, o_ref, ckv_scr, s_scr, qbd_scr = refs[pp:]
    j = pl.program_id(1)
    nj = pl.num_programs(1)
    nq = DEC_SEQ * MLA_HEADS
    hw = MLA_HEADS * MLA_NOPE
    n = pp * PAGE_SIZE
    n_past = nj * n

    @pl.when(j == 0)
    def _():
        r = lax.broadcasted_iota(jnp.int32, (nq, hw), 0)
        c = lax.broadcasted_iota(jnp.int32, (nq, hw), 1)
        q_rep = jnp.concatenate([qn_ref[...]] * MLA_HEADS, axis=1)
        qbd_scr[...] = jnp.where(r % MLA_HEADS == c // MLA_NOPE, q_rep, 0.0).astype(jnp.bfloat16)

    gr = lax.broadcasted_iota(jnp.int32, (hw, nq), 0)
    gc = lax.broadcasted_iota(jnp.int32, (hw, nq), 1)
    gexp = jnp.where(gr // MLA_NOPE == gc % MLA_HEADS, 1.0 / MLA_NOPE, 0.0).astype(jnp.bfloat16)
    qr = qr_ref[...].astype(jnp.bfloat16)
    lat = jnp.concatenate([_mla_page_rows(p) for p in page_refs], axis=0)
    ckv, s = _mla_page_scores(lat, w_uk_ref, qbd_scr[...], gexp, qr)
    row0 = pl.multiple_of(j * n, n)
    ones_col = (lax.broadcasted_iota(jnp.int32, (1, LANES), 1) == 0).astype(jnp.bfloat16)
    ckv_scr[pl.ds(row0, n), :MLA_KV_LORA] = ckv
    ckv_scr[pl.ds(row0, n), MLA_KV_LORA:] = jnp.broadcast_to(ones_col, (n, LANES))
    s_scr[pl.ds(row0, n), :] = s

    @pl.when(j == nj - 1)
    def _():
        ckv_n, s_n = _mla_page_scores(latn_ref[...], w_uk_ref, qbd_scr[...], gexp, qr)
        kr = lax.broadcasted_iota(jnp.int32, (MLA_NEW_ROWS, nq), 0)
        kc = lax.broadcasted_iota(jnp.int32, (MLA_NEW_ROWS, nq), 1)
        s_scr[n_past:, :] = jnp.where(kr <= kc // MLA_HEADS, s_n, NEG)
        ckv_scr[n_past:, :MLA_KV_LORA] = ckv_n
        ckv_scr[n_past:, MLA_KV_LORA:] = jnp.broadcast_to(ones_col, (MLA_NEW_ROWS, LANES))
        sc = s_scr[...]
        p = jnp.exp(sc - jnp.max(sc, axis=0, keepdims=True)).astype(jnp.bfloat16)
        acc = _dot_tn(p, ckv_scr[...])
        o_lat = (acc[:, :MLA_KV_LORA] / acc[:, MLA_KV_LORA:MLA_KV_LORA + 1]).astype(jnp.bfloat16)
        z = jnp.dot(o_lat, w_uv_ref[...], preferred_element_type=jnp.float32)
        zr = lax.broadcasted_iota(jnp.int32, (MLA_HEADS, MLA_HEADS * MLA_VD), 0)
        zc = lax.broadcasted_iota(jnp.int32, (MLA_HEADS, MLA_HEADS * MLA_VD), 1)
        keep = zr == zc // MLA_VD
        o_ref[...] = jnp.concatenate(
            [jnp.sum(jnp.where(keep, z[t * MLA_HEADS:(t + 1) * MLA_HEADS], 0.0), axis=0, keepdims=True)
             for t in range(DEC_SEQ)], axis=0)


def mla_sample_attn(pool, li, page_table, qn, qr, latent, w_uk, w_uv):
    b, n_pages = page_table.shape
    pp = MLA_PAGES_PER_STEP
    assert n_pages % pp == 0 and latent.shape[1] == DEC_SEQ
    nq = DEC_SEQ * MLA_HEADS
    n_rows = n_pages * PAGE_SIZE + MLA_NEW_ROWS
    latn = jnp.pad(latent, ((0, 0), (0, MLA_NEW_ROWS - DEC_SEQ), (0, 0)))

    def page_spec(k):
        return pl.BlockSpec((None, None, MLA_LAT, PAGE_SIZE), lambda i, j, pt: (li, pt[i, j * pp + k], 0, 0))

    def whole(r, c):
        return pl.BlockSpec((r, c), lambda i, j, pt: (0, 0))

    def per_seq(r, c):
        return pl.BlockSpec((None, r, c), lambda i, j, pt: (i, 0, 0))

    grid_spec = pltpu.PrefetchScalarGridSpec(
        num_scalar_prefetch=1,
        grid=(b, n_pages // pp),
        in_specs=[page_spec(k) for k in range(pp)] + [
            per_seq(nq, MLA_NOPE), per_seq(nq, MLA_ROPE), per_seq(MLA_NEW_ROWS, MLA_LAT),
            whole(MLA_KV_LORA, MLA_HEADS * MLA_NOPE), whole(MLA_KV_LORA, MLA_HEADS * MLA_VD)],
        out_specs=per_seq(DEC_SEQ, MLA_HEADS * MLA_VD),
        scratch_shapes=[pltpu.VMEM((n_rows, MLA_KV_LORA + LANES), jnp.bfloat16),
                        pltpu.VMEM((n_rows, nq), jnp.float32),
                        pltpu.VMEM((nq, MLA_HEADS * MLA_NOPE), jnp.bfloat16)])
    return pl.pallas_call(
        _mla_sample_body,
        out_shape=jax.ShapeDtypeStruct((b, DEC_SEQ, MLA_HEADS * MLA_VD), jnp.float32),
        grid_spec=grid_spec,
        compiler_params=pltpu.CompilerParams(dimension_semantics=("arbitrary", "arbitrary"),
                                             vmem_limit_bytes=VMEM_LIMIT_BYTES),
        name="mla_sample_attn",
    )(page_table, *([pool] * pp), qn, qr, latn, w_uk, w_uv)


def mla_sample(h, pool, li, page_table, w_in, g_qa, g_kva, w_uq, w_uk, w_uv, g_qk, w_out):
    b, t, _ = h.shape
    q_nope, q_rope, latent = mla_project(h, PAST_LEN + jnp.arange(t, dtype=jnp.float32), w_in, g_qa, g_kva, w_uq, g_qk)
    qn = (q_nope * g_qk[1, :MLA_NOPE]).reshape(b, t * MLA_HEADS, MLA_NOPE)
    qr = q_rope.reshape(b, t * MLA_HEADS, MLA_ROPE)
    o = mla_sample_attn(pool.transpose(0, 1, 3, 2), li, page_table, qn, qr, latent,
                        w_uk.reshape(MLA_KV_LORA, -1).astype(jnp.bfloat16),
                        w_uv.reshape(MLA_KV_LORA, -1).astype(jnp.bfloat16))
    return o @ w_out, latent


def nsa_project(h, w_in, b_gate, g_qk):
    b, t, _ = h.shape
    qw, kvw = NSA_HEADS * NSA_HD, 6 * NSA_GROUPS * NSA_HD
    q, kv, gates = jnp.split(h @ w_in, [qw, qw + kvw], axis=-1)
    q = rmsnorm(q.reshape(b, t, NSA_GROUPS, NSA_HPG, NSA_HD), g_qk[0])
    kv = kv.reshape(b, t, 6, NSA_GROUPS, NSA_HD)
    rows = jnp.stack([kv[:, :, 0], kv[:, :, 1], rmsnorm(kv[:, :, 2], g_qk[1]), kv[:, :, 3]], axis=2)
    win = jnp.stack([rmsnorm(kv[:, :, 4], g_qk[1]), kv[:, :, 5]], axis=2)
    gates = jax.nn.sigmoid(gates + b_gate).reshape(b, t, NSA_GROUPS, NSA_HPG, 3)
    return q, rows, win, gates


def nsa_compress(kv_rows, w_cmp, g_k):
    b, l = kv_rows.shape[:2]
    blk = kv_rows.reshape(b, l // NSA_BLOCK, NSA_BLOCK, 2, NSA_GROUPS, NSA_HD)
    c = jnp.einsum('bnpcgd,cpde->bncge', blk, w_cmp)
    return rmsnorm(c[:, :, 0], g_k), c[:, :, 1]


def nsa_cmp_attn(q, k_c, v_c, qpos):
    nb = k_c.shape[1]
    s = jnp.einsum('btghd,bngd->bghtn', q, k_c) * NSA_SCALE
    mask = ((jnp.arange(nb) + 1) * NSA_BLOCK - 1)[None, :] <= qpos[:, None]
    p = masked_softmax(s, mask)
    o = jnp.einsum('bghtn,bngd->btghd', p.astype(v_c.dtype), v_c)
    return o, jnp.sum(p, axis=2)


def nsa_select(imp, qpos):
    nb = imp.shape[-1]
    cand = jnp.arange(nb)[None, :] < (qpos // NSA_BLOCK)[:, None]
    val, idx = lax.top_k(jnp.where(cand, imp, -jnp.inf), min(NSA_TOPK, nb))
    return idx, jnp.isfinite(val)


def nsa_sel_prompt(q, k_s, v_s, idx, valid):
    b, t = q.shape[:2]
    nb = t // NSA_BLOCK
    kb = k_s.reshape(b, nb, NSA_BLOCK, NSA_GROUPS, NSA_HD).transpose(0, 3, 1, 2, 4)
    vb = v_s.reshape(b, nb, NSA_BLOCK, NSA_GROUPS, NSA_HD).transpose(0, 3, 1, 2, 4)
    bi = jnp.arange(b)[:, None, None, None]
    gi = jnp.arange(NSA_GROUPS)[None, None, :, None]
    idx, valid = idx.transpose(0, 2, 1, 3), valid.transpose(0, 2, 1, 3)

    def block(i):
        t0 = i * NSA_QB
        qpos = t0 + jnp.arange(NSA_QB)
        qq = lax.dynamic_slice_in_dim(q, t0, NSA_QB, axis=1)
        cur = jnp.broadcast_to((qpos // NSA_BLOCK)[None, :, None, None], (b, NSA_QB, NSA_GROUPS, 1))
        ii = jnp.concatenate([lax.dynamic_slice_in_dim(idx, t0, NSA_QB, axis=1), cur], axis=-1)
        ok = jnp.concatenate([lax.dynamic_slice_in_dim(valid, t0, NSA_QB, axis=1), jnp.ones(cur.shape, bool)], axis=-1)
        kpos = ii[..., None] * NSA_BLOCK + jnp.arange(NSA_BLOCK)
        mask = ok[..., None] & (kpos <= qpos[None, :, None, None, None])
        n = ii.shape[-1] * NSA_BLOCK
        ks, vs = kb[bi, gi, ii], vb[bi, gi, ii]
        s = jnp.einsum('btghd,btgkpd->bghtkp', qq, ks).reshape(b, NSA_GROUPS, NSA_HPG, NSA_QB, n) * NSA_SCALE
        mask = mask.transpose(0, 2, 1, 3, 4).reshape(b, NSA_GROUPS, 1, NSA_QB, n)
        p = masked_softmax(s, mask)
        return jnp.einsum('bghtn,btgnd->btghd', p.astype(vs.dtype), vs.reshape(b, NSA_QB, NSA_GROUPS, n, NSA_HD))

    return jnp.moveaxis(lax.map(block, jnp.arange(t // NSA_QB)), 0, 1).reshape(q.shape)


def nsa_sel_sample(q, k_new, v_new, idx, valid, pool, li, page_table):
    b, t = q.shape[:2]
    bpp = PAGE_SIZE // NSA_BLOCK
    kk = idx.shape[-1]
    n = kk * NSA_BLOCK
    ii, ok = idx.transpose(0, 2, 1, 3), valid.transpose(0, 2, 1, 3)
    phys = page_table[jnp.arange(b)[:, None, None, None], ii // bpp][..., None]
    rows = (ii % bpp)[..., None] * NSA_BLOCK + jnp.arange(NSA_BLOCK)
    gi = jnp.arange(NSA_GROUPS)[None, None, :, None, None]
    ks = pool[li, phys, rows, 2, gi].reshape(b, t, NSA_GROUPS, n, NSA_HD)
    vs = pool[li, phys, rows, 3, gi].reshape(b, t, NSA_GROUPS, n, NSA_HD)
    s = jnp.concatenate([jnp.einsum('btghd,btgnd->bghtn', q, ks), jnp.einsum('btghd,bsgd->bghts', q, k_new)], axis=-1)
    mask_past = jnp.repeat(ok, NSA_BLOCK, axis=-1).transpose(0, 2, 1, 3)[:, :, None]
    mask_new = jnp.broadcast_to(jnp.tril(jnp.ones((t, t), bool)), (b, NSA_GROUPS, 1, t, t))
    p = masked_softmax(s * NSA_SCALE, jnp.concatenate([mask_past, mask_new], axis=-1)).astype(v_new.dtype)
    return (jnp.einsum('bghtn,btgnd->btghd', p[..., :n], vs)
            + jnp.einsum('bghts,bsgd->btghd', p[..., n:], v_new))


def nsa_win_prompt(q, k_w, v_w):
    t = q.shape[1]
    pad = ((0, 0), (NSA_WINDOW, 0), (0, 0), (0, 0))
    kp, vp = jnp.pad(k_w, pad), jnp.pad(v_w, pad)
    span = NSA_WINDOW + Q_BLOCK

    def block(i):
        t0 = i * Q_BLOCK
        qq = lax.dynamic_slice_in_dim(q, t0, Q_BLOCK, axis=1)
        kk = lax.dynamic_slice_in_dim(kp, t0, span, axis=1)
        vv = lax.dynamic_slice_in_dim(vp, t0, span, axis=1)
        qpos = t0 + jnp.arange(Q_BLOCK)
        kpos = t0 - NSA_WINDOW + jnp.arange(span)
        mask = ((kpos[None, :] <= qpos[:, None]) & (kpos[None, :] > qpos[:, None] - NSA_WINDOW)
                & (kpos[None, :] >= 0))
        p = masked_softmax(jnp.einsum('bqghd,bkgd->bghqk', qq, kk) * NSA_SCALE, mask)
        return jnp.einsum('bghqk,bkgd->bqghd', p.astype(vv.dtype), vv)

    return jnp.moveaxis(lax.map(block, jnp.arange(t // Q_BLOCK)), 0, 1).reshape(q.shape)


def nsa_win_sample(q, win, buf):
    t = q.shape[1]
    wb = buf.shape[1]
    kv = jnp.concatenate([buf.astype(win.dtype), win], axis=1)
    qpos = PAST_LEN + jnp.arange(t)
    kpos = PAST_LEN - wb + jnp.arange(wb + t)
    mask = (kpos[None, :] <= qpos[:, None]) & (kpos[None, :] > qpos[:, None] - NSA_WINDOW)
    p = masked_softmax(jnp.einsum('btghd,bsgd->bghts', q, kv[:, :, 0]) * NSA_SCALE, mask)
    o = jnp.einsum('bghts,bsgd->btghd', p.astype(kv.dtype), kv[:, :, 1])
    return o, kv[:, -wb:]


def nsa_merge(gates, o_cmp, o_sel, o_win):
    o = gates[..., 0:1] * o_cmp + gates[..., 1:2] * o_sel + gates[..., 2:3] * o_win
    return o.reshape(o.shape[0], o.shape[1], NSA_HEADS * NSA_HD)


def _nsa_body(q_ref, rows_ref, win_ref, kc_ref, vc_ref, gates_ref, o_ref, kv_scr, m_scr, acc_scr):
    qi = pl.program_id(1)
    nb = kc_ref.shape[0]
    gw = NSA_GROUPS * NSA_HD

    @pl.when(qi == 0)
    def _():
        lane = lax.broadcasted_iota(jnp.int32, (1, LANES), 1)
        one_col = (lane == NSA_HD).astype(jnp.float32)
        srcs = (rows_ref[:, 2 * gw:3 * gw], rows_ref[:, 3 * gw:4 * gw], win_ref[:, 0:gw], win_ref[:, gw:2 * gw])
        for i, x in enumerate(srcs):
            for g in range(NSA_GROUPS):
                xg = x if g == 0 else pltpu.roll(x, NSA_HD, 1)
                kv_scr[i * NSA_GROUPS + g] = jnp.where(lane < NSA_HD, xg, one_col if i % 2 else 0.0).astype(jnp.bfloat16)

    t0 = qi * NSA_TQ
    tpos = t0 + lax.broadcasted_iota(jnp.int32, (NSA_TQ, 1), 0)
    blk_t = tpos // NSA_BLOCK
    nidx = lax.broadcasted_iota(jnp.int32, (1, nb), 1)
    rows = NSA_HPG * NSA_TQ
    gq = NSA_HPG * NSA_HD
    zpad = jnp.zeros((NSA_TQ, LANES - NSA_HD), jnp.float32)
    for g in range(NSA_GROUPS):
        lanes = slice(g * NSA_HD, (g + 1) * NSA_HD)
        qg = q_ref[:, g * gq:(g + 1) * gq] * NSA_SCALE
        q_all = jnp.concatenate([jnp.concatenate([qg[:, h * NSA_HD:(h + 1) * NSA_HD], zpad], axis=1)
                                 for h in range(NSA_HPG)], axis=0).astype(jnp.bfloat16)

        kc = jnp.concatenate([kc_ref[:, lanes], jnp.zeros((nb, LANES - NSA_HD), jnp.float32)],
                             axis=1).astype(jnp.bfloat16)
        vc = vc_ref[:, lanes].astype(jnp.bfloat16)
        cmask = ((nidx + 1) * NSA_BLOCK - 1) <= tpos
        s = jnp.where(cmask[None], _dot_nt(q_all, kc).reshape(NSA_HPG, NSA_TQ, nb), NEG)
        e = jnp.exp(s - jnp.max(s, axis=-1, keepdims=True))
        p = e / jnp.sum(e, axis=-1, keepdims=True) * cmask[None].astype(jnp.float32)
        imp = jnp.sum(p, axis=0)
        o_cmp = jnp.dot(p.reshape(rows, nb).astype(jnp.bfloat16), vc, preferred_element_type=jnp.float32)

        rank = jnp.zeros((NSA_TQ, nb), jnp.int32)
        for m in range(nb):
            col = imp[:, m:m + 1]
            beats = (col > imp) | ((col == imp) & (m < nidx))
            rank = rank + jnp.where(beats & (m < blk_t), 1, 0)
        sel_f = (((nidx < blk_t) & (rank < NSA_TOPK)) | (nidx == blk_t)).astype(jnp.float32)
        blk_id = lax.broadcasted_iota(jnp.int32, (nb, FLASH_TK), 0)

        def sel_mask(k0):
            spos = k0 + lax.broadcasted_iota(jnp.int32, (1, FLASH_TK), 1)
            expand = (blk_id == spos // NSA_BLOCK).astype(jnp.float32)
            chosen = jnp.dot(sel_f, expand, preferred_element_type=jnp.float32) > 0.5
            return chosen & (spos <= tpos)

        def win_mask(k0):
            spos = k0 + lax.broadcasted_iota(jnp.int32, (1, FLASH_TK), 1)
            return (spos <= tpos) & (spos > tpos - NSA_WINDOW)

        kt_hi = (t0 + NSA_TQ + FLASH_TK - 1) // FLASH_TK
        _flash_init(m_scr, acc_scr)
        _flash_pass(q_all, kv_scr.at[g], kv_scr.at[NSA_GROUPS + g], 0, kt_hi, sel_mask, NSA_HPG, m_scr, acc_scr)
        o_sel = _flash_result(acc_scr, NSA_HD)
        kt_lo = jnp.maximum(t0 - (NSA_WINDOW - 1), 0) // FLASH_TK
        _flash_init(m_scr, acc_scr)
        _flash_pass(q_all, kv_scr.at[2 * NSA_GROUPS + g], kv_scr.at[3 * NSA_GROUPS + g], kt_lo, kt_hi, win_mask,
                    NSA_HPG, m_scr, acc_scr)
        o_win = _flash_result(acc_scr, NSA_HD)

        outs = []
        for h in range(NSA_HPG):
            r = slice(h * NSA_TQ, (h + 1) * NSA_TQ)
            c = (g * NSA_HPG + h) * 3
            outs.append(gates_ref[:, c:c + 1] * o_cmp[r] + gates_ref[:, c + 1:c + 2] * o_sel[r]
                        + gates_ref[:, c + 2:c + 3] * o_win[r])
        o_ref[:, g * gq:(g + 1) * gq] = jnp.concatenate(outs, axis=1)


def nsa_prompt_attn(q, rows, win, k_c, v_c, gates):
    b, t, _ = q.shape
    nb = t // NSA_BLOCK
    assert t % FLASH_TK == 0 and t % NSA_TQ == 0
    gw = NSA_GROUPS * NSA_HD
    qw = NSA_HEADS * NSA_HD
    return pl.pallas_call(
        _nsa_body,
        out_shape=jax.ShapeDtypeStruct((b, t, qw), jnp.float32),
        grid=(b, t // NSA_TQ),
        in_specs=[pl.BlockSpec((None, NSA_TQ, qw), lambda i, j: (i, j, 0)),
                  pl.BlockSpec((None, t, 4 * gw), lambda i, j: (i, 0, 0)),
                  pl.BlockSpec((None, t, 2 * gw), lambda i, j: (i, 0, 0)),
                  pl.BlockSpec((None, nb, gw), lambda i, j: (i, 0, 0)),
                  pl.BlockSpec((None, nb, gw), lambda i, j: (i, 0, 0)),
                  pl.BlockSpec((None, NSA_TQ, NSA_HEADS * 3), lambda i, j: (i, j, 0))],
        out_specs=pl.BlockSpec((None, NSA_TQ, qw), lambda i, j: (i, j, 0)),
        scratch_shapes=[pltpu.VMEM((4 * NSA_GROUPS, t, LANES), jnp.bfloat16),
                        pltpu.VMEM((NSA_HPG * NSA_TQ, LANES), jnp.float32),
                        pltpu.VMEM((NSA_HPG * NSA_TQ, LANES), jnp.float32)],
        compiler_params=pltpu.CompilerParams(dimension_semantics=("arbitrary", "arbitrary"),
                                             vmem_limit_bytes=VMEM_LIMIT_BYTES),
        name="nsa_prompt_attn",
    )(q, rows, win, k_c, v_c, gates)


def nsa_prompt(h, w_in, b_gate, g_qk, w_cmp, w_out):
    b, t, _ = h.shape
    q, rows, win, gates = nsa_project(h, w_in, b_gate, g_qk)
    k_c, v_c = nsa_compress(rows[:, :, :2], w_cmp, g_qk[1])
    nb = t // NSA_BLOCK
    o = nsa_prompt_attn(q.reshape(b, t, -1), rows.reshape(b, t, -1), win.reshape(b, t, -1),
                        k_c.reshape(b, nb, -1), v_c.reshape(b, nb, -1), gates.reshape(b, t, -1))
    return o @ w_out, rows, win[:, -min(NSA_WINDOW, t):]


def _masked_softmax_rows(s, mask):
    sm = jnp.where(mask, s, NEG)
    e = jnp.exp(sm - jnp.max(sm, axis=-1, keepdims=True))
    return e / jnp.sum(e, axis=-1, keepdims=True) * mask.astype(jnp.float32)


def _nsa_sample_body(pt_ref, *refs):
    n_pages = len(refs) - 15
    pages = refs[:n_pages]
    (q_ref, knew_ref, vnew_ref, win_ref, kwnew_ref, vwnew_ref, gates_ref, wcmp_ref, gk_ref,
     o_ref, cmp_scr, kv_scr, sel_scr, ocmp_scr, s_scr) = refs[n_pages:]
    ph = pl.program_id(1)
    gd = NSA_GROUPS * NSA_HD
    half = 2 * gd
    rows = DEC_SEQ * NSA_HPG
    n_past = n_pages * PAGE_SIZE
    trow = lax.broadcasted_iota(jnp.int32, (rows, 1), 0) // NSA_HPG
    lane = lax.broadcasted_iota(jnp.int32, (1, LANES), 1)
    own_mask = (lane <= trow) & (lane < DEC_SEQ)

    @pl.when(ph == 0)
    def _():
        for p in range(n_pages):
            cmp_scr[p * half:(p + 1) * half, :] = pages[p][...]
        for c in range(2):
            acc = jnp.zeros((NSA_GROUPS * n_pages, LANES), jnp.float32)
            for d in range(NSA_HD):
                lhs = jnp.concatenate([cmp_scr[pl.ds((c * NSA_GROUPS + g) * NSA_HD + d, n_pages, stride=half), :]
                                       for g in range(NSA_GROUPS)], axis=0)
                acc = acc + jnp.dot(lhs.astype(jnp.bfloat16), wcmp_ref[c, d], preferred_element_type=jnp.float32)
            kv_scr[c] = acc
        blk_t = (PAST_LEN + lax.broadcasted_iota(jnp.int32, (DEC_SEQ, 1), 0)) // NSA_BLOCK
        blk = 2 * (lane % n_pages) + lane // n_pages
        cmask = ((blk + 1) * NSA_BLOCK - 1) <= (PAST_LEN + trow)
        for g in range(NSA_GROUPS):
            qg = (q_ref[g] * NSA_SCALE).astype(jnp.bfloat16)
            kc = kv_scr[0, g * n_pages:(g + 1) * n_pages, :]
            vc = kv_scr[1, g * n_pages:(g + 1) * n_pages, :]
            s_parts = []
            for n in range(2):
                x = kc[:, n * NSA_HD:(n + 1) * NSA_HD]
                kn = x * lax.rsqrt(jnp.mean(x * x, axis=-1, keepdims=True) + RMS_EPS) * gk_ref[...]
                s_parts.append(_dot_nt(qg, kn.astype(jnp.bfloat16)))
            p = _masked_softmax_rows(jnp.concatenate(s_parts, axis=1), cmask)
            imp = jnp.sum(p.reshape(DEC_SEQ, NSA_HPG, LANES), axis=1)
            pb = p.astype(jnp.bfloat16)
            ocmp_scr[g] = (
                jnp.dot(pb[:, :n_pages], vc[:, :NSA_HD].astype(jnp.bfloat16), preferred_element_type=jnp.float32)
                + jnp.dot(pb[:, n_pages:], vc[:, NSA_HD:].astype(jnp.bfloat16), preferred_element_type=jnp.float32))
            rank = jnp.zeros((DEC_SEQ, LANES), jnp.int32)
            for m in range(2 * n_pages):
                bm = 2 * (m % n_pages) + m // n_pages
                col = imp[:, m:m + 1]
                beats = (col > imp) | ((col == imp) & (bm < blk))
                rank = rank + jnp.where(beats & (bm < blk_t), 1, 0)
            sel_scr[g] = ((blk < blk_t) & (rank < NSA_TOPK)).astype(jnp.float32)

    @pl.when(ph == 1)
    def _():
        span = NSA_SEL_PAGES * PAGE_SIZE
        for g in range(NSA_GROUPS):
            qg = (q_ref[g] * NSA_SCALE).astype(jnp.bfloat16)
            sel_rows = jnp.broadcast_to(sel_scr[g][:, None, :], (DEC_SEQ, NSA_HPG, LANES)).reshape(rows, LANES)
            krows = slice(g * NSA_HD, (g + 1) * NSA_HD)
            vrows = slice(gd + g * NSA_HD, gd + (g + 1) * NSA_HD)
            for ch in range(n_pages // NSA_SEL_PAGES):
                ps = range(ch * NSA_SEL_PAGES, (ch + 1) * NSA_SEL_PAGES)
                k_t = jnp.concatenate([pages[p][krows, :] for p in ps], axis=1).astype(jnp.bfloat16)
                s = jnp.dot(qg, k_t, preferred_element_type=jnp.float32)
                chosen = jnp.concatenate(
                    [jnp.where(lane < NSA_BLOCK, sel_rows[:, p:p + 1], sel_rows[:, n_pages + p:n_pages + p + 1])
                     for p in ps], axis=1) > 0.5
                s_scr[:, ch * span:(ch + 1) * span] = jnp.where(chosen, s, NEG)
            s_new = jnp.dot(qg, knew_ref[g].astype(jnp.bfloat16), preferred_element_type=jnp.float32)
            s_scr[:, n_past:] = jnp.where(own_mask, s_new, NEG)
            sa = s_scr[...]
            e = jnp.exp(sa - jnp.max(sa, axis=-1, keepdims=True))
            pb = (e / jnp.sum(e, axis=-1, keepdims=True)).astype(jnp.bfloat16)
            o_sel = jnp.dot(pb[:, n_past:], vnew_ref[g].astype(jnp.bfloat16), preferred_element_type=jnp.float32)
            for ch in range(n_pages // NSA_SEL_PAGES):
                ps = range(ch * NSA_SEL_PAGES, (ch + 1) * NSA_SEL_PAGES)
                v_t = jnp.concatenate([pages[p][vrows, :] for p in ps], axis=1).astype(jnp.bfloat16)
                o_sel = o_sel + _dot_nt(pb[:, ch * span:(ch + 1) * span], v_t)

            wb = win_ref.shape[1]
            wpos = PAST_LEN - wb + lax.broadcasted_iota(jnp.int32, (1, wb), 1)
            sw = jnp.dot(qg, win_ref[krows, :].astype(jnp.bfloat16), preferred_element_type=jnp.float32)
            sw_new = jnp.dot(qg, kwnew_ref[g].astype(jnp.bfloat16), preferred_element_type=jnp.float32)
            wmask = jnp.concatenate([wpos > (PAST_LEN + trow - NSA_WINDOW), jnp.broadcast_to(own_mask, (rows, LANES))],
                                    axis=1)
            pw = _masked_softmax_rows(jnp.concatenate([sw, sw_new], axis=1), wmask).astype(jnp.bfloat16)
            o_win = (_dot_nt(pw[:, :wb], win_ref[vrows, :].astype(jnp.bfloat16))
                     + jnp.dot(pw[:, wb:], vwnew_ref[g].astype(jnp.bfloat16), preferred_element_type=jnp.float32))
            gt = gates_ref[g]
            o_ref[g] = gt[:, 0:1] * ocmp_scr[g] + gt[:, 1:2] * o_sel + gt[:, 2:3] * o_win


def nsa_sample_attn(pool_t, li, page_table, q, k_new, v_new, win_t, kw_new, vw_new, gates, w_cmp_t, g_k):
    b, n_pages = page_table.shape
    assert n_pages % NSA_SEL_PAGES == 0 and 2 * n_pages == LANES and PAGE_SIZE == 2 * NSA_BLOCK
    gd = NSA_GROUPS * NSA_HD
    rows = DEC_SEQ * NSA_HPG
    wb = win_t.shape[-1]

    def page_spec(p):
        return pl.BlockSpec((None, None, 2 * gd, PAGE_SIZE), lambda i, ph, pt: (li, pt[i, p], ph, 0))

    def per_seq(*shape):
        return pl.BlockSpec((None,) + shape, lambda i, ph, pt: (i,) + (0,) * len(shape))

    grid_spec = pltpu.PrefetchScalarGridSpec(
        num_scalar_prefetch=1,
        grid=(b, 2),
        in_specs=[page_spec(p) for p in range(n_pages)] + [
            per_seq(NSA_GROUPS, rows, NSA_HD), per_seq(NSA_GROUPS, NSA_HD, LANES), per_seq(NSA_GROUPS, LANES, NSA_HD),
            per_seq(2 * gd, wb), per_seq(NSA_GROUPS, NSA_HD, LANES), per_seq(NSA_GROUPS, LANES, NSA_HD),
            per_seq(NSA_GROUPS, rows, 3),
            pl.BlockSpec((2, NSA_HD, LANES, LANES), lambda i, ph, pt: (0, 0, 0, 0)),
            pl.BlockSpec((1, NSA_HD), lambda i, ph, pt: (0, 0))],
        out_specs=per_seq(NSA_GROUPS, rows, NSA_HD),
        scratch_shapes=[pltpu.VMEM((n_pages * 2 * gd, PAGE_SIZE), jnp.float32),
                        pltpu.VMEM((2, NSA_GROUPS * n_pages, LANES), jnp.float32),
                        pltpu.VMEM((NSA_GROUPS, DEC_SEQ, LANES), jnp.float32),
                        pltpu.VMEM((NSA_GROUPS, rows, NSA_HD), jnp.float32),
                        pltpu.VMEM((rows, n_pages * PAGE_SIZE + LANES), jnp.float32)])
    return pl.pallas_call(
        _nsa_sample_body,
        out_shape=jax.ShapeDtypeStruct((b, NSA_GROUPS, rows, NSA_HD), jnp.float32),
        grid_spec=grid_spec,
        compiler_params=pltpu.CompilerParams(dimension_semantics=("arbitrary", "arbitrary"),
                                             vmem_limit_bytes=VMEM_LIMIT_BYTES),
        name="nsa_sample_attn",
    )(page_table, *([pool_t] * n_pages), q, k_new, v_new, win_t, kw_new, vw_new, gates, w_cmp_t, g_k)


def nsa_cmp_block_weights(w_cmp):
    w = w_cmp.transpose(0, 2, 1, 3)
    z = jnp.zeros_like(w)
    return jnp.concatenate([jnp.concatenate([w, z], axis=-1), jnp.concatenate([z, w], axis=-1)],
                           axis=-2).astype(jnp.bfloat16)


def nsa_sample(h, pool, li, page_table, win_buf, w_in, b_gate, g_qk, w_cmp, w_out):
    b, t, _ = h.shape
    q, rows, win, gates = nsa_project(h, w_in, b_gate, g_qk)
    pool_t = pool.transpose(0, 1, 3, 4, 5, 2).reshape(pool.shape[0], pool.shape[1], 4 * NSA_GROUPS * NSA_HD, PAGE_SIZE)
    win_t = win_buf.transpose(0, 2, 3, 4, 1).reshape(b, 2 * NSA_GROUPS * NSA_HD, win_buf.shape[1])

    def keys_t(x):
        return jnp.pad(x.transpose(0, 2, 3, 1), ((0, 0), (0, 0), (0, 0), (0, LANES - t)))

    def vals(x):
        return jnp.pad(x.transpose(0, 2, 1, 3), ((0, 0), (0, 0), (0, LANES - t), (0, 0)))

    o = nsa_sample_attn(pool_t, li, page_table,
                        q.transpose(0, 2, 1, 3, 4).reshape(b, NSA_GROUPS, t * NSA_HPG, NSA_HD),
                        keys_t(rows[:, :, 2]), vals(rows[:, :, 3]), win_t, keys_t(win[:, :, 0]), vals(win[:, :, 1]),
                        gates.transpose(0, 2, 1, 3, 4).reshape(b, NSA_GROUPS, t * NSA_HPG, 3),
                        nsa_cmp_block_weights(w_cmp), g_qk[1].reshape(1, NSA_HD))
    o = o.reshape(b, NSA_GROUPS, t, NSA_HPG, NSA_HD).transpose(0, 2, 1, 3, 4).reshape(b, t, NSA_HEADS * NSA_HD)
    new_buf = jnp.concatenate([win_buf.astype(win.dtype), win], axis=1)[:, -win_buf.shape[1]:]
    return o @ w_out, rows, new_buf


def kernel(x_prompt, x_sample, state_mlstm_C, state_mlstm_n, state_mlstm_m, cache_mla, cache_nsa,
           state_nsa_win, state_gdn_S, state_gdn_conv, page_table, c_prompt, c_sample,
           w_ada, b_ada, g_norm, w_ffn_up, w_ffn_down,
           w_ml_in, b_ml_gate, g_ml_hn, w_ml_out,
           w_mla_in, g_mla_qa, g_mla_kva, w_mla_uq, w_mla_uk, w_mla_uv, g_mla_qk, w_mla_out,
           w_nsa_in, b_nsa_gate, g_nsa_qk, w_nsa_cmp, w_nsa_out,
           w_gdn_in, w_gdn_conv, a_gdn_log, b_gdn_dt, g_gdn_out, w_gdn_out):
    bp = x_prompt.shape[0]
    xp, xs = x_prompt, x_sample
    ml_p, ml_s, mla_p, mla_s, nsa_p, nsa_s, gdn_p, gdn_s = [], [], [], [], [], [], [], []
    w_up_bf = w_ffn_up.astype(jnp.bfloat16)
    w_dn_bf = w_ffn_down.astype(jnp.bfloat16)
    for i in range(DEPTH):
        kind, li = i % N_MIXERS, i // N_MIXERS
        mod_p = adaln(c_prompt, w_ada[i], b_ada[i])
        mod_s = adaln(c_sample, w_ada[i], b_ada[i])
        xp = ffn_half_prompt(xp, g_norm[i, 0], mod_p, 0, w_up_bf[i, 0], w_dn_bf[i, 0])
        xs = ffn_half_sample(xs, g_norm[i, 0], mod_s, 0, w_up_bf[i, 0], w_dn_bf[i, 0])
        hp = modulate(xp, g_norm[i, 1], mod_p, 1)
        hs = modulate(xs, g_norm[i, 1], mod_s, 1)
        if kind == 0:
            wts = (w_ml_in[li], b_ml_gate[li], g_ml_hn[li], w_ml_out[li])
            op, c_p, n_p, m_p = mlstm_mixer(hp, *wts, jnp.zeros((bp, ML_HEADS, ML_DQK, ML_DV), jnp.float32),
                                            jnp.zeros((bp, ML_HEADS, ML_DQK), jnp.float32),
                                            jnp.zeros((bp, ML_HEADS), jnp.float32))
            os_, c_s, n_s, m_s = mlstm_mixer(hs, *wts, state_mlstm_C[li], state_mlstm_n[li], state_mlstm_m[li])
            ml_p.append((c_p, n_p, m_p))
            ml_s.append((c_s, n_s, m_s))
        elif kind == 1:
            wts = (w_mla_in[li], g_mla_qa[li], g_mla_kva[li], w_mla_uq[li], w_mla_uk[li], w_mla_uv[li],
                   g_mla_qk[li], w_mla_out[li])
            op, r_p = mla_prompt(hp, *wts)
            os_, r_s = mla_sample(hs, cache_mla, li, page_table, *wts)
            mla_p.append((r_p,))
            mla_s.append((r_s,))
        elif kind == 2:
            wts = (w_nsa_in[li], b_nsa_gate[li], g_nsa_qk[li], w_nsa_cmp[li], w_nsa_out[li])
            op, r_p, wb_p = nsa_prompt(hp, *wts)
            os_, r_s, wb_s = nsa_sample(hs, cache_nsa, li, page_table, state_nsa_win[li], *wts)
            nsa_p.append((r_p, wb_p))
            nsa_s.append((r_s, wb_s))
        else:
            wts = (w_gdn_in[li], w_gdn_conv[li], a_gdn_log[li], b_gdn_dt[li], g_gdn_out[li], w_gdn_out[li])
            op, s_p, cv_p = gdn_mixer(hp, *wts, jnp.zeros((bp, GDN_HEADS, GDN_DK, GDN_DV), jnp.float32),
                                      jnp.zeros((bp, GDN_CONV - 1, GDN_CONV_DIM), hp.dtype))
            os_, s_s, cv_s = gdn_mixer(hs, *wts, state_gdn_S[li], state_gdn_conv[li])
            gdn_p.append((s_p, cv_p))
            gdn_s.append((s_s, cv_s))
        xp = xp + mod_p[:, 1, 2][:, None] * op
        xs = xs + mod_s[:, 1, 2][:, None] * os_
        xp = ffn_half_prompt(xp, g_norm[i, 2], mod_p, 2, w_up_bf[i, 1], w_dn_bf[i, 1])
        xs = ffn_half_sample(xs, g_norm[i, 2], mod_s, 2, w_up_bf[i, 1], w_dn_bf[i, 1])
    return (xp, xs,
            _stack(ml_p, 0), _stack(ml_s, 0), _stack(ml_p, 1), _stack(ml_s, 1), _stack(ml_p, 2), _stack(ml_s, 2),
            _stack(mla_p, 0), _stack(mla_s, 0),
            _stack(nsa_p, 0), _stack(nsa_s, 0), _stack(nsa_p, 1), _stack(nsa_s, 1),
            _stack(gdn_p, 0), _stack(gdn_s, 0), _stack(gdn_p, 1), _stack(gdn_s, 1))
```

```python
import functools

import jax
import jax.numpy as jnp
from jax import lax
from jax.experimental import pallas as pl
from jax.experimental.pallas import tpu as pltpu

D_MODEL = 1024
BATCH = 16
SEQ = 2048
DEPTH = 4
DEC_BATCH = 128
DEC_SEQ = 4
PAST_LEN = 8192
PAGE_SIZE = 128

N_MIXERS = 4
D_FF = 2816
RMS_EPS = 1e-6
NEG = -1e30
Q_BLOCK = 128

ML_HEADS = 8
ML_DQK = 64
ML_DV = 128
ML_CHUNK = 64
ML_GATE_CAP = 15.0

MLA_HEADS = 16
MLA_NOPE = 64
MLA_ROPE = 32
MLA_VD = 64
MLA_KV_LORA = 256
MLA_Q_LORA = 384
MLA_LAT = MLA_KV_LORA + MLA_ROPE
MLA_SCALE = (MLA_NOPE + MLA_ROPE) ** -0.5
ROPE_THETA = 10000.0

NSA_HEADS = 16
NSA_GROUPS = 2
NSA_HPG = NSA_HEADS // NSA_GROUPS
NSA_HD = 64
NSA_BLOCK = 64
NSA_TOPK = 15
NSA_WINDOW = 512
NSA_QB = 32
NSA_SCALE = NSA_HD ** -0.5

GDN_HEADS = 8
GDN_DK = 128
GDN_DV = 128
GDN_CONV = 4
GDN_CHUNK = 64
GDN_CONV_DIM = GDN_HEADS * (2 * GDN_DK + GDN_DV)

VMEM_LIMIT_BYTES = 56 * 1024 * 1024
FFN_ROW_TILE = 512
FFN_COL_CHUNK = 256
PROJ_COL_CHUNK = 256
GDN_HEADS_PER_STEP = 8
GDN_INV_BLOCK = 4
LANES = 128
FLASH_TK = 256
NSA_TQ = 128
NSA_SEL_PAGES = 8
MLA_TQ = 512
MLA_PAGES_PER_STEP = 8
MLA_NEW_ROWS = 256
_HI = lax.Precision.HIGHEST


def _ffn_body(x_ref, shift_ref, scale_ref, gate_ref, g_ref, wup_ref, wdn_ref, o_ref, acc_ref):
    x = x_ref[...]
    y = x * lax.rsqrt(jnp.mean(x * x, axis=-1, keepdims=True) + RMS_EPS) * g_ref[...]
    h = (y * (1.0 + scale_ref[...]) + shift_ref[...]).astype(jnp.bfloat16)
    for j in range(D_FF // FFN_COL_CHUNK):
        lo = j * FFN_COL_CHUNK
        a = jnp.dot(h, wup_ref[:, lo:lo + FFN_COL_CHUNK], preferred_element_type=jnp.float32)
        b = jnp.dot(h, wup_ref[:, D_FF + lo:D_FF + lo + FFN_COL_CHUNK], preferred_element_type=jnp.float32)
        act = (a * jax.nn.sigmoid(a) * b).astype(jnp.bfloat16)
        part = jnp.dot(act, wdn_ref[lo:lo + FFN_COL_CHUNK, :], preferred_element_type=jnp.float32)
        if j == 0:
            acc_ref[...] = part
        else:
            acc_ref[...] += part
    o_ref[...] = x + 0.5 * gate_ref[...] * acc_ref[...]


def _ffn_call(x2d, shift, scale, gate, g, w_up, w_down, rows_per_mod):
    n = x2d.shape[0]
    tm = min(FFN_ROW_TILE, n)
    assert n % tm == 0
    if rows_per_mod:
        assert rows_per_mod % tm == 0
        tiles_per_mod = rows_per_mod // tm
        mod_spec = pl.BlockSpec((None, 1, D_MODEL), lambda i: (i // tiles_per_mod, 0, 0))
    else:
        mod_spec = pl.BlockSpec((tm, D_MODEL), lambda i: (i, 0))
    resident = functools.partial(pl.BlockSpec, pipeline_mode=pl.Buffered(1))
    return pl.pallas_call(
        _ffn_body,
        out_shape=jax.ShapeDtypeStruct((n, D_MODEL), jnp.float32),
        grid=(n // tm,),
        in_specs=[
            pl.BlockSpec((tm, D_MODEL), lambda i: (i, 0)),
            mod_spec, mod_spec, mod_spec,
            resident((1, D_MODEL), lambda i: (0, 0)),
            resident((D_MODEL, 2 * D_FF), lambda i: (0, 0)),
            resident((D_FF, D_MODEL), lambda i: (0, 0)),
        ],
        out_specs=pl.BlockSpec((tm, D_MODEL), lambda i: (i, 0)),
        scratch_shapes=[pltpu.VMEM((tm, D_MODEL), jnp.float32)],
        compiler_params=pltpu.CompilerParams(dimension_semantics=("arbitrary",),
                                             vmem_limit_bytes=VMEM_LIMIT_BYTES),
        name="ffn_half",
    )(x2d, shift, scale, gate, g.reshape(1, D_MODEL), w_up, w_down)


def ffn_half_prompt(x, g, mod, j, w_up, w_down):
    b, t, d = x.shape
    m = [mod[:, j, k][:, None, :] for k in range(3)]
    return _ffn_call(x.reshape(b * t, d), m[0], m[1], m[2], g, w_up, w_down, t).reshape(b, t, d)


def ffn_half_sample(x, g, mod, j, w_up, w_down):
    b, t, d = x.shape
    m = [jnp.repeat(mod[:, j, k], t, axis=0) for k in range(3)]
    return _ffn_call(x.reshape(b * t, d), m[0], m[1], m[2], g, w_up, w_down, 0).reshape(b, t, d)


def _mod_proj_body(x_ref, shift_ref, scale_ref, g_ref, w_ref, o_ref):
    x = x_ref[...]
    y = x * lax.rsqrt(jnp.mean(x * x, axis=-1, keepdims=True) + RMS_EPS) * g_ref[...]
    h = (y * (1.0 + scale_ref[...]) + shift_ref[...]).astype(jnp.bfloat16)
    for j in range(w_ref.shape[1] // PROJ_COL_CHUNK):
        cols = slice(j * PROJ_COL_CHUNK, (j + 1) * PROJ_COL_CHUNK)
        o_ref[:, cols] = jnp.dot(h, w_ref[:, cols], preferred_element_type=jnp.float32)


def _row_mod_specs(n, tm, rows_per_mod):
    if rows_per_mod:
        assert rows_per_mod % tm == 0
        tiles_per_mod = rows_per_mod // tm
        return pl.BlockSpec((None, 1, D_MODEL), lambda i: (i // tiles_per_mod, 0, 0))
    return pl.BlockSpec((tm, D_MODEL), lambda i: (i, 0))


def mod_proj(x, g, mod, j, w):
    b, t, d = x.shape
    n_out = w.shape[1]
    n_pad = -(-n_out // PROJ_COL_CHUNK) * PROJ_COL_CHUNK
    wp = jnp.pad(w.astype(jnp.bfloat16), ((0, 0), (0, n_pad - n_out)))
    n = b * t
    tm = min(FFN_ROW_TILE, n)
    assert n % tm == 0
    if t % tm == 0:
        shift, scale = (mod[:, j, k][:, None, :] for k in range(2))
        mod_spec = _row_mod_specs(n, tm, t)
    else:
        shift, scale = (jnp.repeat(mod[:, j, k], t, axis=0) for k in range(2))
        mod_spec = _row_mod_specs(n, tm, 0)
    resident = functools.partial(pl.BlockSpec, pipeline_mode=pl.Buffered(1))
    out = pl.pallas_call(
        _mod_proj_body,
        out_shape=jax.ShapeDtypeStruct((n, n_pad), jnp.float32),
        grid=(n // tm,),
        in_specs=[pl.BlockSpec((tm, D_MODEL), lambda i: (i, 0)), mod_spec, mod_spec,
                  resident((1, D_MODEL), lambda i: (0, 0)), resident((D_MODEL, n_pad), lambda i: (0, 0))],
        out_specs=pl.BlockSpec((tm, n_pad), lambda i: (i, 0)),
        compiler_params=pltpu.CompilerParams(dimension_semantics=("arbitrary",),
                                             vmem_limit_bytes=VMEM_LIMIT_BYTES),
        name="mod_proj",
    )(x.reshape(n, d), shift, scale, g.reshape(1, D_MODEL), wp)
    return out[:, :n_out].reshape(b, t, n_out)


def _proj_residual_body(o_ref, x_ref, gate_ref, w_ref, y_ref):
    y_ref[...] = x_ref[...] + gate_ref[...] * jnp.dot(o_ref[...].astype(jnp.bfloat16), w_ref[...],
                                                      preferred_element_type=jnp.float32)


def proj_residual(x, o, gate, w):
    b, t, d = x.shape
    kdim = o.shape[-1]
    n = b * t
    tm = min(FFN_ROW_TILE, n)
    assert n % tm == 0
    if t % tm == 0:
        gate_rows, gate_spec = gate[:, None, :], _row_mod_specs(n, tm, t)
    else:
        gate_rows, gate_spec = jnp.repeat(gate, t, axis=0), _row_mod_specs(n, tm, 0)
    return pl.pallas_call(
        _proj_residual_body,
        out_shape=jax.ShapeDtypeStruct((n, d), jnp.float32),
        grid=(n // tm,),
        in_specs=[pl.BlockSpec((tm, kdim), lambda i: (i, 0)), pl.BlockSpec((tm, d), lambda i: (i, 0)), gate_spec,
                  pl.BlockSpec((kdim, d), lambda i: (0, 0), pipeline_mode=pl.Buffered(1))],
        out_specs=pl.BlockSpec((tm, d), lambda i: (i, 0)),
        compiler_params=pltpu.CompilerParams(dimension_semantics=("arbitrary",),
                                             vmem_limit_bytes=VMEM_LIMIT_BYTES),
        name="proj_residual",
    )(o.reshape(n, kdim), x.reshape(n, d), gate_rows, w.astype(jnp.bfloat16)).reshape(b, t, d)


def rmsnorm(x, g):
    xf = x.astype(jnp.float32)
    y = xf * lax.rsqrt(jnp.mean(xf * xf, axis=-1, keepdims=True) + RMS_EPS)
    return (y * g.astype(jnp.float32)).astype(x.dtype)


def l2norm(x):
    xf = x.astype(jnp.float32)
    return xf * lax.rsqrt(jnp.sum(xf * xf, axis=-1, keepdims=True) + 1e-6)


def rope(x, pos):
    half = x.shape[-1] // 2
    freqs = ROPE_THETA ** (-jnp.arange(half, dtype=jnp.float32) / half)
    ang = pos[:, None] * freqs[None, :]
    ang = ang.reshape((ang.shape[0],) + (1,) * (x.ndim - 3) + (half,))
    cos, sin = jnp.cos(ang), jnp.sin(ang)
    xf = x.astype(jnp.float32)
    x1, x2 = xf[..., :half], xf[..., half:]
    return jnp.concatenate([x1 * cos - x2 * sin, x2 * cos + x1 * sin], axis=-1).astype(x.dtype)


def masked_softmax(s, mask):
    return jax.nn.softmax(jnp.where(mask, s.astype(jnp.float32), NEG), axis=-1) * mask


def adaln(c, w, b):
    return (jax.nn.silu(c) @ w + b).reshape(c.shape[0], 3, 3, c.shape[1])


def modulate(x, g, mod, j):
    return rmsnorm(x, g) * (1.0 + mod[:, j, 1][:, None]) + mod[:, j, 0][:, None]


def _chunks(a, chunk):
    b, h, t = a.shape[:3]
    return jnp.moveaxis(a.reshape((b, h, t // chunk, chunk) + a.shape[3:]), 2, 0)


def _unchunk(a):
    a = jnp.moveaxis(a, 0, 2)
    return a.reshape(a.shape[:2] + (a.shape[2] * a.shape[3],) + a.shape[4:])


def _stack(entries, j):
    return jnp.stack([e[j] for e in entries])


def _mlstm_body(q_ref, k_ref, v_ref, icol_ref, irow_ref, fcol_ref, frow_ref, ghn_ref, c0_ref, m0_ref,
                o_ref, c_out_ref, m_out_ref, c_scr, m_scr):
    c = pl.program_id(1)
    L = ML_CHUNK

    @pl.when(c == 0)
    def _():
        c_scr[...] = c0_ref[...]
        m_scr[...] = m0_ref[...]

    row = lax.broadcasted_iota(jnp.int32, (L, L), 0)
    col = lax.broadcasted_iota(jnp.int32, (L, L), 1)
    causal = col <= row
    b_cols = jnp.dot(causal.astype(jnp.float32), fcol_ref[...], precision=_HI, preferred_element_type=jnp.float32)
    b_rows = jnp.dot(frow_ref[...], (col >= row).astype(jnp.float32), precision=_HI,
                     preferred_element_type=jnp.float32)

    def per_head(f):
        return jnp.stack([f(h) for h in range(ML_HEADS)], axis=0)

    q = per_head(lambda h: q_ref[:, h * ML_DQK:(h + 1) * ML_DQK]) * ML_DQK ** -0.5
    k = per_head(lambda h: k_ref[:, h * ML_DQK:(h + 1) * ML_DQK])
    one_col = (lax.broadcasted_iota(jnp.int32, (L, LANES), 1) == 0).astype(jnp.float32)
    v1 = per_head(lambda h: jnp.concatenate([v_ref[:, h * ML_DV:(h + 1) * ML_DV], one_col], axis=1)).astype(jnp.bfloat16)
    bc = per_head(lambda h: b_cols[:, h:h + 1])
    br = per_head(lambda h: b_rows[h:h + 1, :])
    ic = per_head(lambda h: icol_ref[:, h:h + 1])
    ir = per_head(lambda h: irow_ref[h:h + 1, :])
    m_old = m_scr[:, 0:1, 0:1]
    dmat = jnp.where(causal[None], bc - br + ir, NEG)
    inter = bc + m_old
    mt = jnp.maximum(inter, jnp.max(dmat, axis=-1, keepdims=True))
    w_inter = jnp.exp(inter - mt)
    qb, kb = q.astype(jnp.bfloat16), k.astype(jnp.bfloat16)
    s = _bdot_nt(qb, kb) * jnp.exp(dmat - mt)
    cx = c_scr[...]
    nd = w_inter * _bdot(qb, cx.astype(jnp.bfloat16)) + _bdot(s.astype(jnp.bfloat16), v1)
    hc = nd[:, :, :ML_DV] / jnp.maximum(jnp.abs(nd[:, :, ML_DV:ML_DV + 1]), jnp.exp(-mt))
    hn = hc * lax.rsqrt(jnp.mean(hc * hc, axis=-1, keepdims=True) + RMS_EPS)
    for h in range(ML_HEADS):
        o_ref[:, h * ML_DV:(h + 1) * ML_DV] = hn[h] * ghn_ref[h:h + 1, :]
    b_last = bc[:, L - 1:L, :]
    m_new = mt[:, L - 1:L, :]
    kw = (jnp.exp(b_last - bc + ic - m_new) * k).astype(jnp.bfloat16)
    c_scr[...] = jnp.exp(b_last + m_old - m_new) * cx + _bdot_tn(kw, v1)
    m_scr[...] = jnp.broadcast_to(m_new, m_scr.shape)

    @pl.when(c == pl.num_programs(1) - 1)
    def _():
        c_out_ref[...] = c_scr[...]
        m_out_ref[...] = m_scr[...]


def mlstm_scan_call(zp, ig, lf, g_hn, c0, n0, m0):
    b, t, _ = zp.shape
    L = ML_CHUNK
    nc = t // L
    qw = ML_HEADS * ML_DQK
    vw = ML_HEADS * ML_DV

    def cols(z):
        return z.reshape(b, nc, L, ML_HEADS)

    def rows(z):
        return z.reshape(b, nc, L, ML_HEADS).swapaxes(2, 3)

    cx0 = jnp.concatenate([c0, n0[..., None], jnp.zeros(c0.shape[:-1] + (LANES - 1,), jnp.float32)], axis=-1)
    mx0 = jnp.broadcast_to(m0[:, :, None, None], (b, ML_HEADS, 8, LANES))
    col_spec = pl.BlockSpec((None, None, L, ML_HEADS), lambda i, c: (i, c, 0, 0))
    row_spec = pl.BlockSpec((None, None, ML_HEADS, L), lambda i, c: (i, c, 0, 0))
    c_spec = pl.BlockSpec((None, ML_HEADS, ML_DQK, ML_DV + LANES), lambda i, c: (i, 0, 0, 0))
    m_spec = pl.BlockSpec((None, ML_HEADS, 8, LANES), lambda i, c: (i, 0, 0, 0))
    o, cx, mx = pl.pallas_call(
        _mlstm_body,
        out_shape=(jax.ShapeDtypeStruct((b, t, vw), jnp.float32),
                   jax.ShapeDtypeStruct((b, ML_HEADS, ML_DQK, ML_DV + LANES), jnp.float32),
                   jax.ShapeDtypeStruct((b, ML_HEADS, 8, LANES), jnp.float32)),
        grid=(b, nc),
        in_specs=[pl.BlockSpec((None, L, qw), lambda i, c: (i, c, 0)),
                  pl.BlockSpec((None, L, qw), lambda i, c: (i, c, 1)),
                  pl.BlockSpec((None, L, vw), lambda i, c: (i, c, 1)),
                  col_spec, row_spec, col_spec, row_spec,
                  pl.BlockSpec((ML_HEADS, ML_DV), lambda i, c: (0, 0)), c_spec, m_spec],
        out_specs=(pl.BlockSpec((None, L, vw), lambda i, c: (i, c, 0)), c_spec, m_spec),
        scratch_shapes=[pltpu.VMEM((ML_HEADS, ML_DQK, ML_DV + LANES), jnp.float32),
                        pltpu.VMEM((ML_HEADS, 8, LANES), jnp.float32)],
        compiler_params=pltpu.CompilerParams(dimension_semantics=("arbitrary", "arbitrary"),
                                             vmem_limit_bytes=VMEM_LIMIT_BYTES),
        name="mlstm_scan",
    )(zp, zp, zp, cols(ig), rows(ig), cols(lf), rows(lf), g_hn, cx0, mx0)
    return o, cx[..., :ML_DV], cx[..., ML_DV], mx[:, :, 0, 0]


def mlstm_mixer(zp, b_gate, g_hn, C0, n0, m0):
    b, t, _ = zp.shape
    qk_w, v_w = ML_HEADS * ML_DQK, ML_HEADS * ML_DV
    assert 2 * qk_w == v_w
    o = zp[..., 2 * qk_w + v_w:2 * qk_w + 2 * v_w]
    gates = ML_GATE_CAP * jnp.tanh((zp[..., 2 * qk_w + 2 * v_w:] + b_gate).astype(jnp.float32) / ML_GATE_CAP)
    ig = gates[..., :ML_HEADS]
    lf = jax.nn.log_sigmoid(gates[..., ML_HEADS:])
    pad = (-t) % ML_CHUNK
    if pad:
        zp = jnp.pad(zp, ((0, 0), (0, pad), (0, 0)))
        ig = jnp.pad(ig, ((0, 0), (0, pad), (0, 0)), constant_values=NEG)
        lf = jnp.pad(lf, ((0, 0), (0, pad), (0, 0)))
    hh, C, n, m = mlstm_scan_call(zp, ig, lf, g_hn, C0.astype(jnp.float32), n0.astype(jnp.float32),
                                  m0.astype(jnp.float32))
    return jax.nn.sigmoid(o) * hh[:, :t], C, n, m


def _dot_nt(a, b, **kw):
    return lax.dot_general(a, b, (((1,), (1,)), ((), ())), preferred_element_type=jnp.float32, **kw)


def _dot_tn(a, b, **kw):
    return lax.dot_general(a, b, (((0,), (0,)), ((), ())), preferred_element_type=jnp.float32, **kw)


def _bdot(a, b, **kw):
    return lax.dot_general(a, b, (((2,), (1,)), ((0,), (0,))), preferred_element_type=jnp.float32, **kw)


def _bdot_nt(a, b, **kw):
    return lax.dot_general(a, b, (((2,), (2,)), ((0,), (0,))), preferred_element_type=jnp.float32, **kw)


def _bdot_tn(a, b, **kw):
    return lax.dot_general(a, b, (((1,), (1,)), ((0,), (0,))), preferred_element_type=jnp.float32, **kw)


def _flash_init(m_scr, acc_scr):
    m_scr[...] = jnp.full(m_scr.shape, NEG, jnp.float32)
    acc_scr[...] = jnp.zeros(acc_scr.shape, jnp.float32)


def _flash_pass(q_all, k_ref, v_ref, kt_lo, kt_hi, mask_fn, heads, m_scr, acc_scr):
    rows = q_all.shape[0]
    tq = rows // heads

    def body(kt, carry):
        k0 = pl.multiple_of(kt * FLASH_TK, FLASH_TK)
        s = _dot_nt(q_all, k_ref[pl.ds(k0, FLASH_TK), :])
        if mask_fn is not None:
            s = jnp.where(mask_fn(k0)[None], s.reshape(heads, tq, FLASH_TK), 2.0 * NEG).reshape(rows, FLASH_TK)
        m_old = m_scr[...]
        m_new = jnp.maximum(m_old, jnp.max(s, axis=-1, keepdims=True))
        p = jnp.exp(s - jnp.concatenate([m_new] * (FLASH_TK // LANES), axis=1))
        acc_scr[...] = acc_scr[...] * jnp.exp(m_old - m_new) + jnp.dot(
            p.astype(jnp.bfloat16), v_ref[pl.ds(k0, FLASH_TK), :], preferred_element_type=jnp.float32)
        m_scr[...] = m_new
        return carry

    lax.fori_loop(kt_lo, kt_hi, body, 0)


def _flash_result(acc_scr, dv):
    acc = acc_scr[...]
    return acc[:, :dv] / acc[:, dv:dv + 1]


def _gdn_body(q_ref, k_ref, v_ref, beta_ref, gcol_ref, grow_ref, s0_ref, o_ref, s_out_ref, s_scr):
    c = pl.program_id(2)
    L = GDN_CHUNK

    @pl.when(c == 0)
    def _():
        s_scr[...] = s0_ref[...]

    row = lax.broadcasted_iota(jnp.int32, (L, L), 0)
    col = lax.broadcasted_iota(jnp.int32, (L, L), 1)
    causal = col <= row
    strict = col < row
    eye = (col == row).astype(jnp.float32)
    b_cols = jnp.dot(causal.astype(jnp.float32), gcol_ref[...], precision=_HI, preferred_element_type=jnp.float32)
    b_rows = jnp.dot(grow_ref[...], (col >= row).astype(jnp.float32), precision=_HI,
                     preferred_element_type=jnp.float32)
    def per_head(f):
        return jnp.stack([f(h, slice(h * GDN_DK, (h + 1) * GDN_DK)) for h in range(GDN_HEADS_PER_STEP)], axis=0)

    qf = per_head(lambda h, lanes: q_ref[:, lanes])
    kf = per_head(lambda h, lanes: k_ref[:, lanes])
    v = per_head(lambda h, lanes: v_ref[:, lanes])
    beta = per_head(lambda h, lanes: beta_ref[:, h:h + 1])
    bc = per_head(lambda h, lanes: b_cols[:, h:h + 1])
    br = per_head(lambda h, lanes: b_rows[h:h + 1, :])
    q = qf * lax.rsqrt(jnp.sum(qf * qf, axis=-1, keepdims=True) + 1e-6) * GDN_DK ** -0.5
    k = kf * lax.rsqrt(jnp.sum(kf * kf, axis=-1, keepdims=True) + 1e-6)
    decay = jnp.exp(jnp.where(causal[None], bc - br, NEG))
    qb, kb = q.astype(jnp.bfloat16), k.astype(jnp.bfloat16)
    a_mat = jnp.where(strict[None], beta * _bdot_nt(kb, kb) * decay, 0.0)
    blk = GDN_INV_BLOCK
    a_diag = jnp.where((row // blk == col // blk)[None], a_mat, 0.0)
    a_diag2 = _bdot(a_diag, a_diag, precision=_HI)
    inv = eye[None] - a_diag + a_diag2 - _bdot(a_diag, a_diag2, precision=_HI)
    while blk < L:
        join = (row // (2 * blk) == col // (2 * blk)) & (row // blk == col // blk + 1)
        lower = jnp.where(join[None], a_mat, 0.0)
        inv = inv - _bdot(_bdot(inv, lower, precision=_HI), inv, precision=_HI)
        blk *= 2
    s = s_scr[...]
    sb = s.astype(jnp.bfloat16)
    eb = jnp.exp(bc)
    rhs = beta * (v - eb * _bdot(kb, sb))
    ub = _bdot(inv, rhs, precision=_HI).astype(jnp.bfloat16)
    qk = (_bdot_nt(qb, kb) * decay).astype(jnp.bfloat16)
    o = eb * _bdot(qb, sb) + _bdot(qk, ub)
    for h in range(GDN_HEADS_PER_STEP):
        o_ref[:, h * GDN_DV:(h + 1) * GDN_DV] = o[h]
    b_last = bc[:, L - 1:L, :]
    kw = (jnp.exp(b_last - bc) * k).astype(jnp.bfloat16)
    s_scr[...] = jnp.exp(b_last) * s + _bdot_tn(kw, ub)

    @pl.when(c == pl.num_programs(2) - 1)
    def _():
        s_out_ref[...] = s_scr[...]


def gdn_scan_call(qkv, beta, g, s0):
    b, t, _ = qkv.shape
    hb = GDN_HEADS_PER_STEP
    ng = GDN_HEADS // hb
    L = GDN_CHUNK
    w = hb * GDN_DK

    def qkv_spec(part):
        return pl.BlockSpec((None, L, w), lambda i, j, c: (i, c, part * ng + j))

    def per_chunk(z):
        return z.reshape(b, t // L, L, ng, hb).transpose(0, 3, 1, 2, 4)

    col_spec = pl.BlockSpec((None, None, None, L, hb), lambda i, j, c: (i, j, c, 0, 0))
    state_spec = pl.BlockSpec((None, hb, GDN_DK, GDN_DV), lambda i, j, c: (i, j, 0, 0))
    return pl.pallas_call(
        _gdn_body,
        out_shape=(jax.ShapeDtypeStruct((b, t, GDN_HEADS * GDN_DV), jnp.float32),
                   jax.ShapeDtypeStruct((b, GDN_HEADS, GDN_DK, GDN_DV), jnp.float32)),
        grid=(b, ng, t // L),
        in_specs=[qkv_spec(0), qkv_spec(1), qkv_spec(2), col_spec, col_spec,
                  pl.BlockSpec((None, None, None, hb, L), lambda i, j, c: (i, j, c, 0, 0)),
                  state_spec],
        out_specs=(pl.BlockSpec((None, L, w), lambda i, j, c: (i, c, j)), state_spec),
        scratch_shapes=[pltpu.VMEM((hb, GDN_DK, GDN_DV), jnp.float32)],
        compiler_params=pltpu.CompilerParams(dimension_semantics=("arbitrary", "arbitrary", "arbitrary"),
                                             vmem_limit_bytes=VMEM_LIMIT_BYTES),
        name="gdn_scan",
    )(qkv, qkv, qkv, per_chunk(beta), per_chunk(g), per_chunk(g).swapaxes(3, 4), s0)


def gdn_mixer(zp, w_conv, a_log, dt_bias, g_out, S0, conv0):
    b, t, _ = zp.shape
    qkv, beta, a, gate = jnp.split(zp, [GDN_CONV_DIM, GDN_CONV_DIM + GDN_HEADS, GDN_CONV_DIM + 2 * GDN_HEADS], axis=-1)
    xc = jnp.concatenate([conv0.astype(qkv.dtype), qkv], axis=1)
    conv = lax.conv_general_dilated(xc, w_conv[:, None, :].astype(xc.dtype), (1,), 'VALID',
                                    dimension_numbers=('NWC', 'WIO', 'NWC'), feature_group_count=GDN_CONV_DIM)
    act = jax.nn.silu(conv)
    beta = jax.nn.sigmoid(beta.astype(jnp.float32))
    g = -jnp.exp(a_log.astype(jnp.float32)) * jax.nn.softplus((a + dt_bias).astype(jnp.float32))
    pad = (-t) % GDN_CHUNK
    if pad:
        act, beta, g = (jnp.pad(z, ((0, 0), (0, pad), (0, 0))) for z in (act, beta, g))
    o, S = gdn_scan_call(act, beta, g, S0.astype(jnp.float32))
    o = o[:, :t].reshape(b, t, GDN_HEADS, GDN_DV)
    o = rmsnorm(o, g_out).reshape(b, t, GDN_HEADS * GDN_DV)
    return o * jax.nn.silu(gate), S, xc[:, -(GDN_CONV - 1):]


def mla_project(zp, pos, g_qa, g_kva, w_uq, g_qk):
    b, t, _ = zp.shape
    cq, ckv, kr = jnp.split(zp, [MLA_Q_LORA, MLA_Q_LORA + MLA_KV_LORA], axis=-1)
    q = (rmsnorm(cq, g_qa) @ w_uq).reshape(b, t, MLA_HEADS, MLA_NOPE + MLA_ROPE)
    q_nope = rmsnorm(q[..., :MLA_NOPE], g_qk[0, :MLA_NOPE])
    q_rope = rope(rmsnorm(q[..., MLA_NOPE:], g_qk[0, MLA_NOPE:]), pos)
    k_rope = rope(rmsnorm(kr, g_qk[1, MLA_NOPE:]), pos)
    latent = jnp.concatenate([rmsnorm(ckv, g_kva), k_rope], axis=-1)
    return q_nope, q_rope, latent


def mla_scores(q_nope, q_rope, k_nope, k_rope):
    s = jnp.einsum('bthd,bshd->bhts', q_nope, k_nope) + jnp.einsum('bthr,bsr->bhts', q_rope, k_rope)
    return s.astype(jnp.float32) * MLA_SCALE


def mla_keys(latent, w_uk, g_kn):
    ckv = latent[..., :MLA_KV_LORA]
    return ckv, rmsnorm(jnp.einsum('bsc,chd->bshd', ckv, w_uk), g_kn), latent[..., MLA_KV_LORA:]


def _mla_prompt_body(q_ref, k_ref, v_ref, o_ref, m_scr, acc_scr):
    qi = pl.program_id(2)
    t0 = qi * MLA_TQ
    tpos = t0 + lax.broadcasted_iota(jnp.int32, (MLA_TQ, 1), 0)

    def causal(k0):
        return (k0 + lax.broadcasted_iota(jnp.int32, (1, FLASH_TK), 1)) <= tpos

    n_full = t0 // FLASH_TK
    outs = []
    for h in range(2):
        _flash_init(m_scr, acc_scr)
        _flash_pass(q_ref[h], k_ref.at[h], v_ref.at[h], 0, n_full, None, 1, m_scr, acc_scr)
        _flash_pass(q_ref[h], k_ref.at[h], v_ref.at[h], n_full, n_full + MLA_TQ // FLASH_TK, causal, 1,
                    m_scr, acc_scr)
        outs.append(_flash_result(acc_scr, MLA_VD))
    o_ref[...] = jnp.concatenate(outs, axis=1)


def mla_prompt_attn(q, k, v):
    b, h, t, _ = q.shape
    assert t % MLA_TQ == 0 and MLA_TQ % FLASH_TK == 0 and h % 2 == 0
    seq_spec = pl.BlockSpec((None, 2, t, LANES), lambda i, j, k: (i, j, 0, 0))
    return pl.pallas_call(
        _mla_prompt_body,
        out_shape=jax.ShapeDtypeStruct((b, t, h * MLA_VD), jnp.float32),
        grid=(b, h // 2, t // MLA_TQ),
        in_specs=[pl.BlockSpec((None, 2, MLA_TQ, LANES), lambda i, j, k: (i, j, k, 0)), seq_spec, seq_spec],
        out_specs=pl.BlockSpec((None, MLA_TQ, 2 * MLA_VD), lambda i, j, k: (i, k, j)),
        scratch_shapes=[pltpu.VMEM((MLA_TQ, LANES), jnp.float32), pltpu.VMEM((MLA_TQ, LANES), jnp.float32)],
        compiler_params=pltpu.CompilerParams(dimension_semantics=("arbitrary", "arbitrary", "arbitrary"),
                                             vmem_limit_bytes=VMEM_LIMIT_BYTES),
        name="mla_prompt_attn",
    )(q, k, v)


def mla_prompt(zp, g_qa, g_kva, w_uq, w_uk, w_uv, g_qk):
    b, t, _ = zp.shape
    q_nope, q_rope, latent = mla_project(zp, jnp.arange(t, dtype=jnp.float32), g_qa, g_kva, w_uq, g_qk)
    ckv, k_nope, k_rope = mla_keys(latent, w_uk, g_qk[1, :MLA_NOPE])
    v = jnp.einsum('bsc,chd->bshd', ckv, w_uv)

    def heads_first(x):
        return x.transpose(0, 2, 1, 3)

    pad = jnp.zeros((b, MLA_HEADS, t, LANES - MLA_NOPE - MLA_ROPE), jnp.float32)
    q = jnp.concatenate([heads_first(q_nope), heads_first(q_rope), pad], axis=-1) * MLA_SCALE
    k = jnp.concatenate([heads_first(k_nope), jnp.broadcast_to(k_rope[:, None], (b, MLA_HEADS, t, MLA_ROPE)), pad],
                        axis=-1)
    v1 = jnp.concatenate([heads_first(v), jnp.ones((b, MLA_HEADS, t, 1), jnp.float32),
                          jnp.zeros((b, MLA_HEADS, t, LANES - MLA_VD - 1), jnp.float32)], axis=-1)
    return mla_prompt_attn(q.astype(jnp.bfloat16), k.astype(jnp.bfloat16), v1.astype(jnp.bfloat16)), latent


def _mla_page_scores(lat, w_uk_ref, qbd, gexp, qr):
    ckv = lat[:, :MLA_KV_LORA].astype(jnp.bfloat16)
    kn = jnp.dot(ckv, w_uk_ref[...], preferred_element_type=jnp.float32)
    s_n = _dot_nt(kn.astype(jnp.bfloat16), qbd)
    ms = jnp.dot((kn * kn).astype(jnp.bfloat16), gexp, preferred_element_type=jnp.float32)
    s_r = _dot_nt(lat[:, MLA_KV_LORA:].astype(jnp.bfloat16), qr)
    return ckv, (s_n * lax.rsqrt(ms + RMS_EPS) + s_r) * MLA_SCALE


def _mla_page_rows(page_ref):
    x = page_ref[...]
    tail = x[MLA_LAT - LANES:, :].T
    return jnp.concatenate([x[:LANES, :].T, x[LANES:2 * LANES, :].T, tail[:, LANES - MLA_ROPE:]], axis=1)


def _mla_sample_body(pt_ref, *refs):
    pp = MLA_PAGES_PER_STEP
    page_refs = refs[:pp]
    qn_ref, qr_ref, latn_ref, w_uk_ref, w_uv_ref, o_ref, ckv_scr, s_scr, qbd_scr = refs[pp:]
    j = pl.program_id(1)
    nj = pl.num_programs(1)
    nq = DEC_SEQ * MLA_HEADS
    hw = MLA_HEADS * MLA_NOPE
    n = pp * PAGE_SIZE
    n_past = nj * n

    @pl.when(j == 0)
    def _():
        r = lax.broadcasted_iota(jnp.int32, (nq, hw), 0)
        c = lax.broadcasted_iota(jnp.int32, (nq, hw), 1)
        q_rep = jnp.concatenate([qn_ref[...]] * MLA_HEADS, axis=1)
        qbd_scr[...] = jnp.where(r % MLA_HEADS == c // MLA_NOPE, q_rep, 0.0).astype(jnp.bfloat16)

    gr = lax.broadcasted_iota(jnp.int32, (hw, nq), 0)
    gc = lax.broadcasted_iota(jnp.int32, (hw, nq), 1)
    gexp = jnp.where(gr // MLA_NOPE == gc % MLA_HEADS, 1.0 / MLA_NOPE, 0.0).astype(jnp.bfloat16)
    qr = qr_ref[...].astype(jnp.bfloat16)
    lat = jnp.concatenate([_mla_page_rows(p) for p in page_refs], axis=0)
    ckv, s = _mla_page_scores(lat, w_uk_ref, qbd_scr[...], gexp, qr)
    row0 = pl.multiple_of(j * n, n)
    ones_col = (lax.broadcasted_iota(jnp.int32, (1, LANES), 1) == 0).astype(jnp.bfloat16)
    ckv_scr[pl.ds(row0, n), :MLA_KV_LORA] = ckv
    ckv_scr[pl.ds(row0, n), MLA_KV_LORA:] = jnp.broadcast_to(ones_col, (n, LANES))
    s_scr[pl.ds(row0, n), :] = s

    @pl.when(j == nj - 1)
    def _():
        ckv_n, s_n = _mla_page_scores(latn_ref[...], w_uk_ref, qbd_scr[...], gexp, qr)
        kr = lax.broadcasted_iota(jnp.int32, (MLA_NEW_ROWS, nq), 0)
        kc = lax.broadcasted_iota(jnp.int32, (MLA_NEW_ROWS, nq), 1)
        s_scr[n_past:, :] = jnp.where(kr <= kc // MLA_HEADS, s_n, NEG)
        ckv_scr[n_past:, :MLA_KV_LORA] = ckv_n
        ckv_scr[n_past:, MLA_KV_LORA:] = jnp.broadcast_to(ones_col, (MLA_NEW_ROWS, LANES))
        sc = s_scr[...]
        p = jnp.exp(sc - jnp.max(sc, axis=0, keepdims=True)).astype(jnp.bfloat16)
        acc = _dot_tn(p, ckv_scr[...])
        o_lat = (acc[:, :MLA_KV_LORA] / acc[:, MLA_KV_LORA:MLA_KV_LORA + 1]).astype(jnp.bfloat16)
        z = jnp.dot(o_lat, w_uv_ref[...], preferred_element_type=jnp.float32)
        zr = lax.broadcasted_iota(jnp.int32, (MLA_HEADS, MLA_HEADS * MLA_VD), 0)
        zc = lax.broadcasted_iota(jnp.int32, (MLA_HEADS, MLA_HEADS * MLA_VD), 1)
        keep = zr == zc // MLA_VD
        o_ref[...] = jnp.concatenate(
            [jnp.sum(jnp.where(keep, z[t * MLA_HEADS:(t + 1) * MLA_HEADS], 0.0), axis=0, keepdims=True)
             for t in range(DEC_SEQ)], axis=0)


def mla_sample_attn(pool, li, page_table, qn, qr, latent, w_uk, w_uv):
    b, n_pages = page_table.shape
    pp = MLA_PAGES_PER_STEP
    assert n_pages % pp == 0 and latent.shape[1] == DEC_SEQ
    nq = DEC_SEQ * MLA_HEADS
    n_rows = n_pages * PAGE_SIZE + MLA_NEW_ROWS
    latn = jnp.pad(latent, ((0, 0), (0, MLA_NEW_ROWS - DEC_SEQ), (0, 0)))

    def page_spec(k):
        return pl.BlockSpec((None, None, MLA_LAT, PAGE_SIZE), lambda i, j, pt: (li, pt[i, j * pp + k], 0, 0))

    def whole(r, c):
        return pl.BlockSpec((r, c), lambda i, j, pt: (0, 0))

    def per_seq(r, c):
        return pl.BlockSpec((None, r, c), lambda i, j, pt: (i, 0, 0))

    grid_spec = pltpu.PrefetchScalarGridSpec(
        num_scalar_prefetch=1,
        grid=(b, n_pages // pp),
        in_specs=[page_spec(k) for k in range(pp)] + [
            per_seq(nq, MLA_NOPE), per_seq(nq, MLA_ROPE), per_seq(MLA_NEW_ROWS, MLA_LAT),
            whole(MLA_KV_LORA, MLA_HEADS * MLA_NOPE), whole(MLA_KV_LORA, MLA_HEADS * MLA_VD)],
        out_specs=per_seq(DEC_SEQ, MLA_HEADS * MLA_VD),
        scratch_shapes=[pltpu.VMEM((n_rows, MLA_KV_LORA + LANES), jnp.bfloat16),
                        pltpu.VMEM((n_rows, nq), jnp.float32),
                        pltpu.VMEM((nq, MLA_HEADS * MLA_NOPE), jnp.bfloat16)])
    return pl.pallas_call(
        _mla_sample_body,
        out_shape=jax.ShapeDtypeStruct((b, DEC_SEQ, MLA_HEADS * MLA_VD), jnp.float32),
        grid_spec=grid_spec,
        compiler_params=pltpu.CompilerParams(dimension_semantics=("arbitrary", "arbitrary"),
                                             vmem_limit_bytes=VMEM_LIMIT_BYTES),
        name="mla_sample_attn",
    )(page_table, *([pool] * pp), qn, qr, latn, w_uk, w_uv)


def mla_sample(zp, pool, li, page_table, g_qa, g_kva, w_uq, w_uk, w_uv, g_qk):
    b, t, _ = zp.shape
    q_nope, q_rope, latent = mla_project(zp, PAST_LEN + jnp.arange(t, dtype=jnp.float32), g_qa, g_kva, w_uq, g_qk)
    qn = (q_nope * g_qk[1, :MLA_NOPE]).reshape(b, t * MLA_HEADS, MLA_NOPE)
    qr = q_rope.reshape(b, t * MLA_HEADS, MLA_ROPE)
    o = mla_sample_attn(pool.transpose(0, 1, 3, 2), li, page_table, qn, qr, latent,
                        w_uk.reshape(MLA_KV_LORA, -1).astype(jnp.bfloat16),
                        w_uv.reshape(MLA_KV_LORA, -1).astype(jnp.bfloat16))
    return o, latent


def nsa_project(zp, b_gate, g_qk):
    b, t, _ = zp.shape
    qw, kvw = NSA_HEADS * NSA_HD, 6 * NSA_GROUPS * NSA_HD
    q, kv, gates = jnp.split(zp, [qw, qw + kvw], axis=-1)
    q = rmsnorm(q.reshape(b, t, NSA_GROUPS, NSA_HPG, NSA_HD), g_qk[0])
    kv = kv.reshape(b, t, 6, NSA_GROUPS, NSA_HD)
    rows = jnp.stack([kv[:, :, 0], kv[:, :, 1], rmsnorm(kv[:, :, 2], g_qk[1]), kv[:, :, 3]], axis=2)
    win = jnp.stack([rmsnorm(kv[:, :, 4], g_qk[1]), kv[:, :, 5]], axis=2)
    gates = jax.nn.sigmoid(gates + b_gate).reshape(b, t, NSA_GROUPS, NSA_HPG, 3)
    return q, rows, win, gates


def nsa_compress(kv_rows, w_cmp, g_k):
    b, l = kv_rows.shape[:2]
    blk = kv_rows.reshape(b, l // NSA_BLOCK, NSA_BLOCK, 2, NSA_GROUPS, NSA_HD)
    c = jnp.einsum('bnpcgd,cpde->bncge', blk, w_cmp)
    return rmsnorm(c[:, :, 0], g_k), c[:, :, 1]


def nsa_cmp_attn(q, k_c, v_c, qpos):
    nb = k_c.shape[1]
    s = jnp.einsum('btghd,bngd->bghtn', q, k_c) * NSA_SCALE
    mask = ((jnp.arange(nb) + 1) * NSA_BLOCK - 1)[None, :] <= qpos[:, None]
    p = masked_softmax(s, mask)
    o = jnp.einsum('bghtn,bngd->btghd', p.astype(v_c.dtype), v_c)
    return o, jnp.sum(p, axis=2)


def nsa_select(imp, qpos):
    nb = imp.shape[-1]
    cand = jnp.arange(nb)[None, :] < (qpos // NSA_BLOCK)[:, None]
    val, idx = lax.top_k(jnp.where(cand, imp, -jnp.inf), min(NSA_TOPK, nb))
    return idx, jnp.isfinite(val)


def nsa_sel_prompt(q, k_s, v_s, idx, valid):
    b, t = q.shape[:2]
    nb = t // NSA_BLOCK
    kb = k_s.reshape(b, nb, NSA_BLOCK, NSA_GROUPS, NSA_HD).transpose(0, 3, 1, 2, 4)
    vb = v_s.reshape(b, nb, NSA_BLOCK, NSA_GROUPS, NSA_HD).transpose(0, 3, 1, 2, 4)
    bi = jnp.arange(b)[:, None, None, None]
    gi = jnp.arange(NSA_GROUPS)[None, None, :, None]
    idx, valid = idx.transpose(0, 2, 1, 3), valid.transpose(0, 2, 1, 3)

    def block(i):
        t0 = i * NSA_QB
        qpos = t0 + jnp.arange(NSA_QB)
        qq = lax.dynamic_slice_in_dim(q, t0, NSA_QB, axis=1)
        cur = jnp.broadcast_to((qpos // NSA_BLOCK)[None, :, None, None], (b, NSA_QB, NSA_GROUPS, 1))
        ii = jnp.concatenate([lax.dynamic_slice_in_dim(idx, t0, NSA_QB, axis=1), cur], axis=-1)
        ok = jnp.concatenate([lax.dynamic_slice_in_dim(valid, t0, NSA_QB, axis=1), jnp.ones(cur.shape, bool)], axis=-1)
        kpos = ii[..., None] * NSA_BLOCK + jnp.arange(NSA_BLOCK)
        mask = ok[..., None] & (kpos <= qpos[None, :, None, None, None])
        n = ii.shape[-1] * NSA_BLOCK
        ks, vs = kb[bi, gi, ii], vb[bi, gi, ii]
        s = jnp.einsum('btghd,btgkpd->bghtkp', qq, ks).reshape(b, NSA_GROUPS, NSA_HPG, NSA_QB, n) * NSA_SCALE
        mask = mask.transpose(0, 2, 1, 3, 4).reshape(b, NSA_GROUPS, 1, NSA_QB, n)
        p = masked_softmax(s, mask)
        return jnp.einsum('bghtn,btgnd->btghd', p.astype(vs.dtype), vs.reshape(b, NSA_QB, NSA_GROUPS, n, NSA_HD))

    return jnp.moveaxis(lax.map(block, jnp.arange(t // NSA_QB)), 0, 1).reshape(q.shape)


def nsa_sel_sample(q, k_new, v_new, idx, valid, pool, li, page_table):
    b, t = q.shape[:2]
    bpp = PAGE_SIZE // NSA_BLOCK
    kk = idx.shape[-1]
    n = kk * NSA_BLOCK
    ii, ok = idx.transpose(0, 2, 1, 3), valid.transpose(0, 2, 1, 3)
    phys = page_table[jnp.arange(b)[:, None, None, None], ii // bpp][..., None]
    rows = (ii % bpp)[..., None] * NSA_BLOCK + jnp.arange(NSA_BLOCK)
    gi = jnp.arange(NSA_GROUPS)[None, None, :, None, None]
    ks = pool[li, phys, rows, 2, gi].reshape(b, t, NSA_GROUPS, n, NSA_HD)
    vs = pool[li, phys, rows, 3, gi].reshape(b, t, NSA_GROUPS, n, NSA_HD)
    s = jnp.concatenate([jnp.einsum('btghd,btgnd->bghtn', q, ks), jnp.einsum('btghd,bsgd->bghts', q, k_new)], axis=-1)
    mask_past = jnp.repeat(ok, NSA_BLOCK, axis=-1).transpose(0, 2, 1, 3)[:, :, None]
    mask_new = jnp.broadcast_to(jnp.tril(jnp.ones((t, t), bool)), (b, NSA_GROUPS, 1, t, t))
    p = masked_softmax(s * NSA_SCALE, jnp.concatenate([mask_past, mask_new], axis=-1)).astype(v_new.dtype)
    return (jnp.einsum('bghtn,btgnd->btghd', p[..., :n], vs)
            + jnp.einsum('bghts,bsgd->btghd', p[..., n:], v_new))


def nsa_win_prompt(q, k_w, v_w):
    t = q.shape[1]
    pad = ((0, 0), (NSA_WINDOW, 0), (0, 0), (0, 0))
    kp, vp = jnp.pad(k_w, pad), jnp.pad(v_w, pad)
    span = NSA_WINDOW + Q_BLOCK

    def block(i):
        t0 = i * Q_BLOCK
        qq = lax.dynamic_slice_in_dim(q, t0, Q_BLOCK, axis=1)
        kk = lax.dynamic_slice_in_dim(kp, t0, span, axis=1)
        vv = lax.dynamic_slice_in_dim(vp, t0, span, axis=1)
        qpos = t0 + jnp.arange(Q_BLOCK)
        kpos = t0 - NSA_WINDOW + jnp.arange(span)
        mask = ((kpos[None, :] <= qpos[:, None]) & (kpos[None, :] > qpos[:, None] - NSA_WINDOW)
                & (kpos[None, :] >= 0))
        p = masked_softmax(jnp.einsum('bqghd,bkgd->bghqk', qq, kk) * NSA_SCALE, mask)
        return jnp.einsum('bghqk,bkgd->bqghd', p.astype(vv.dtype), vv)

    return jnp.moveaxis(lax.map(block, jnp.arange(t // Q_BLOCK)), 0, 1).reshape(q.shape)


def nsa_win_sample(q, win, buf):
    t = q.shape[1]
    wb = buf.shape[1]
    kv = jnp.concatenate([buf.astype(win.dtype), win], axis=1)
    qpos = PAST_LEN + jnp.arange(t)
    kpos = PAST_LEN - wb + jnp.arange(wb + t)
    mask = (kpos[None, :] <= qpos[:, None]) & (kpos[None, :] > qpos[:, None] - NSA_WINDOW)
    p = masked_softmax(jnp.einsum('btghd,bsgd->bghts', q, kv[:, :, 0]) * NSA_SCALE, mask)
    o = jnp.einsum('bghts,bsgd->btghd', p.astype(kv.dtype), kv[:, :, 1])
    return o, kv[:, -wb:]


def nsa_merge(gates, o_cmp, o_sel, o_win):
    o = gates[..., 0:1] * o_cmp + gates[..., 1:2] * o_sel + gates[..., 2:3] * o_win
    return o.reshape(o.shape[0], o.shape[1], NSA_HEADS * NSA_HD)


def _nsa_body(q_ref, rows_ref, win_ref, kc_ref, vc_ref, gates_ref, o_ref, kv_scr, m_scr, acc_scr):
    qi = pl.program_id(1)
    nb = kc_ref.shape[0]
    gw = NSA_GROUPS * NSA_HD

    @pl.when(qi == 0)
    def _():
        lane = lax.broadcasted_iota(jnp.int32, (1, LANES), 1)
        one_col = (lane == NSA_HD).astype(jnp.float32)
        srcs = (rows_ref[:, 2 * gw:3 * gw], rows_ref[:, 3 * gw:4 * gw], win_ref[:, 0:gw], win_ref[:, gw:2 * gw])
        for i, x in enumerate(srcs):
            for g in range(NSA_GROUPS):
                xg = x if g == 0 else pltpu.roll(x, NSA_HD, 1)
                kv_scr[i * NSA_GROUPS + g] = jnp.where(lane < NSA_HD, xg, one_col if i % 2 else 0.0).astype(jnp.bfloat16)

    t0 = qi * NSA_TQ
    tpos = t0 + lax.broadcasted_iota(jnp.int32, (NSA_TQ, 1), 0)
    blk_t = tpos // NSA_BLOCK
    nidx = lax.broadcasted_iota(jnp.int32, (1, nb), 1)
    rows = NSA_HPG * NSA_TQ
    gq = NSA_HPG * NSA_HD
    zpad = jnp.zeros((NSA_TQ, LANES - NSA_HD), jnp.float32)
    for g in range(NSA_GROUPS):
        lanes = slice(g * NSA_HD, (g + 1) * NSA_HD)
        qg = q_ref[:, g * gq:(g + 1) * gq] * NSA_SCALE
        q_all = jnp.concatenate([jnp.concatenate([qg[:, h * NSA_HD:(h + 1) * NSA_HD], zpad], axis=1)
                                 for h in range(NSA_HPG)], axis=0).astype(jnp.bfloat16)

        kc = jnp.concatenate([kc_ref[:, lanes], jnp.zeros((nb, LANES - NSA_HD), jnp.float32)],
                             axis=1).astype(jnp.bfloat16)
        vc = vc_ref[:, lanes].astype(jnp.bfloat16)
        cmask = ((nidx + 1) * NSA_BLOCK - 1) <= tpos
        s = jnp.where(cmask[None], _dot_nt(q_all, kc).reshape(NSA_HPG, NSA_TQ, nb), NEG)
        e = jnp.exp(s - jnp.max(s, axis=-1, keepdims=True))
        p = e / jnp.sum(e, axis=-1, keepdims=True) * cmask[None].astype(jnp.float32)
        imp = jnp.sum(p, axis=0)
        o_cmp = jnp.dot(p.reshape(rows, nb).astype(jnp.bfloat16), vc, preferred_element_type=jnp.float32)

        rank = jnp.zeros((NSA_TQ, nb), jnp.int32)
        for m in range(nb):
            col = imp[:, m:m + 1]
            beats = (col > imp) | ((col == imp) & (m < nidx))
            rank = rank + jnp.where(beats & (m < blk_t), 1, 0)
        sel_f = (((nidx < blk_t) & (rank < NSA_TOPK)) | (nidx == blk_t)).astype(jnp.float32)
        blk_id = lax.broadcasted_iota(jnp.int32, (nb, FLASH_TK), 0)

        def sel_mask(k0):
            spos = k0 + lax.broadcasted_iota(jnp.int32, (1, FLASH_TK), 1)
            expand = (blk_id == spos // NSA_BLOCK).astype(jnp.float32)
            chosen = jnp.dot(sel_f, expand, preferred_element_type=jnp.float32) > 0.5
            return chosen & (spos <= tpos)

        def win_mask(k0):
            spos = k0 + lax.broadcasted_iota(jnp.int32, (1, FLASH_TK), 1)
            return (spos <= tpos) & (spos > tpos - NSA_WINDOW)

        kt_hi = (t0 + NSA_TQ + FLASH_TK - 1) // FLASH_TK
        _flash_init(m_scr, acc_scr)
        _flash_pass(q_all, kv_scr.at[g], kv_scr.at[NSA_GROUPS + g], 0, kt_hi, sel_mask, NSA_HPG, m_scr, acc_scr)
        o_sel = _flash_result(acc_scr, NSA_HD)
        kt_lo = jnp.maximum(t0 - (NSA_WINDOW - 1), 0) // FLASH_TK
        _flash_init(m_scr, acc_scr)
        _flash_pass(q_all, kv_scr.at[2 * NSA_GROUPS + g], kv_scr.at[3 * NSA_GROUPS + g], kt_lo, kt_hi, win_mask,
                    NSA_HPG, m_scr, acc_scr)
        o_win = _flash_result(acc_scr, NSA_HD)

        outs = []
        for h in range(NSA_HPG):
            r = slice(h * NSA_TQ, (h + 1) * NSA_TQ)
            c = (g * NSA_HPG + h) * 3
            outs.append(gates_ref[:, c:c + 1] * o_cmp[r] + gates_ref[:, c + 1:c + 2] * o_sel[r]
                        + gates_ref[:, c + 2:c + 3] * o_win[r])
        o_ref[:, g * gq:(g + 1) * gq] = jnp.concatenate(outs, axis=1)


def nsa_prompt_attn(q, rows, win, k_c, v_c, gates):
    b, t, _ = q.shape
    nb = t // NSA_BLOCK
    assert t % FLASH_TK == 0 and t % NSA_TQ == 0
    gw = NSA_GROUPS * NSA_HD
    qw = NSA_HEADS * NSA_HD
    return pl.pallas_call(
        _nsa_body,
        out_shape=jax.ShapeDtypeStruct((b, t, qw), jnp.float32),
        grid=(b, t // NSA_TQ),
        in_specs=[pl.BlockSpec((None, NSA_TQ, qw), lambda i, j: (i, j, 0)),
                  pl.BlockSpec((None, t, 4 * gw), lambda i, j: (i, 0, 0)),
                  pl.BlockSpec((None, t, 2 * gw), lambda i, j: (i, 0, 0)),
                  pl.BlockSpec((None, nb, gw), lambda i, j: (i, 0, 0)),
                  pl.BlockSpec((None, nb, gw), lambda i, j: (i, 0, 0)),
                  pl.BlockSpec((None, NSA_TQ, NSA_HEADS * 3), lambda i, j: (i, j, 0))],
        out_specs=pl.BlockSpec((None, NSA_TQ, qw), lambda i, j: (i, j, 0)),
        scratch_shapes=[pltpu.VMEM((4 * NSA_GROUPS, t, LANES), jnp.bfloat16),
                        pltpu.VMEM((NSA_HPG * NSA_TQ, LANES), jnp.float32),
                        pltpu.VMEM((NSA_HPG * NSA_TQ, LANES), jnp.float32)],
        compiler_params=pltpu.CompilerParams(dimension_semantics=("arbitrary", "arbitrary"),
                                             vmem_limit_bytes=VMEM_LIMIT_BYTES),
        name="nsa_prompt_attn",
    )(q, rows, win, k_c, v_c, gates)


def nsa_prompt(zp, b_gate, g_qk, w_cmp):
    b, t, _ = zp.shape
    q, rows, win, gates = nsa_project(zp, b_gate, g_qk)
    k_c, v_c = nsa_compress(rows[:, :, :2], w_cmp, g_qk[1])
    nb = t // NSA_BLOCK
    o = nsa_prompt_attn(q.reshape(b, t, -1), rows.reshape(b, t, -1), win.reshape(b, t, -1),
                        k_c.reshape(b, nb, -1), v_c.reshape(b, nb, -1), gates.reshape(b, t, -1))
    return o, rows, win[:, -min(NSA_WINDOW, t):]


def _masked_softmax_rows(s, mask):
    sm = jnp.where(mask, s, NEG)
    e = jnp.exp(sm - jnp.max(sm, axis=-1, keepdims=True))
    return e / jnp.sum(e, axis=-1, keepdims=True) * mask.astype(jnp.float32)


def _nsa_sample_body(pt_ref, *refs):
    n_pages = len(refs) - 15
    pages = refs[:n_pages]
    (q_ref, knew_ref, vnew_ref, win_ref, kwnew_ref, vwnew_ref, gates_ref, wcmp_ref, gk_ref,
     o_ref, cmp_scr, kv_scr, sel_scr, ocmp_scr, s_scr) = refs[n_pages:]
    ph = pl.program_id(1)
    gd = NSA_GROUPS * NSA_HD
    half = 2 * gd
    rows = DEC_SEQ * NSA_HPG
    n_past = n_pages * PAGE_SIZE
    trow = lax.broadcasted_iota(jnp.int32, (rows, 1), 0) // NSA_HPG
    lane = lax.broadcasted_iota(jnp.int32, (1, LANES), 1)
    own_mask = (lane <= trow) & (lane < DEC_SEQ)

    @pl.when(ph == 0)
    def _():
        for p in range(n_pages):
            cmp_scr[p * half:(p + 1) * half, :] = pages[p][...]
        for c in range(2):
            acc = jnp.zeros((NSA_GROUPS * n_pages, LANES), jnp.float32)
            for d in range(NSA_HD):
                lhs = jnp.concatenate([cmp_scr[pl.ds((c * NSA_GROUPS + g) * NSA_HD + d, n_pages, stride=half), :]
                                       for g in range(NSA_GROUPS)], axis=0)
                acc = acc + jnp.dot(lhs.astype(jnp.bfloat16), wcmp_ref[c, d], preferred_element_type=jnp.float32)
            kv_scr[c] = acc
        blk_t = (PAST_LEN + lax.broadcasted_iota(jnp.int32, (DEC_SEQ, 1), 0)) // NSA_BLOCK
        blk = 2 * (lane % n_pages) + lane // n_pages
        cmask = ((blk + 1) * NSA_BLOCK - 1) <= (PAST_LEN + trow)
        for g in range(NSA_GROUPS):
            qg = (q_ref[g] * NSA_SCALE).astype(jnp.bfloat16)
            kc = kv_scr[0, g * n_pages:(g + 1) * n_pages, :]
            vc = kv_scr[1, g * n_pages:(g + 1) * n_pages, :]
            s_parts = []
            for n in range(2):
                x = kc[:, n * NSA_HD:(n + 1) * NSA_HD]
                kn = x * lax.rsqrt(jnp.mean(x * x, axis=-1, keepdims=True) + RMS_EPS) * gk_ref[...]
                s_parts.append(_dot_nt(qg, kn.astype(jnp.bfloat16)))
            p = _masked_softmax_rows(jnp.concatenate(s_parts, axis=1), cmask)
            imp = jnp.sum(p.reshape(DEC_SEQ, NSA_HPG, LANES), axis=1)
            pb = p.astype(jnp.bfloat16)
            ocmp_scr[g] = (
                jnp.dot(pb[:, :n_pages], vc[:, :NSA_HD].astype(jnp.bfloat16), preferred_element_type=jnp.float32)
                + jnp.dot(pb[:, n_pages:], vc[:, NSA_HD:].astype(jnp.bfloat16), preferred_element_type=jnp.float32))
            rank = jnp.zeros((DEC_SEQ, LANES), jnp.int32)
            for m in range(2 * n_pages):
                bm = 2 * (m % n_pages) + m // n_pages
                col = imp[:, m:m + 1]
                beats = (col > imp) | ((col == imp) & (bm < blk))
                rank = rank + jnp.where(beats & (bm < blk_t), 1, 0)
            sel_scr[g] = ((blk < blk_t) & (rank < NSA_TOPK)).astype(jnp.float32)

    @pl.when(ph == 1)
    def _():
        span = NSA_SEL_PAGES * PAGE_SIZE
        for g in range(NSA_GROUPS):
            qg = (q_ref[g] * NSA_SCALE).astype(jnp.bfloat16)
            sel_rows = jnp.broadcast_to(sel_scr[g][:, None, :], (DEC_SEQ, NSA_HPG, LANES)).reshape(rows, LANES)
            krows = slice(g * NSA_HD, (g + 1) * NSA_HD)
            vrows = slice(gd + g * NSA_HD, gd + (g + 1) * NSA_HD)
            for ch in range(n_pages // NSA_SEL_PAGES):
                ps = range(ch * NSA_SEL_PAGES, (ch + 1) * NSA_SEL_PAGES)
                k_t = jnp.concatenate([pages[p][krows, :] for p in ps], axis=1).astype(jnp.bfloat16)
                s = jnp.dot(qg, k_t, preferred_element_type=jnp.float32)
                chosen = jnp.concatenate(
                    [jnp.where(lane < NSA_BLOCK, sel_rows[:, p:p + 1], sel_rows[:, n_pages + p:n_pages + p + 1])
                     for p in ps], axis=1) > 0.5
                s_scr[:, ch * span:(ch + 1) * span] = jnp.where(chosen, s, NEG)
            s_new = jnp.dot(qg, knew_ref[g].astype(jnp.bfloat16), preferred_element_type=jnp.float32)
            s_scr[:, n_past:] = jnp.where(own_mask, s_new, NEG)
            sa = s_scr[...]
            e = jnp.exp(sa - jnp.max(sa, axis=-1, keepdims=True))
            pb = (e / jnp.sum(e, axis=-1, keepdims=True)).astype(jnp.bfloat16)
            o_sel = jnp.dot(pb[:, n_past:], vnew_ref[g].astype(jnp.bfloat16), preferred_element_type=jnp.float32)
            for ch in range(n_pages // NSA_SEL_PAGES):
                ps = range(ch * NSA_SEL_PAGES, (ch + 1) * NSA_SEL_PAGES)
                v_t = jnp.concatenate([pages[p][vrows, :] for p in ps], axis=1).astype(jnp.bfloat16)
                o_sel = o_sel + _dot_nt(pb[:, ch * span:(ch + 1) * span], v_t)

            wb = win_ref.shape[1]
            wpos = PAST_LEN - wb + lax.broadcasted_iota(jnp.int32, (1, wb), 1)
            sw = jnp.dot(qg, win_ref[krows, :].astype(jnp.bfloat16), preferred_element_type=jnp.float32)
            sw_new = jnp.dot(qg, kwnew_ref[g].astype(jnp.bfloat16), preferred_element_type=jnp.float32)
            wmask = jnp.concatenate([wpos > (PAST_LEN + trow - NSA_WINDOW), jnp.broadcast_to(own_mask, (rows, LANES))],
                                    axis=1)
            pw = _masked_softmax_rows(jnp.concatenate([sw, sw_new], axis=1), wmask).astype(jnp.bfloat16)
            o_win = (_dot_nt(pw[:, :wb], win_ref[vrows, :].astype(jnp.bfloat16))
                     + jnp.dot(pw[:, wb:], vwnew_ref[g].astype(jnp.bfloat16), preferred_element_type=jnp.float32))
            gt = gates_ref[g]
            o_ref[g] = gt[:, 0:1] * ocmp_scr[g] + gt[:, 1:2] * o_sel + gt[:, 2:3] * o_win


def nsa_sample_attn(pool_t, li, page_table, q, k_new, v_new, win_t, kw_new, vw_new, gates, w_cmp_t, g_k):
    b, n_pages = page_table.shape
    assert n_pages % NSA_SEL_PAGES == 0 and 2 * n_pages == LANES and PAGE_SIZE == 2 * NSA_BLOCK
    gd = NSA_GROUPS * NSA_HD
    rows = DEC_SEQ * NSA_HPG
    wb = win_t.shape[-1]

    def page_spec(p):
        return pl.BlockSpec((None, None, 2 * gd, PAGE_SIZE), lambda i, ph, pt: (li, pt[i, p], ph, 0))

    def per_seq(*shape):
        return pl.BlockSpec((None,) + shape, lambda i, ph, pt: (i,) + (0,) * len(shape))

    grid_spec = pltpu.PrefetchScalarGridSpec(
        num_scalar_prefetch=1,
        grid=(b, 2),
        in_specs=[page_spec(p) for p in range(n_pages)] + [
            per_seq(NSA_GROUPS, rows, NSA_HD), per_seq(NSA_GROUPS, NSA_HD, LANES), per_seq(NSA_GROUPS, LANES, NSA_HD),
            per_seq(2 * gd, wb), per_seq(NSA_GROUPS, NSA_HD, LANES), per_seq(NSA_GROUPS, LANES, NSA_HD),
            per_seq(NSA_GROUPS, rows, 3),
            pl.BlockSpec((2, NSA_HD, LANES, LANES), lambda i, ph, pt: (0, 0, 0, 0)),
            pl.BlockSpec((1, NSA_HD), lambda i, ph, pt: (0, 0))],
        out_specs=per_seq(NSA_GROUPS, rows, NSA_HD),
        scratch_shapes=[pltpu.VMEM((n_pages * 2 * gd, PAGE_SIZE), jnp.float32),
                        pltpu.VMEM((2, NSA_GROUPS * n_pages, LANES), jnp.float32),
                        pltpu.VMEM((NSA_GROUPS, DEC_SEQ, LANES), jnp.float32),
                        pltpu.VMEM((NSA_GROUPS, rows, NSA_HD), jnp.float32),
                        pltpu.VMEM((rows, n_pages * PAGE_SIZE + LANES), jnp.float32)])
    return pl.pallas_call(
        _nsa_sample_body,
        out_shape=jax.ShapeDtypeStruct((b, NSA_GROUPS, rows, NSA_HD), jnp.float32),
        grid_spec=grid_spec,
        compiler_params=pltpu.CompilerParams(dimension_semantics=("arbitrary", "arbitrary"),
                                             vmem_limit_bytes=VMEM_LIMIT_BYTES),
        name="nsa_sample_attn",
    )(page_table, *([pool_t] * n_pages), q, k_new, v_new, win_t, kw_new, vw_new, gates, w_cmp_t, g_k)


def nsa_cmp_block_weights(w_cmp):
    w = w_cmp.transpose(0, 2, 1, 3)
    z = jnp.zeros_like(w)
    return jnp.concatenate([jnp.concatenate([w, z], axis=-1), jnp.concatenate([z, w], axis=-1)],
                           axis=-2).astype(jnp.bfloat16)


def nsa_sample(zp, pool, li, page_table, win_buf, b_gate, g_qk, w_cmp):
    b, t, _ = zp.shape
    q, rows, win, gates = nsa_project(zp, b_gate, g_qk)
    pool_t = pool.transpose(0, 1, 3, 4, 5, 2).reshape(pool.shape[0], pool.shape[1], 4 * NSA_GROUPS * NSA_HD, PAGE_SIZE)
    win_t = win_buf.transpose(0, 2, 3, 4, 1).reshape(b, 2 * NSA_GROUPS * NSA_HD, win_buf.shape[1])

    def keys_t(x):
        return jnp.pad(x.transpose(0, 2, 3, 1), ((0, 0), (0, 0), (0, 0), (0, LANES - t)))

    def vals(x):
        return jnp.pad(x.transpose(0, 2, 1, 3), ((0, 0), (0, 0), (0, LANES - t), (0, 0)))

    o = nsa_sample_attn(pool_t, li, page_table,
                        q.transpose(0, 2, 1, 3, 4).reshape(b, NSA_GROUPS, t * NSA_HPG, NSA_HD),
                        keys_t(rows[:, :, 2]), vals(rows[:, :, 3]), win_t, keys_t(win[:, :, 0]), vals(win[:, :, 1]),
                        gates.transpose(0, 2, 1, 3, 4).reshape(b, NSA_GROUPS, t * NSA_HPG, 3),
                        nsa_cmp_block_weights(w_cmp), g_qk[1].reshape(1, NSA_HD))
    o = o.reshape(b, NSA_GROUPS, t, NSA_HPG, NSA_HD).transpose(0, 2, 1, 3, 4).reshape(b, t, NSA_HEADS * NSA_HD)
    new_buf = jnp.concatenate([win_buf.astype(win.dtype), win], axis=1)[:, -win_buf.shape[1]:]
    return o, rows, new_buf


def kernel(x_prompt, x_sample, state_mlstm_C, state_mlstm_n, state_mlstm_m, cache_mla, cache_nsa,
           state_nsa_win, state_gdn_S, state_gdn_conv, page_table, c_prompt, c_sample,
           w_ada, b_ada, g_norm, w_ffn_up, w_ffn_down,
           w_ml_in, b_ml_gate, g_ml_hn, w_ml_out,
           w_mla_in, g_mla_qa, g_mla_kva, w_mla_uq, w_mla_uk, w_mla_uv, g_mla_qk, w_mla_out,
           w_nsa_in, b_nsa_gate, g_nsa_qk, w_nsa_cmp, w_nsa_out,
           w_gdn_in, w_gdn_conv, a_gdn_log, b_gdn_dt, g_gdn_out, w_gdn_out):
    bp = x_prompt.shape[0]
    xp, xs = x_prompt, x_sample
    ml_p, ml_s, mla_p, mla_s, nsa_p, nsa_s, gdn_p, gdn_s = [], [], [], [], [], [], [], []
    w_up_bf = w_ffn_up.astype(jnp.bfloat16)
    w_dn_bf = w_ffn_down.astype(jnp.bfloat16)
    for i in range(DEPTH):
        kind, li = i % N_MIXERS, i // N_MIXERS
        mod_p = adaln(c_prompt, w_ada[i], b_ada[i])
        mod_s = adaln(c_sample, w_ada[i], b_ada[i])
        xp = ffn_half_prompt(xp, g_norm[i, 0], mod_p, 0, w_up_bf[i, 0], w_dn_bf[i, 0])
        xs = ffn_half_sample(xs, g_norm[i, 0], mod_s, 0, w_up_bf[i, 0], w_dn_bf[i, 0])
        w_in, w_out = ((w_ml_in, w_ml_out), (w_mla_in, w_mla_out), (w_nsa_in, w_nsa_out), (w_gdn_in, w_gdn_out))[kind]
        zp = mod_proj(xp, g_norm[i, 1], mod_p, 1, w_in[li])
        zs = mod_proj(xs, g_norm[i, 1], mod_s, 1, w_in[li])
        if kind == 0:
            wts = (b_ml_gate[li], g_ml_hn[li])
            op, c_p, n_p, m_p = mlstm_mixer(zp, *wts, jnp.zeros((bp, ML_HEADS, ML_DQK, ML_DV), jnp.float32),
                                            jnp.zeros((bp, ML_HEADS, ML_DQK), jnp.float32),
                                            jnp.zeros((bp, ML_HEADS), jnp.float32))
            os_, c_s, n_s, m_s = mlstm_mixer(zs, *wts, state_mlstm_C[li], state_mlstm_n[li], state_mlstm_m[li])
            ml_p.append((c_p, n_p, m_p))
            ml_s.append((c_s, n_s, m_s))
        elif kind == 1:
            wts = (g_mla_qa[li], g_mla_kva[li], w_mla_uq[li], w_mla_uk[li], w_mla_uv[li], g_mla_qk[li])
            op, r_p = mla_prompt(zp, *wts)
            os_, r_s = mla_sample(zs, cache_mla, li, page_table, *wts)
            mla_p.append((r_p,))
            mla_s.append((r_s,))
        elif kind == 2:
            wts = (b_nsa_gate[li], g_nsa_qk[li], w_nsa_cmp[li])
            op, r_p, wb_p = nsa_prompt(zp, *wts)
            os_, r_s, wb_s = nsa_sample(zs, cache_nsa, li, page_table, state_nsa_win[li], *wts)
            nsa_p.append((r_p, wb_p))
            nsa_s.append((r_s, wb_s))
        else:
            wts = (w_gdn_conv[li], a_gdn_log[li], b_gdn_dt[li], g_gdn_out[li])
            op, s_p, cv_p = gdn_mixer(zp, *wts, jnp.zeros((bp, GDN_HEADS, GDN_DK, GDN_DV), jnp.float32),
                                      jnp.zeros((bp, GDN_CONV - 1, GDN_CONV_DIM), jnp.float32))
            os_, s_s, cv_s = gdn_mixer(zs, *wts, state_gdn_S[li], state_gdn_conv[li])
            gdn_p.append((s_p, cv_p))
            gdn_s.append((s_s, cv_s))
        xp = proj_residual(xp, op, mod_p[:, 1, 2], w_out[li])
        xs = proj_residual(xs, os_, mod_s[:, 1, 2], w_out[li])
        xp = ffn_half_prompt(xp, g_norm[i, 2], mod_p, 2, w_up_bf[i, 1], w_dn_bf[i, 1])
        xs = ffn_half_sample(xs, g_norm[i, 2], mod_s, 2, w_up_bf[i, 1], w_dn_bf[i, 1])
    return (xp, xs,
            _stack(ml_p, 0), _stack(ml_s, 0), _stack(ml_p, 1), _stack(ml_s, 1), _stack(ml_p, 2), _stack(ml_s, 2),
            _stack(mla_p, 0), _stack(mla_s, 0),
            _stack(nsa_p, 0), _stack(nsa_s, 0), _stack(nsa_p, 1), _stack(nsa_s, 1),
            _stack(gdn_p, 0), _stack(gdn_s, 0), _stack(gdn_p, 1), _stack(gdn_s, 1))
```

```python
import functools

import jax
import jax.numpy as jnp
from jax import lax
from jax.experimental import pallas as pl
from jax.experimental.pallas import tpu as pltpu

D_MODEL = 1024
BATCH = 16
SEQ = 2048
DEPTH = 4
DEC_BATCH = 128
DEC_SEQ = 4
PAST_LEN = 8192
PAGE_SIZE = 128

N_MIXERS = 4
D_FF = 2816
RMS_EPS = 1e-6
NEG = -1e30
Q_BLOCK = 128

ML_HEADS = 8
ML_DQK = 64
ML_DV = 128
ML_CHUNK = 64
ML_GATE_CAP = 15.0

MLA_HEADS = 16
MLA_NOPE = 64
MLA_ROPE = 32
MLA_VD = 64
MLA_KV_LORA = 256
MLA_Q_LORA = 384
MLA_LAT = MLA_KV_LORA + MLA_ROPE
MLA_SCALE = (MLA_NOPE + MLA_ROPE) ** -0.5
ROPE_THETA = 10000.0

NSA_HEADS = 16
NSA_GROUPS = 2
NSA_HPG = NSA_HEADS // NSA_GROUPS
NSA_HD = 64
NSA_BLOCK = 64
NSA_TOPK = 15
NSA_WINDOW = 512
NSA_QB = 32
NSA_SCALE = NSA_HD ** -0.5

GDN_HEADS = 8
GDN_DK = 128
GDN_DV = 128
GDN_CONV = 4
GDN_CHUNK = 64
GDN_CONV_DIM = GDN_HEADS * (2 * GDN_DK + GDN_DV)

VMEM_LIMIT_BYTES = 56 * 1024 * 1024
FFN_ROW_TILE = 512
FFN_COL_CHUNK = 256
PROJ_COL_CHUNK = 256
GDN_HEADS_PER_STEP = 8
GDN_INV_BLOCK = 4
LANES = 128
FLASH_TK = 256
NSA_TQ = 128
NSA_SEL_PAGES = 8
MLA_TQ = 512
MLA_PAGES_PER_STEP = 8
MLA_NEW_ROWS = 256
_HI = lax.Precision.HIGHEST


def _ffn_body(x_ref, shift_ref, scale_ref, gate_ref, g_ref, wup_ref, wdn_ref, o_ref, acc_ref):
    x = x_ref[...]
    y = x * lax.rsqrt(jnp.mean(x * x, axis=-1, keepdims=True) + RMS_EPS) * g_ref[...]
    h = (y * (1.0 + scale_ref[...]) + shift_ref[...]).astype(jnp.bfloat16)
    for j in range(D_FF // FFN_COL_CHUNK):
        lo = j * FFN_COL_CHUNK
        a = jnp.dot(h, wup_ref[:, lo:lo + FFN_COL_CHUNK], preferred_element_type=jnp.float32)
        b = jnp.dot(h, wup_ref[:, D_FF + lo:D_FF + lo + FFN_COL_CHUNK], preferred_element_type=jnp.float32)
        act = (a * jax.nn.sigmoid(a) * b).astype(jnp.bfloat16)
        part = jnp.dot(act, wdn_ref[lo:lo + FFN_COL_CHUNK, :], preferred_element_type=jnp.float32)
        if j == 0:
            acc_ref[...] = part
        else:
            acc_ref[...] += part
    o_ref[...] = x + 0.5 * gate_ref[...] * acc_ref[...]


def _ffn_call(x2d, shift, scale, gate, g, w_up, w_down, rows_per_mod):
    n = x2d.shape[0]
    tm = min(FFN_ROW_TILE, n)
    assert n % tm == 0
    if rows_per_mod:
        assert rows_per_mod % tm == 0
        tiles_per_mod = rows_per_mod // tm
        mod_spec = pl.BlockSpec((None, 1, D_MODEL), lambda i: (i // tiles_per_mod, 0, 0))
    else:
        mod_spec = pl.BlockSpec((tm, D_MODEL), lambda i: (i, 0))
    resident = functools.partial(pl.BlockSpec, pipeline_mode=pl.Buffered(1))
    return pl.pallas_call(
        _ffn_body,
        out_shape=jax.ShapeDtypeStruct((n, D_MODEL), jnp.float32),
        grid=(n // tm,),
        in_specs=[
            pl.BlockSpec((tm, D_MODEL), lambda i: (i, 0)),
            mod_spec, mod_spec, mod_spec,
            resident((1, D_MODEL), lambda i: (0, 0)),
            resident((D_MODEL, 2 * D_FF), lambda i: (0, 0)),
            resident((D_FF, D_MODEL), lambda i: (0, 0)),
        ],
        out_specs=pl.BlockSpec((tm, D_MODEL), lambda i: (i, 0)),
        scratch_shapes=[pltpu.VMEM((tm, D_MODEL), jnp.float32)],
        compiler_params=pltpu.CompilerParams(dimension_semantics=("arbitrary",),
                                             vmem_limit_bytes=VMEM_LIMIT_BYTES),
        name="ffn_half",
    )(x2d, shift, scale, gate, g.reshape(1, D_MODEL), w_up, w_down)


def ffn_half_prompt(x, g, mod, j, w_up, w_down):
    b, t, d = x.shape
    m = [mod[:, j, k][:, None, :] for k in range(3)]
    return _ffn_call(x.reshape(b * t, d), m[0], m[1], m[2], g, w_up, w_down, t).reshape(b, t, d)


def ffn_half_sample(x, g, mod, j, w_up, w_down):
    b, t, d = x.shape
    m = [jnp.repeat(mod[:, j, k], t, axis=0) for k in range(3)]
    return _ffn_call(x.reshape(b * t, d), m[0], m[1], m[2], g, w_up, w_down, 0).reshape(b, t, d)


def _mod_proj_body(x_ref, shift_ref, scale_ref, g_ref, w_ref, o_ref):
    x = x_ref[...]
    y = x * lax.rsqrt(jnp.mean(x * x, axis=-1, keepdims=True) + RMS_EPS) * g_ref[...]
    h = (y * (1.0 + scale_ref[...]) + shift_ref[...]).astype(jnp.bfloat16)
    n_out = o_ref.shape[1]
    for j in range(w_ref.shape[1] // PROJ_COL_CHUNK):
        lo = j * PROJ_COL_CHUNK
        part = jnp.dot(h, w_ref[:, lo:lo + PROJ_COL_CHUNK], preferred_element_type=jnp.float32)
        width = min(PROJ_COL_CHUNK, n_out - lo)
        o_ref[:, lo:lo + width] = part[:, :width]


def _row_mod_specs(n, tm, rows_per_mod):
    if rows_per_mod:
        assert rows_per_mod % tm == 0
        tiles_per_mod = rows_per_mod // tm
        return pl.BlockSpec((None, 1, D_MODEL), lambda i: (i // tiles_per_mod, 0, 0))
    return pl.BlockSpec((tm, D_MODEL), lambda i: (i, 0))


def mod_proj(x, g, mod, j, w):
    b, t, d = x.shape
    n_out = w.shape[1]
    n_pad = -(-n_out // PROJ_COL_CHUNK) * PROJ_COL_CHUNK
    wp = jnp.pad(w.astype(jnp.bfloat16), ((0, 0), (0, n_pad - n_out)))
    n = b * t
    tm = min(FFN_ROW_TILE, n)
    assert n % tm == 0
    if t % tm == 0:
        shift, scale = (mod[:, j, k][:, None, :] for k in range(2))
        mod_spec = _row_mod_specs(n, tm, t)
    else:
        shift, scale = (jnp.repeat(mod[:, j, k], t, axis=0) for k in range(2))
        mod_spec = _row_mod_specs(n, tm, 0)
    resident = functools.partial(pl.BlockSpec, pipeline_mode=pl.Buffered(1))
    out = pl.pallas_call(
        _mod_proj_body,
        out_shape=jax.ShapeDtypeStruct((n, n_out), jnp.float32),
        grid=(n // tm,),
        in_specs=[pl.BlockSpec((tm, D_MODEL), lambda i: (i, 0)), mod_spec, mod_spec,
                  resident((1, D_MODEL), lambda i: (0, 0)), resident((D_MODEL, n_pad), lambda i: (0, 0))],
        out_specs=pl.BlockSpec((tm, n_out), lambda i: (i, 0)),
        compiler_params=pltpu.CompilerParams(dimension_semantics=("arbitrary",),
                                             vmem_limit_bytes=VMEM_LIMIT_BYTES),
        name="mod_proj",
    )(x.reshape(n, d), shift, scale, g.reshape(1, D_MODEL), wp)
    return out.reshape(b, t, n_out)


def _proj_residual_body(o_ref, x_ref, gate_ref, w_ref, y_ref):
    y_ref[...] = x_ref[...] + gate_ref[...] * jnp.dot(o_ref[...].astype(jnp.bfloat16), w_ref[...],
                                                      preferred_element_type=jnp.float32)


def proj_residual(x, o, gate, w):
    b, t, d = x.shape
    kdim = o.shape[-1]
    n = b * t
    tm = min(FFN_ROW_TILE, n)
    assert n % tm == 0
    if t % tm == 0:
        gate_rows, gate_spec = gate[:, None, :], _row_mod_specs(n, tm, t)
    else:
        gate_rows, gate_spec = jnp.repeat(gate, t, axis=0), _row_mod_specs(n, tm, 0)
    return pl.pallas_call(
        _proj_residual_body,
        out_shape=jax.ShapeDtypeStruct((n, d), jnp.float32),
        grid=(n // tm,),
        in_specs=[pl.BlockSpec((tm, kdim), lambda i: (i, 0)), pl.BlockSpec((tm, d), lambda i: (i, 0)), gate_spec,
                  pl.BlockSpec((kdim, d), lambda i: (0, 0), pipeline_mode=pl.Buffered(1))],
        out_specs=pl.BlockSpec((tm, d), lambda i: (i, 0)),
        compiler_params=pltpu.CompilerParams(dimension_semantics=("arbitrary",),
                                             vmem_limit_bytes=VMEM_LIMIT_BYTES),
        name="proj_residual",
    )(o.reshape(n, kdim), x.reshape(n, d), gate_rows, w.astype(jnp.bfloat16)).reshape(b, t, d)


def rmsnorm(x, g):
    xf = x.astype(jnp.float32)
    y = xf * lax.rsqrt(jnp.mean(xf * xf, axis=-1, keepdims=True) + RMS_EPS)
    return (y * g.astype(jnp.float32)).astype(x.dtype)


def l2norm(x):
    xf = x.astype(jnp.float32)
    return xf * lax.rsqrt(jnp.sum(xf * xf, axis=-1, keepdims=True) + 1e-6)


def rope(x, pos):
    half = x.shape[-1] // 2
    freqs = ROPE_THETA ** (-jnp.arange(half, dtype=jnp.float32) / half)
    ang = pos[:, None] * freqs[None, :]
    ang = ang.reshape((ang.shape[0],) + (1,) * (x.ndim - 3) + (half,))
    cos, sin = jnp.cos(ang), jnp.sin(ang)
    xf = x.astype(jnp.float32)
    x1, x2 = xf[..., :half], xf[..., half:]
    return jnp.concatenate([x1 * cos - x2 * sin, x2 * cos + x1 * sin], axis=-1).astype(x.dtype)


def masked_softmax(s, mask):
    return jax.nn.softmax(jnp.where(mask, s.astype(jnp.float32), NEG), axis=-1) * mask


def adaln(c, w, b):
    return (jax.nn.silu(c) @ w + b).reshape(c.shape[0], 3, 3, c.shape[1])


def modulate(x, g, mod, j):
    return rmsnorm(x, g) * (1.0 + mod[:, j, 1][:, None]) + mod[:, j, 0][:, None]


def _chunks(a, chunk):
    b, h, t = a.shape[:3]
    return jnp.moveaxis(a.reshape((b, h, t // chunk, chunk) + a.shape[3:]), 2, 0)


def _unchunk(a):
    a = jnp.moveaxis(a, 0, 2)
    return a.reshape(a.shape[:2] + (a.shape[2] * a.shape[3],) + a.shape[4:])


def _stack(entries, j):
    return jnp.stack([e[j] for e in entries])


def _mlstm_body(q_ref, k_ref, v_ref, icol_ref, irow_ref, fcol_ref, frow_ref, ghn_ref, c0_ref, m0_ref,
                o_ref, c_out_ref, m_out_ref, c_scr, m_scr):
    c = pl.program_id(1)
    L = ML_CHUNK

    @pl.when(c == 0)
    def _():
        c_scr[...] = c0_ref[...]
        m_scr[...] = m0_ref[...]

    row = lax.broadcasted_iota(jnp.int32, (L, L), 0)
    col = lax.broadcasted_iota(jnp.int32, (L, L), 1)
    causal = col <= row
    b_cols = jnp.dot(causal.astype(jnp.float32), fcol_ref[...], precision=_HI, preferred_element_type=jnp.float32)
    b_rows = jnp.dot(frow_ref[...], (col >= row).astype(jnp.float32), precision=_HI,
                     preferred_element_type=jnp.float32)

    def per_head(f):
        return jnp.stack([f(h) for h in range(ML_HEADS)], axis=0)

    q = per_head(lambda h: q_ref[:, h * ML_DQK:(h + 1) * ML_DQK]) * ML_DQK ** -0.5
    k = per_head(lambda h: k_ref[:, h * ML_DQK:(h + 1) * ML_DQK])
    one_col = (lax.broadcasted_iota(jnp.int32, (L, LANES), 1) == 0).astype(jnp.float32)
    v1 = per_head(lambda h: jnp.concatenate([v_ref[:, h * ML_DV:(h + 1) * ML_DV], one_col], axis=1)).astype(jnp.bfloat16)
    bc = per_head(lambda h: b_cols[:, h:h + 1])
    br = per_head(lambda h: b_rows[h:h + 1, :])
    ic = per_head(lambda h: icol_ref[:, h:h + 1])
    ir = per_head(lambda h: irow_ref[h:h + 1, :])
    m_old = m_scr[:, 0:1, 0:1]
    dmat = jnp.where(causal[None], bc - br + ir, NEG)
    inter = bc + m_old
    mt = jnp.maximum(inter, jnp.max(dmat, axis=-1, keepdims=True))
    w_inter = jnp.exp(inter - mt)
    qb, kb = q.astype(jnp.bfloat16), k.astype(jnp.bfloat16)
    s = _bdot_nt(qb, kb) * jnp.exp(dmat - mt)
    cx = c_scr[...]
    nd = w_inter * _bdot(qb, cx.astype(jnp.bfloat16)) + _bdot(s.astype(jnp.bfloat16), v1)
    hc = nd[:, :, :ML_DV] / jnp.maximum(jnp.abs(nd[:, :, ML_DV:ML_DV + 1]), jnp.exp(-mt))
    hn = hc * lax.rsqrt(jnp.mean(hc * hc, axis=-1, keepdims=True) + RMS_EPS)
    for h in range(ML_HEADS):
        o_ref[:, h * ML_DV:(h + 1) * ML_DV] = hn[h] * ghn_ref[h:h + 1, :]
    b_last = bc[:, L - 1:L, :]
    m_new = mt[:, L - 1:L, :]
    kw = (jnp.exp(b_last - bc + ic - m_new) * k).astype(jnp.bfloat16)
    c_scr[...] = jnp.exp(b_last + m_old - m_new) * cx + _bdot_tn(kw, v1)
    m_scr[...] = jnp.broadcast_to(m_new, m_scr.shape)

    @pl.when(c == pl.num_programs(1) - 1)
    def _():
        c_out_ref[...] = c_scr[...]
        m_out_ref[...] = m_scr[...]


def mlstm_scan_call(zp, ig, lf, g_hn, c0, n0, m0):
    b, t, _ = zp.shape
    L = ML_CHUNK
    nc = t // L
    qw = ML_HEADS * ML_DQK
    vw = ML_HEADS * ML_DV

    def cols(z):
        return z.reshape(b, nc, L, ML_HEADS)

    def rows(z):
        return z.reshape(b, nc, L, ML_HEADS).swapaxes(2, 3)

    cx0 = jnp.concatenate([c0, n0[..., None], jnp.zeros(c0.shape[:-1] + (LANES - 1,), jnp.float32)], axis=-1)
    mx0 = jnp.broadcast_to(m0[:, :, None, None], (b, ML_HEADS, 8, LANES))
    col_spec = pl.BlockSpec((None, None, L, ML_HEADS), lambda i, c: (i, c, 0, 0))
    row_spec = pl.BlockSpec((None, None, ML_HEADS, L), lambda i, c: (i, c, 0, 0))
    c_spec = pl.BlockSpec((None, ML_HEADS, ML_DQK, ML_DV + LANES), lambda i, c: (i, 0, 0, 0))
    m_spec = pl.BlockSpec((None, ML_HEADS, 8, LANES), lambda i, c: (i, 0, 0, 0))
    o, cx, mx = pl.pallas_call(
        _mlstm_body,
        out_shape=(jax.ShapeDtypeStruct((b, t, vw), jnp.float32),
                   jax.ShapeDtypeStruct((b, ML_HEADS, ML_DQK, ML_DV + LANES), jnp.float32),
                   jax.ShapeDtypeStruct((b, ML_HEADS, 8, LANES), jnp.float32)),
        grid=(b, nc),
        in_specs=[pl.BlockSpec((None, L, qw), lambda i, c: (i, c, 0)),
                  pl.BlockSpec((None, L, qw), lambda i, c: (i, c, 1)),
                  pl.BlockSpec((None, L, vw), lambda i, c: (i, c, 1)),
                  col_spec, row_spec, col_spec, row_spec,
                  pl.BlockSpec((ML_HEADS, ML_DV), lambda i, c: (0, 0)), c_spec, m_spec],
        out_specs=(pl.BlockSpec((None, L, vw), lambda i, c: (i, c, 0)), c_spec, m_spec),
        scratch_shapes=[pltpu.VMEM((ML_HEADS, ML_DQK, ML_DV + LANES), jnp.float32),
                        pltpu.VMEM((ML_HEADS, 8, LANES), jnp.float32)],
        compiler_params=pltpu.CompilerParams(dimension_semantics=("arbitrary", "arbitrary"),
                                             vmem_limit_bytes=VMEM_LIMIT_BYTES),
        name="mlstm_scan",
    )(zp, zp, zp, cols(ig), rows(ig), cols(lf), rows(lf), g_hn, cx0, mx0)
    return o, cx[..., :ML_DV], cx[..., ML_DV], mx[:, :, 0, 0]


def mlstm_mixer(zp, b_gate, g_hn, C0, n0, m0):
    b, t, _ = zp.shape
    qk_w, v_w = ML_HEADS * ML_DQK, ML_HEADS * ML_DV
    assert 2 * qk_w == v_w
    o = zp[..., 2 * qk_w + v_w:2 * qk_w + 2 * v_w]
    gates = ML_GATE_CAP * jnp.tanh((zp[..., 2 * qk_w + 2 * v_w:] + b_gate).astype(jnp.float32) / ML_GATE_CAP)
    ig = gates[..., :ML_HEADS]
    lf = jax.nn.log_sigmoid(gates[..., ML_HEADS:])
    pad = (-t) % ML_CHUNK
    if pad:
        zp = jnp.pad(zp, ((0, 0), (0, pad), (0, 0)))
        ig = jnp.pad(ig, ((0, 0), (0, pad), (0, 0)), constant_values=NEG)
        lf = jnp.pad(lf, ((0, 0), (0, pad), (0, 0)))
    hh, C, n, m = mlstm_scan_call(zp, ig, lf, g_hn, C0.astype(jnp.float32), n0.astype(jnp.float32),
                                  m0.astype(jnp.float32))
    return jax.nn.sigmoid(o) * hh[:, :t], C, n, m


def _dot_nt(a, b, **kw):
    return lax.dot_general(a, b, (((1,), (1,)), ((), ())), preferred_element_type=jnp.float32, **kw)


def _dot_tn(a, b, **kw):
    return lax.dot_general(a, b, (((0,), (0,)), ((), ())), preferred_element_type=jnp.float32, **kw)


def _bdot(a, b, **kw):
    return lax.dot_general(a, b, (((2,), (1,)), ((0,), (0,))), preferred_element_type=jnp.float32, **kw)


def _bdot_nt(a, b, **kw):
    return lax.dot_general(a, b, (((2,), (2,)), ((0,), (0,))), preferred_element_type=jnp.float32, **kw)


def _bdot_tn(a, b, **kw):
    return lax.dot_general(a, b, (((1,), (1,)), ((0,), (0,))), preferred_element_type=jnp.float32, **kw)


def _flash_init(m_scr, acc_scr):
    m_scr[...] = jnp.full(m_scr.shape, NEG, jnp.float32)
    acc_scr[...] = jnp.zeros(acc_scr.shape, jnp.float32)


def _flash_pass(q_all, k_ref, v_ref, kt_lo, kt_hi, mask_fn, heads, m_scr, acc_scr):
    rows = q_all.shape[0]
    tq = rows // heads

    def body(kt, carry):
        k0 = pl.multiple_of(kt * FLASH_TK, FLASH_TK)
        s = _dot_nt(q_all, k_ref[pl.ds(k0, FLASH_TK), :])
        if mask_fn is not None:
            s = jnp.where(mask_fn(k0)[None], s.reshape(heads, tq, FLASH_TK), 2.0 * NEG).reshape(rows, FLASH_TK)
        m_old = m_scr[...]
        m_new = jnp.maximum(m_old, jnp.max(s, axis=-1, keepdims=True))
        p = jnp.exp(s - jnp.concatenate([m_new] * (FLASH_TK // LANES), axis=1))
        acc_scr[...] = acc_scr[...] * jnp.exp(m_old - m_new) + jnp.dot(
            p.astype(jnp.bfloat16), v_ref[pl.ds(k0, FLASH_TK), :], preferred_element_type=jnp.float32)
        m_scr[...] = m_new
        return carry

    lax.fori_loop(kt_lo, kt_hi, body, 0)


def _flash_result(acc_scr, dv):
    acc = acc_scr[...]
    return acc[:, :dv] / acc[:, dv:dv + 1]


def _gdn_body(q_ref, k_ref, v_ref, beta_ref, gcol_ref, grow_ref, s0_ref, o_ref, s_out_ref, s_scr):
    c = pl.program_id(2)
    L = GDN_CHUNK

    @pl.when(c == 0)
    def _():
        s_scr[...] = s0_ref[...]

    row = lax.broadcasted_iota(jnp.int32, (L, L), 0)
    col = lax.broadcasted_iota(jnp.int32, (L, L), 1)
    causal = col <= row
    strict = col < row
    eye = (col == row).astype(jnp.float32)
    b_cols = jnp.dot(causal.astype(jnp.float32), gcol_ref[...], precision=_HI, preferred_element_type=jnp.float32)
    b_rows = jnp.dot(grow_ref[...], (col >= row).astype(jnp.float32), precision=_HI,
                     preferred_element_type=jnp.float32)
    def per_head(f):
        return jnp.stack([f(h, slice(h * GDN_DK, (h + 1) * GDN_DK)) for h in range(GDN_HEADS_PER_STEP)], axis=0)

    qf = per_head(lambda h, lanes: q_ref[:, lanes])
    kf = per_head(lambda h, lanes: k_ref[:, lanes])
    v = per_head(lambda h, lanes: v_ref[:, lanes])
    beta = per_head(lambda h, lanes: beta_ref[:, h:h + 1])
    bc = per_head(lambda h, lanes: b_cols[:, h:h + 1])
    br = per_head(lambda h, lanes: b_rows[h:h + 1, :])
    q = qf * lax.rsqrt(jnp.sum(qf * qf, axis=-1, keepdims=True) + 1e-6) * GDN_DK ** -0.5
    k = kf * lax.rsqrt(jnp.sum(kf * kf, axis=-1, keepdims=True) + 1e-6)
    decay = jnp.exp(jnp.where(causal[None], bc - br, NEG))
    qb, kb = q.astype(jnp.bfloat16), k.astype(jnp.bfloat16)
    a_mat = jnp.where(strict[None], beta * _bdot_nt(kb, kb) * decay, 0.0)
    blk = GDN_INV_BLOCK
    a_diag = jnp.where((row // blk == col // blk)[None], a_mat, 0.0)
    a_diag2 = _bdot(a_diag, a_diag, precision=_HI)
    inv = eye[None] - a_diag + a_diag2 - _bdot(a_diag, a_diag2, precision=_HI)
    while blk < L:
        join = (row // (2 * blk) == col // (2 * blk)) & (row // blk == col // blk + 1)
        lower = jnp.where(join[None], a_mat, 0.0)
        inv = inv - _bdot(_bdot(inv, lower, precision=_HI), inv, precision=_HI)
        blk *= 2
    s = s_scr[...]
    sb = s.astype(jnp.bfloat16)
    eb = jnp.exp(bc)
    rhs = beta * (v - eb * _bdot(kb, sb))
    ub = _bdot(inv, rhs, precision=_HI).astype(jnp.bfloat16)
    qk = (_bdot_nt(qb, kb) * decay).astype(jnp.bfloat16)
    o = eb * _bdot(qb, sb) + _bdot(qk, ub)
    for h in range(GDN_HEADS_PER_STEP):
        o_ref[:, h * GDN_DV:(h + 1) * GDN_DV] = o[h]
    b_last = bc[:, L - 1:L, :]
    kw = (jnp.exp(b_last - bc) * k).astype(jnp.bfloat16)
    s_scr[...] = jnp.exp(b_last) * s + _bdot_tn(kw, ub)

    @pl.when(c == pl.num_programs(2) - 1)
    def _():
        s_out_ref[...] = s_scr[...]


def gdn_scan_call(qkv, beta, g, s0):
    b, t, _ = qkv.shape
    hb = GDN_HEADS_PER_STEP
    ng = GDN_HEADS // hb
    L = GDN_CHUNK
    w = hb * GDN_DK

    def qkv_spec(part):
        return pl.BlockSpec((None, L, w), lambda i, j, c: (i, c, part * ng + j))

    def per_chunk(z):
        return z.reshape(b, t // L, L, ng, hb).transpose(0, 3, 1, 2, 4)

    col_spec = pl.BlockSpec((None, None, None, L, hb), lambda i, j, c: (i, j, c, 0, 0))
    state_spec = pl.BlockSpec((None, hb, GDN_DK, GDN_DV), lambda i, j, c: (i, j, 0, 0))
    return pl.pallas_call(
        _gdn_body,
        out_shape=(jax.ShapeDtypeStruct((b, t, GDN_HEADS * GDN_DV), jnp.float32),
                   jax.ShapeDtypeStruct((b, GDN_HEADS, GDN_DK, GDN_DV), jnp.float32)),
        grid=(b, ng, t // L),
        in_specs=[qkv_spec(0), qkv_spec(1), qkv_spec(2), col_spec, col_spec,
                  pl.BlockSpec((None, None, None, hb, L), lambda i, j, c: (i, j, c, 0, 0)),
                  state_spec],
        out_specs=(pl.BlockSpec((None, L, w), lambda i, j, c: (i, c, j)), state_spec),
        scratch_shapes=[pltpu.VMEM((hb, GDN_DK, GDN_DV), jnp.float32)],
        compiler_params=pltpu.CompilerParams(dimension_semantics=("arbitrary", "arbitrary", "arbitrary"),
                                             vmem_limit_bytes=VMEM_LIMIT_BYTES),
        name="gdn_scan",
    )(qkv, qkv, qkv, per_chunk(beta), per_chunk(g), per_chunk(g).swapaxes(3, 4), s0)


def gdn_mixer(zp, w_conv, a_log, dt_bias, g_out, S0, conv0):
    b, t, _ = zp.shape
    qkv, beta, a, gate = jnp.split(zp, [GDN_CONV_DIM, GDN_CONV_DIM + GDN_HEADS, GDN_CONV_DIM + 2 * GDN_HEADS], axis=-1)
    xc = jnp.concatenate([conv0.astype(qkv.dtype), qkv], axis=1)
    conv = lax.conv_general_dilated(xc, w_conv[:, None, :].astype(xc.dtype), (1,), 'VALID',
                                    dimension_numbers=('NWC', 'WIO', 'NWC'), feature_group_count=GDN_CONV_DIM)
    act = jax.nn.silu(conv)
    beta = jax.nn.sigmoid(beta.astype(jnp.float32))
    g = -jnp.exp(a_log.astype(jnp.float32)) * jax.nn.softplus((a + dt_bias).astype(jnp.float32))
    pad = (-t) % GDN_CHUNK
    if pad:
        act, beta, g = (jnp.pad(z, ((0, 0), (0, pad), (0, 0))) for z in (act, beta, g))
    o, S = gdn_scan_call(act, beta, g, S0.astype(jnp.float32))
    o = o[:, :t].reshape(b, t, GDN_HEADS, GDN_DV)
    o = rmsnorm(o, g_out).reshape(b, t, GDN_HEADS * GDN_DV)
    return o * jax.nn.silu(gate), S, xc[:, -(GDN_CONV - 1):]


def mla_project(zp, pos, g_qa, g_kva, w_uq, g_qk):
    b, t, _ = zp.shape
    cq, ckv, kr = jnp.split(zp, [MLA_Q_LORA, MLA_Q_LORA + MLA_KV_LORA], axis=-1)
    q = (rmsnorm(cq, g_qa) @ w_uq).reshape(b, t, MLA_HEADS, MLA_NOPE + MLA_ROPE)
    q_nope = rmsnorm(q[..., :MLA_NOPE], g_qk[0, :MLA_NOPE])
    q_rope = rope(rmsnorm(q[..., MLA_NOPE:], g_qk[0, MLA_NOPE:]), pos)
    k_rope = rope(rmsnorm(kr, g_qk[1, MLA_NOPE:]), pos)
    latent = jnp.concatenate([rmsnorm(ckv, g_kva), k_rope], axis=-1)
    return q_nope, q_rope, latent


def mla_scores(q_nope, q_rope, k_nope, k_rope):
    s = jnp.einsum('bthd,bshd->bhts', q_nope, k_nope) + jnp.einsum('bthr,bsr->bhts', q_rope, k_rope)
    return s.astype(jnp.float32) * MLA_SCALE


def mla_keys(latent, w_uk, g_kn):
    ckv = latent[..., :MLA_KV_LORA]
    return ckv, rmsnorm(jnp.einsum('bsc,chd->bshd', ckv, w_uk), g_kn), latent[..., MLA_KV_LORA:]


def _mla_prompt_body(q_ref, k_ref, v_ref, o_ref, m_scr, acc_scr):
    qi = pl.program_id(2)
    t0 = qi * MLA_TQ
    tpos = t0 + lax.broadcasted_iota(jnp.int32, (MLA_TQ, 1), 0)

    def causal(k0):
        return (k0 + lax.broadcasted_iota(jnp.int32, (1, FLASH_TK), 1)) <= tpos

    n_full = t0 // FLASH_TK
    outs = []
    for h in range(2):
        _flash_init(m_scr, acc_scr)
        _flash_pass(q_ref[h], k_ref.at[h], v_ref.at[h], 0, n_full, None, 1, m_scr, acc_scr)
        _flash_pass(q_ref[h], k_ref.at[h], v_ref.at[h], n_full, n_full + MLA_TQ // FLASH_TK, causal, 1,
                    m_scr, acc_scr)
        outs.append(_flash_result(acc_scr, MLA_VD))
    o_ref[...] = jnp.concatenate(outs, axis=1)


def mla_prompt_attn(q, k, v):
    b, h, t, _ = q.shape
    assert t % MLA_TQ == 0 and MLA_TQ % FLASH_TK == 0 and h % 2 == 0
    seq_spec = pl.BlockSpec((None, 2, t, LANES), lambda i, j, k: (i, j, 0, 0))
    return pl.pallas_call(
        _mla_prompt_body,
        out_shape=jax.ShapeDtypeStruct((b, t, h * MLA_VD), jnp.float32),
        grid=(b, h // 2, t // MLA_TQ),
        in_specs=[pl.BlockSpec((None, 2, MLA_TQ, LANES), lambda i, j, k: (i, j, k, 0)), seq_spec, seq_spec],
        out_specs=pl.BlockSpec((None, MLA_TQ, 2 * MLA_VD), lambda i, j, k: (i, k, j)),
        scratch_shapes=[pltpu.VMEM((MLA_TQ, LANES), jnp.float32), pltpu.VMEM((MLA_TQ, LANES), jnp.float32)],
        compiler_params=pltpu.CompilerParams(dimension_semantics=("arbitrary", "arbitrary", "arbitrary"),
                                             vmem_limit_bytes=VMEM_LIMIT_BYTES),
        name="mla_prompt_attn",
    )(q, k, v)


def mla_prompt(zp, g_qa, g_kva, w_uq, w_uk, w_uv, g_qk):
    b, t, _ = zp.shape
    q_nope, q_rope, latent = mla_project(zp, jnp.arange(t, dtype=jnp.float32), g_qa, g_kva, w_uq, g_qk)
    ckv, k_nope, k_rope = mla_keys(latent, w_uk, g_qk[1, :MLA_NOPE])
    v = jnp.einsum('bsc,chd->bshd', ckv, w_uv)

    def heads_first(x):
        return x.transpose(0, 2, 1, 3)

    pad = jnp.zeros((b, MLA_HEADS, t, LANES - MLA_NOPE - MLA_ROPE), jnp.float32)
    q = jnp.concatenate([heads_first(q_nope), heads_first(q_rope), pad], axis=-1) * MLA_SCALE
    k = jnp.concatenate([heads_first(k_nope), jnp.broadcast_to(k_rope[:, None], (b, MLA_HEADS, t, MLA_ROPE)), pad],
                        axis=-1)
    v1 = jnp.concatenate([heads_first(v), jnp.ones((b, MLA_HEADS, t, 1), jnp.float32),
                          jnp.zeros((b, MLA_HEADS, t, LANES - MLA_VD - 1), jnp.float32)], axis=-1)
    return mla_prompt_attn(q.astype(jnp.bfloat16), k.astype(jnp.bfloat16), v1.astype(jnp.bfloat16)), latent


def _mla_page_scores(lat, w_uk_ref, qbd, gexp, qr):
    ckv = lat[:, :MLA_KV_LORA].astype(jnp.bfloat16)
    kn = jnp.dot(ckv, w_uk_ref[...], preferred_element_type=jnp.float32)
    s_n = _dot_nt(kn.astype(jnp.bfloat16), qbd)
    ms = jnp.dot((kn * kn).astype(jnp.bfloat16), gexp, preferred_element_type=jnp.float32)
    s_r = _dot_nt(lat[:, MLA_KV_LORA:].astype(jnp.bfloat16), qr)
    return ckv, (s_n * lax.rsqrt(ms + RMS_EPS) + s_r) * MLA_SCALE


def _mla_page_rows(page_ref):
    x = page_ref[...]
    tail = x[MLA_LAT - LANES:, :].T
    return jnp.concatenate([x[:LANES, :].T, x[LANES:2 * LANES, :].T, tail[:, LANES - MLA_ROPE:]], axis=1)


def _mla_sample_body(pt_ref, *refs):
    pp = MLA_PAGES_PER_STEP
    page_refs = refs[:pp]
    qn_ref, qr_ref, latn_ref, w_uk_ref, w_uv_ref, o_ref, ckv_scr, s_scr, qbd_scr = refs[pp:]
    j = pl.program_id(1)
    nj = pl.num_programs(1)
    nq = DEC_SEQ * MLA_HEADS
    hw = MLA_HEADS * MLA_NOPE
    n = pp * PAGE_SIZE
    n_past = nj * n

    @pl.when(j == 0)
    def _():
        r = lax.broadcasted_iota(jnp.int32, (nq, hw), 0)
        c = lax.broadcasted_iota(jnp.int32, (nq, hw), 1)
        q_rep = jnp.concatenate([qn_ref[...]] * MLA_HEADS, axis=1)
        qbd_scr[...] = jnp.where(r % MLA_HEADS == c // MLA_NOPE, q_rep, 0.0).astype(jnp.bfloat16)

    gr = lax.broadcasted_iota(jnp.int32, (hw, nq), 0)
    gc = lax.broadcasted_iota(jnp.int32, (hw, nq), 1)
    gexp = jnp.where(gr // MLA_NOPE == gc % MLA_HEADS, 1.0 / MLA_NOPE, 0.0).astype(jnp.bfloat16)
    qr = qr_ref[...].astype(jnp.bfloat16)
    lat = jnp.concatenate([_mla_page_rows(p) for p in page_refs], axis=0)
    ckv, s = _mla_page_scores(lat, w_uk_ref, qbd_scr[...], gexp, qr)
    row0 = pl.multiple_of(j * n, n)
    ones_col = (lax.broadcasted_iota(jnp.int32, (1, LANES), 1) == 0).astype(jnp.bfloat16)
    ckv_scr[pl.ds(row0, n), :MLA_KV_LORA] = ckv
    ckv_scr[pl.ds(row0, n), MLA_KV_LORA:] = jnp.broadcast_to(ones_col, (n, LANES))
    s_scr[pl.ds(row0, n), :] = s

    @pl.when(j == nj - 1)
    def _():
        ckv_n, s_n = _mla_page_scores(latn_ref[...], w_uk_ref, qbd_scr[...], gexp, qr)
        kr = lax.broadcasted_iota(jnp.int32, (MLA_NEW_ROWS, nq), 0)
        kc = lax.broadcasted_iota(jnp.int32, (MLA_NEW_ROWS, nq), 1)
        s_scr[n_past:, :] = jnp.where(kr <= kc // MLA_HEADS, s_n, NEG)
        ckv_scr[n_past:, :MLA_KV_LORA] = ckv_n
        ckv_scr[n_past:, MLA_KV_LORA:] = jnp.broadcast_to(ones_col, (MLA_NEW_ROWS, LANES))
        sc = s_scr[...]
        p = jnp.exp(sc - jnp.max(sc, axis=0, keepdims=True)).astype(jnp.bfloat16)
        acc = _dot_tn(p, ckv_scr[...])
        o_lat = (acc[:, :MLA_KV_LORA] / acc[:, MLA_KV_LORA:MLA_KV_LORA + 1]).astype(jnp.bfloat16)
        z = jnp.dot(o_lat, w_uv_ref[...], preferred_element_type=jnp.float32)
        zr = lax.broadcasted_iota(jnp.int32, (MLA_HEADS, MLA_HEADS * MLA_VD), 0)
        zc = lax.broadcasted_iota(jnp.int32, (MLA_HEADS, MLA_HEADS * MLA_VD), 1)
        keep = zr == zc // MLA_VD
        o_ref[...] = jnp.concatenate(
            [jnp.sum(jnp.where(keep, z[t * MLA_HEADS:(t + 1) * MLA_HEADS], 0.0), axis=0, keepdims=True)
             for t in range(DEC_SEQ)], axis=0)


def mla_sample_attn(pool, li, page_table, qn, qr, latent, w_uk, w_uv):
    b, n_pages = page_table.shape
    pp = MLA_PAGES_PER_STEP
    assert n_pages % pp == 0 and latent.shape[1] == DEC_SEQ
    nq = DEC_SEQ * MLA_HEADS
    n_rows = n_pages * PAGE_SIZE + MLA_NEW_ROWS
    latn = jnp.pad(latent, ((0, 0), (0, MLA_NEW_ROWS - DEC_SEQ), (0, 0)))

    def page_spec(k):
        return pl.BlockSpec((None, None, MLA_LAT, PAGE_SIZE), lambda i, j, pt: (li, pt[i, j * pp + k], 0, 0))

    def whole(r, c):
        return pl.BlockSpec((r, c), lambda i, j, pt: (0, 0))

    def per_seq(r, c):
        return pl.BlockSpec((None, r, c), lambda i, j, pt: (i, 0, 0))

    grid_spec = pltpu.PrefetchScalarGridSpec(
        num_scalar_prefetch=1,
        grid=(b, n_pages // pp),
        in_specs=[page_spec(k) for k in range(pp)] + [
            per_seq(nq, MLA_NOPE), per_seq(nq, MLA_ROPE), per_seq(MLA_NEW_ROWS, MLA_LAT),
            whole(MLA_KV_LORA, MLA_HEADS * MLA_NOPE), whole(MLA_KV_LORA, MLA_HEADS * MLA_VD)],
        out_specs=per_seq(DEC_SEQ, MLA_HEADS * MLA_VD),
        scratch_shapes=[pltpu.VMEM((n_rows, MLA_KV_LORA + LANES), jnp.bfloat16),
                        pltpu.VMEM((n_rows, nq), jnp.float32),
                        pltpu.VMEM((nq, MLA_HEADS * MLA_NOPE), jnp.bfloat16)])
    return pl.pallas_call(
        _mla_sample_body,
        out_shape=jax.ShapeDtypeStruct((b, DEC_SEQ, MLA_HEADS * MLA_VD), jnp.float32),
        grid_spec=grid_spec,
        compiler_params=pltpu.CompilerParams(dimension_semantics=("arbitrary", "arbitrary"),
                                             vmem_limit_bytes=VMEM_LIMIT_BYTES),
        name="mla_sample_attn",
    )(page_table, *([pool] * pp), qn, qr, latn, w_uk, w_uv)


def mla_sample(zp, pool, li, page_table, g_qa, g_kva, w_uq, w_uk, w_uv, g_qk):
    b, t, _ = zp.shape
    q_nope, q_rope, latent = mla_project(zp, PAST_LEN + jnp.arange(t, dtype=jnp.float32), g_qa, g_kva, w_uq, g_qk)
    qn = (q_nope * g_qk[1, :MLA_NOPE]).reshape(b, t * MLA_HEADS, MLA_NOPE)
    qr = q_rope.reshape(b, t * MLA_HEADS, MLA_ROPE)
    o = mla_sample_attn(pool.transpose(0, 1, 3, 2), li, page_table, qn, qr, latent,
                        w_uk.reshape(MLA_KV_LORA, -1).astype(jnp.bfloat16),
                        w_uv.reshape(MLA_KV_LORA, -1).astype(jnp.bfloat16))
    return o, latent


def nsa_project(zp, b_gate, g_qk):
    b, t, _ = zp.shape
    qw, kvw = NSA_HEADS * NSA_HD, 6 * NSA_GROUPS * NSA_HD
    q, kv, gates = jnp.split(zp, [qw, qw + kvw], axis=-1)
    q = rmsnorm(q.reshape(b, t, NSA_GROUPS, NSA_HPG, NSA_HD), g_qk[0])
    kv = kv.reshape(b, t, 6, NSA_GROUPS, NSA_HD)
    rows = jnp.stack([kv[:, :, 0], kv[:, :, 1], rmsnorm(kv[:, :, 2], g_qk[1]), kv[:, :, 3]], axis=2)
    win = jnp.stack([rmsnorm(kv[:, :, 4], g_qk[1]), kv[:, :, 5]], axis=2)
    gates = jax.nn.sigmoid(gates + b_gate).reshape(b, t, NSA_GROUPS, NSA_HPG, 3)
    return q, rows, win, gates


def nsa_compress(kv_rows, w_cmp, g_k):
    b, l = kv_rows.shape[:2]
    blk = kv_rows.reshape(b, l // NSA_BLOCK, NSA_BLOCK, 2, NSA_GROUPS, NSA_HD)
    c = jnp.einsum('bnpcgd,cpde->bncge', blk, w_cmp)
    return rmsnorm(c[:, :, 0], g_k), c[:, :, 1]


def nsa_cmp_attn(q, k_c, v_c, qpos):
    nb = k_c.shape[1]
    s = jnp.einsum('btghd,bngd->bghtn', q, k_c) * NSA_SCALE
    mask = ((jnp.arange(nb) + 1) * NSA_BLOCK - 1)[None, :] <= qpos[:, None]
    p = masked_softmax(s, mask)
    o = jnp.einsum('bghtn,bngd->btghd', p.astype(v_c.dtype), v_c)
    return o, jnp.sum(p, axis=2)


def nsa_select(imp, qpos):
    nb = imp.shape[-1]
    cand = jnp.arange(nb)[None, :] < (qpos // NSA_BLOCK)[:, None]
    val, idx = lax.top_k(jnp.where(cand, imp, -jnp.inf), min(NSA_TOPK, nb))
    return idx, jnp.isfinite(val)


def nsa_sel_prompt(q, k_s, v_s, idx, valid):
    b, t = q.shape[:2]
    nb = t // NSA_BLOCK
    kb = k_s.reshape(b, nb, NSA_BLOCK, NSA_GROUPS, NSA_HD).transpose(0, 3, 1, 2, 4)
    vb = v_s.reshape(b, nb, NSA_BLOCK, NSA_GROUPS, NSA_HD).transpose(0, 3, 1, 2, 4)
    bi = jnp.arange(b)[:, None, None, None]
    gi = jnp.arange(NSA_GROUPS)[None, None, :, None]
    idx, valid = idx.transpose(0, 2, 1, 3), valid.transpose(0, 2, 1, 3)

    def block(i):
        t0 = i * NSA_QB
        qpos = t0 + jnp.arange(NSA_QB)
        qq = lax.dynamic_slice_in_dim(q, t0, NSA_QB, axis=1)
        cur = jnp.broadcast_to((qpos // NSA_BLOCK)[None, :, None, None], (b, NSA_QB, NSA_GROUPS, 1))
        ii = jnp.concatenate([lax.dynamic_slice_in_dim(idx, t0, NSA_QB, axis=1), cur], axis=-1)
        ok = jnp.concatenate([lax.dynamic_slice_in_dim(valid, t0, NSA_QB, axis=1), jnp.ones(cur.shape, bool)], axis=-1)
        kpos = ii[..., None] * NSA_BLOCK + jnp.arange(NSA_BLOCK)
        mask = ok[..., None] & (kpos <= qpos[None, :, None, None, None])
        n = ii.shape[-1] * NSA_BLOCK
        ks, vs = kb[bi, gi, ii], vb[bi, gi, ii]
        s = jnp.einsum('btghd,btgkpd->bghtkp', qq, ks).reshape(b, NSA_GROUPS, NSA_HPG, NSA_QB, n) * NSA_SCALE
        mask = mask.transpose(0, 2, 1, 3, 4).reshape(b, NSA_GROUPS, 1, NSA_QB, n)
        p = masked_softmax(s, mask)
        return jnp.einsum('bghtn,btgnd->btghd', p.astype(vs.dtype), vs.reshape(b, NSA_QB, NSA_GROUPS, n, NSA_HD))

    return jnp.moveaxis(lax.map(block, jnp.arange(t // NSA_QB)), 0, 1).reshape(q.shape)


def nsa_sel_sample(q, k_new, v_new, idx, valid, pool, li, page_table):
    b, t = q.shape[:2]
    bpp = PAGE_SIZE // NSA_BLOCK
    kk = idx.shape[-1]
    n = kk * NSA_BLOCK
    ii, ok = idx.transpose(0, 2, 1, 3), valid.transpose(0, 2, 1, 3)
    phys = page_table[jnp.arange(b)[:, None, None, None], ii // bpp][..., None]
    rows = (ii % bpp)[..., None] * NSA_BLOCK + jnp.arange(NSA_BLOCK)
    gi = jnp.arange(NSA_GROUPS)[None, None, :, None, None]
    ks = pool[li, phys, rows, 2, gi].reshape(b, t, NSA_GROUPS, n, NSA_HD)
    vs = pool[li, phys, rows, 3, gi].reshape(b, t, NSA_GROUPS, n, NSA_HD)
    s = jnp.concatenate([jnp.einsum('btghd,btgnd->bghtn', q, ks), jnp.einsum('btghd,bsgd->bghts', q, k_new)], axis=-1)
    mask_past = jnp.repeat(ok, NSA_BLOCK, axis=-1).transpose(0, 2, 1, 3)[:, :, None]
    mask_new = jnp.broadcast_to(jnp.tril(jnp.ones((t, t), bool)), (b, NSA_GROUPS, 1, t, t))
    p = masked_softmax(s * NSA_SCALE, jnp.concatenate([mask_past, mask_new], axis=-1)).astype(v_new.dtype)
    return (jnp.einsum('bghtn,btgnd->btghd', p[..., :n], vs)
            + jnp.einsum('bghts,bsgd->btghd', p[..., n:], v_new))


def nsa_win_prompt(q, k_w, v_w):
    t = q.shape[1]
    pad = ((0, 0), (NSA_WINDOW, 0), (0, 0), (0, 0))
    kp, vp = jnp.pad(k_w, pad), jnp.pad(v_w, pad)
    span = NSA_WINDOW + Q_BLOCK

    def block(i):
        t0 = i * Q_BLOCK
        qq = lax.dynamic_slice_in_dim(q, t0, Q_BLOCK, axis=1)
        kk = lax.dynamic_slice_in_dim(kp, t0, span, axis=1)
        vv = lax.dynamic_slice_in_dim(vp, t0, span, axis=1)
        qpos = t0 + jnp.arange(Q_BLOCK)
        kpos = t0 - NSA_WINDOW + jnp.arange(span)
        mask = ((kpos[None, :] <= qpos[:, None]) & (kpos[None, :] > qpos[:, None] - NSA_WINDOW)
                & (kpos[None, :] >= 0))
        p = masked_softmax(jnp.einsum('bqghd,bkgd->bghqk', qq, kk) * NSA_SCALE, mask)
        return jnp.einsum('bghqk,bkgd->bqghd', p.astype(vv.dtype), vv)

    return jnp.moveaxis(lax.map(block, jnp.arange(t // Q_BLOCK)), 0, 1).reshape(q.shape)


def nsa_win_sample(q, win, buf):
    t = q.shape[1]
    wb = buf.shape[1]
    kv = jnp.concatenate([buf.astype(win.dtype), win], axis=1)
    qpos = PAST_LEN + jnp.arange(t)
    kpos = PAST_LEN - wb + jnp.arange(wb + t)
    mask = (kpos[None, :] <= qpos[:, None]) & (kpos[None, :] > qpos[:, None] - NSA_WINDOW)
    p = masked_softmax(jnp.einsum('btghd,bsgd->bghts', q, kv[:, :, 0]) * NSA_SCALE, mask)
    o = jnp.einsum('bghts,bsgd->btghd', p.astype(kv.dtype), kv[:, :, 1])
    return o, kv[:, -wb:]


def nsa_merge(gates, o_cmp, o_sel, o_win):
    o = gates[..., 0:1] * o_cmp + gates[..., 1:2] * o_sel + gates[..., 2:3] * o_win
    return o.reshape(o.shape[0], o.shape[1], NSA_HEADS * NSA_HD)


def _nsa_body(q_ref, rows_ref, win_ref, kc_ref, vc_ref, gates_ref, o_ref, kv_scr, m_scr, acc_scr):
    qi = pl.program_id(1)
    nb = kc_ref.shape[0]
    gw = NSA_GROUPS * NSA_HD

    @pl.when(qi == 0)
    def _():
        lane = lax.broadcasted_iota(jnp.int32, (1, LANES), 1)
        one_col = (lane == NSA_HD).astype(jnp.float32)
        srcs = (rows_ref[:, 2 * gw:3 * gw], rows_ref[:, 3 * gw:4 * gw], win_ref[:, 0:gw], win_ref[:, gw:2 * gw])
        for i, x in enumerate(srcs):
            for g in range(NSA_GROUPS):
                xg = x if g == 0 else pltpu.roll(x, NSA_HD, 1)
                kv_scr[i * NSA_GROUPS + g] = jnp.where(lane < NSA_HD, xg, one_col if i % 2 else 0.0).astype(jnp.bfloat16)

    t0 = qi * NSA_TQ
    tpos = t0 + lax.broadcasted_iota(jnp.int32, (NSA_TQ, 1), 0)
    blk_t = tpos // NSA_BLOCK
    nidx = lax.broadcasted_iota(jnp.int32, (1, nb), 1)
    rows = NSA_HPG * NSA_TQ
    gq = NSA_HPG * NSA_HD
    zpad = jnp.zeros((NSA_TQ, LANES - NSA_HD), jnp.float32)
    for g in range(NSA_GROUPS):
        lanes = slice(g * NSA_HD, (g + 1) * NSA_HD)
        qg = q_ref[:, g * gq:(g + 1) * gq] * NSA_SCALE
        q_all = jnp.concatenate([jnp.concatenate([qg[:, h * NSA_HD:(h + 1) * NSA_HD], zpad], axis=1)
                                 for h in range(NSA_HPG)], axis=0).astype(jnp.bfloat16)

        kc = jnp.concatenate([kc_ref[:, lanes], jnp.zeros((nb, LANES - NSA_HD), jnp.float32)],
                             axis=1).astype(jnp.bfloat16)
        vc = vc_ref[:, lanes].astype(jnp.bfloat16)
        cmask = ((nidx + 1) * NSA_BLOCK - 1) <= tpos
        s = jnp.where(cmask[None], _dot_nt(q_all, kc).reshape(NSA_HPG, NSA_TQ, nb), NEG)
        e = jnp.exp(s - jnp.max(s, axis=-1, keepdims=True))
        p = e / jnp.sum(e, axis=-1, keepdims=True) * cmask[None].astype(jnp.float32)
        imp = jnp.sum(p, axis=0)
        o_cmp = jnp.dot(p.reshape(rows, nb).astype(jnp.bfloat16), vc, preferred_element_type=jnp.float32)

        rank = jnp.zeros((NSA_TQ, nb), jnp.int32)
        for m in range(nb):
            col = imp[:, m:m + 1]
            beats = (col > imp) | ((col == imp) & (m < nidx))
            rank = rank + jnp.where(beats & (m < blk_t), 1, 0)
        sel_f = (((nidx < blk_t) & (rank < NSA_TOPK)) | (nidx == blk_t)).astype(jnp.float32)
        blk_id = lax.broadcasted_iota(jnp.int32, (nb, FLASH_TK), 0)

        def sel_mask(k0):
            spos = k0 + lax.broadcasted_iota(jnp.int32, (1, FLASH_TK), 1)
            expand = (blk_id == spos // NSA_BLOCK).astype(jnp.float32)
            chosen = jnp.dot(sel_f, expand, preferred_element_type=jnp.float32) > 0.5
            return chosen & (spos <= tpos)

        def win_mask(k0):
            spos = k0 + lax.broadcasted_iota(jnp.int32, (1, FLASH_TK), 1)
            return (spos <= tpos) & (spos > tpos - NSA_WINDOW)

        kt_hi = (t0 + NSA_TQ + FLASH_TK - 1) // FLASH_TK
        _flash_init(m_scr, acc_scr)
        _flash_pass(q_all, kv_scr.at[g], kv_scr.at[NSA_GROUPS + g], 0, kt_hi, sel_mask, NSA_HPG, m_scr, acc_scr)
        o_sel = _flash_result(acc_scr, NSA_HD)
        kt_lo = jnp.maximum(t0 - (NSA_WINDOW - 1), 0) // FLASH_TK
        _flash_init(m_scr, acc_scr)
        _flash_pass(q_all, kv_scr.at[2 * NSA_GROUPS + g], kv_scr.at[3 * NSA_GROUPS + g], kt_lo, kt_hi, win_mask,
                    NSA_HPG, m_scr, acc_scr)
        o_win = _flash_result(acc_scr, NSA_HD)

        outs = []
        for h in range(NSA_HPG):
            r = slice(h * NSA_TQ, (h + 1) * NSA_TQ)
            c = (g * NSA_HPG + h) * 3
            outs.append(gates_ref[:, c:c + 1] * o_cmp[r] + gates_ref[:, c + 1:c + 2] * o_sel[r]
                        + gates_ref[:, c + 2:c + 3] * o_win[r])
        o_ref[:, g * gq:(g + 1) * gq] = jnp.concatenate(outs, axis=1)


def nsa_prompt_attn(q, rows, win, k_c, v_c, gates):
    b, t, _ = q.shape
    nb = t // NSA_BLOCK
    assert t % FLASH_TK == 0 and t % NSA_TQ == 0
    gw = NSA_GROUPS * NSA_HD
    qw = NSA_HEADS * NSA_HD
    return pl.pallas_call(
        _nsa_body,
        out_shape=jax.ShapeDtypeStruct((b, t, qw), jnp.float32),
        grid=(b, t // NSA_TQ),
        in_specs=[pl.BlockSpec((None, NSA_TQ, qw), lambda i, j: (i, j, 0)),
                  pl.BlockSpec((None, t, 4 * gw), lambda i, j: (i, 0, 0)),
                  pl.BlockSpec((None, t, 2 * gw), lambda i, j: (i, 0, 0)),
                  pl.BlockSpec((None, nb, gw), lambda i, j: (i, 0, 0)),
                  pl.BlockSpec((None, nb, gw), lambda i, j: (i, 0, 0)),
                  pl.BlockSpec((None, NSA_TQ, NSA_HEADS * 3), lambda i, j: (i, j, 0))],
        out_specs=pl.BlockSpec((None, NSA_TQ, qw), lambda i, j: (i, j, 0)),
        scratch_shapes=[pltpu.VMEM((4 * NSA_GROUPS, t, LANES), jnp.bfloat16),
                        pltpu.VMEM((NSA_HPG * NSA_TQ, LANES), jnp.float32),
                        pltpu.VMEM((NSA_HPG * NSA_TQ, LANES), jnp.float32)],
        compiler_params=pltpu.CompilerParams(dimension_semantics=("arbitrary", "arbitrary"),
                                             vmem_limit_bytes=VMEM_LIMIT_BYTES),
        name="nsa_prompt_attn",
    )(q, rows, win, k_c, v_c, gates)


def nsa_prompt(zp, b_gate, g_qk, w_cmp):
    b, t, _ = zp.shape
    q, rows, win, gates = nsa_project(zp, b_gate, g_qk)
    k_c, v_c = nsa_compress(rows[:, :, :2], w_cmp, g_qk[1])
    nb = t // NSA_BLOCK
    o = nsa_prompt_attn(q.reshape(b, t, -1), rows.reshape(b, t, -1), win.reshape(b, t, -1),
                        k_c.reshape(b, nb, -1), v_c.reshape(b, nb, -1), gates.reshape(b, t, -1))
    return o, rows, win[:, -min(NSA_WINDOW, t):]


def _masked_softmax_rows(s, mask):
    sm = jnp.where(mask, s, NEG)
    e = jnp.exp(sm - jnp.max(sm, axis=-1, keepdims=True))
    return e / jnp.sum(e, axis=-1, keepdims=True) * mask.astype(jnp.float32)


def _nsa_sample_body(pt_ref, *refs):
    n_pages = len(refs) - 15
    pages = refs[:n_pages]
    (q_ref, knew_ref, vnew_ref, win_ref, kwnew_ref, vwnew_ref, gates_ref, wcmp_ref, gk_ref,
     o_ref, cmp_scr, kv_scr, sel_scr, ocmp_scr, s_scr) = refs[n_pages:]
    ph = pl.program_id(1)
    gd = NSA_GROUPS * NSA_HD
    half = 2 * gd
    rows = DEC_SEQ * NSA_HPG
    n_past = n_pages * PAGE_SIZE
    trow = lax.broadcasted_iota(jnp.int32, (rows, 1), 0) // NSA_HPG
    lane = lax.broadcasted_iota(jnp.int32, (1, LANES), 1)
    own_mask = (lane <= trow) & (lane < DEC_SEQ)

    @pl.when(ph == 0)
    def _():
        for p in range(n_pages):
            cmp_scr[p * half:(p + 1) * half, :] = pages[p][...]
        for c in range(2):
            acc = jnp.zeros((NSA_GROUPS * n_pages, LANES), jnp.float32)
            for d in range(NSA_HD):
                lhs = jnp.concatenate([cmp_scr[pl.ds((c * NSA_GROUPS + g) * NSA_HD + d, n_pages, stride=half), :]
                                       for g in range(NSA_GROUPS)], axis=0)
                acc = acc + jnp.dot(lhs.astype(jnp.bfloat16), wcmp_ref[c, d], preferred_element_type=jnp.float32)
            kv_scr[c] = acc
        blk_t = (PAST_LEN + lax.broadcasted_iota(jnp.int32, (DEC_SEQ, 1), 0)) // NSA_BLOCK
        blk = 2 * (lane % n_pages) + lane // n_pages
        cmask = ((blk + 1) * NSA_BLOCK - 1) <= (PAST_LEN + trow)
        for g in range(NSA_GROUPS):
            qg = (q_ref[g] * NSA_SCALE).astype(jnp.bfloat16)
            kc = kv_scr[0, g * n_pages:(g + 1) * n_pages, :]
            vc = kv_scr[1, g * n_pages:(g + 1) * n_pages, :]
            s_parts = []
            for n in range(2):
                x = kc[:, n * NSA_HD:(n + 1) * NSA_HD]
                kn = x * lax.rsqrt(jnp.mean(x * x, axis=-1, keepdims=True) + RMS_EPS) * gk_ref[...]
                s_parts.append(_dot_nt(qg, kn.astype(jnp.bfloat16)))
            p = _masked_softmax_rows(jnp.concatenate(s_parts, axis=1), cmask)
            imp = jnp.sum(p.reshape(DEC_SEQ, NSA_HPG, LANES), axis=1)
            pb = p.astype(jnp.bfloat16)
            ocmp_scr[g] = (
                jnp.dot(pb[:, :n_pages], vc[:, :NSA_HD].astype(jnp.bfloat16), preferred_element_type=jnp.float32)
                + jnp.dot(pb[:, n_pages:], vc[:, NSA_HD:].astype(jnp.bfloat16), preferred_element_type=jnp.float32))
            rank = jnp.zeros((DEC_SEQ, LANES), jnp.int32)
            for m in range(2 * n_pages):
                bm = 2 * (m % n_pages) + m // n_pages
                col = imp[:, m:m + 1]
                beats = (col > imp) | ((col == imp) & (bm < blk))
                rank = rank + jnp.where(beats & (bm < blk_t), 1, 0)
            sel_scr[g] = ((blk < blk_t) & (rank < NSA_TOPK)).astype(jnp.float32)

    @pl.when(ph == 1)
    def _():
        span = NSA_SEL_PAGES * PAGE_SIZE
        for g in range(NSA_GROUPS):
            qg = (q_ref[g] * NSA_SCALE).astype(jnp.bfloat16)
            sel_rows = jnp.broadcast_to(sel_scr[g][:, None, :], (DEC_SEQ, NSA_HPG, LANES)).reshape(rows, LANES)
            krows = slice(g * NSA_HD, (g + 1) * NSA_HD)
            vrows = slice(gd + g * NSA_HD, gd + (g + 1) * NSA_HD)
            for ch in range(n_pages // NSA_SEL_PAGES):
                ps = range(ch * NSA_SEL_PAGES, (ch + 1) * NSA_SEL_PAGES)
                k_t = jnp.concatenate([pages[p][krows, :] for p in ps], axis=1).astype(jnp.bfloat16)
                s = jnp.dot(qg, k_t, preferred_element_type=jnp.float32)
                chosen = jnp.concatenate(
                    [jnp.where(lane < NSA_BLOCK, sel_rows[:, p:p + 1], sel_rows[:, n_pages + p:n_pages + p + 1])
                     for p in ps], axis=1) > 0.5
                s_scr[:, ch * span:(ch + 1) * span] = jnp.where(chosen, s, NEG)
            s_new = jnp.dot(qg, knew_ref[g].astype(jnp.bfloat16), preferred_element_type=jnp.float32)
            s_scr[:, n_past:] = jnp.where(own_mask, s_new, NEG)
            sa = s_scr[...]
            e = jnp.exp(sa - jnp.max(sa, axis=-1, keepdims=True))
            pb = (e / jnp.sum(e, axis=-1, keepdims=True)).astype(jnp.bfloat16)
            o_sel = jnp.dot(pb[:, n_past:], vnew_ref[g].astype(jnp.bfloat16), preferred_element_type=jnp.float32)
            for ch in range(n_pages // NSA_SEL_PAGES):
                ps = range(ch * NSA_SEL_PAGES, (ch + 1) * NSA_SEL_PAGES)
                v_t = jnp.concatenate([pages[p][vrows, :] for p in ps], axis=1).astype(jnp.bfloat16)
                o_sel = o_sel + _dot_nt(pb[:, ch * span:(ch + 1) * span], v_t)

            wb = win_ref.shape[1]
            wpos = PAST_LEN - wb + lax.broadcasted_iota(jnp.int32, (1, wb), 1)
            sw = jnp.dot(qg, win_ref[krows, :].astype(jnp.bfloat16), preferred_element_type=jnp.float32)
            sw_new = jnp.dot(qg, kwnew_ref[g].astype(jnp.bfloat16), preferred_element_type=jnp.float32)
            wmask = jnp.concatenate([wpos > (PAST_LEN + trow - NSA_WINDOW), jnp.broadcast_to(own_mask, (rows, LANES))],
                                    axis=1)
            pw = _masked_softmax_rows(jnp.concatenate([sw, sw_new], axis=1), wmask).astype(jnp.bfloat16)
            o_win = (_dot_nt(pw[:, :wb], win_ref[vrows, :].astype(jnp.bfloat16))
                     + jnp.dot(pw[:, wb:], vwnew_ref[g].astype(jnp.bfloat16), preferred_element_type=jnp.float32))
            gt = gates_ref[g]
            o_ref[g] = gt[:, 0:1] * ocmp_scr[g] + gt[:, 1:2] * o_sel + gt[:, 2:3] * o_win


def nsa_sample_attn(pool_t, li, page_table, q, k_new, v_new, win_t, kw_new, vw_new, gates, w_cmp_t, g_k):
    b, n_pages = page_table.shape
    assert n_pages % NSA_SEL_PAGES == 0 and 2 * n_pages == LANES and PAGE_SIZE == 2 * NSA_BLOCK
    gd = NSA_GROUPS * NSA_HD
    rows = DEC_SEQ * NSA_HPG
    wb = win_t.shape[-1]

    def page_spec(p):
        return pl.BlockSpec((None, None, 2 * gd, PAGE_SIZE), lambda i, ph, pt: (li, pt[i, p], ph, 0))

    def per_seq(*shape):
        return pl.BlockSpec((None,) + shape, lambda i, ph, pt: (i,) + (0,) * len(shape))

    grid_spec = pltpu.PrefetchScalarGridSpec(
        num_scalar_prefetch=1,
        grid=(b, 2),
        in_specs=[page_spec(p) for p in range(n_pages)] + [
            per_seq(NSA_GROUPS, rows, NSA_HD), per_seq(NSA_GROUPS, NSA_HD, LANES), per_seq(NSA_GROUPS, LANES, NSA_HD),
            per_seq(2 * gd, wb), per_seq(NSA_GROUPS, NSA_HD, LANES), per_seq(NSA_GROUPS, LANES, NSA_HD),
            per_seq(NSA_GROUPS, rows, 3),
            pl.BlockSpec((2, NSA_HD, LANES, LANES), lambda i, ph, pt: (0, 0, 0, 0)),
            pl.BlockSpec((1, NSA_HD), lambda i, ph, pt: (0, 0))],
        out_specs=per_seq(NSA_GROUPS, rows, NSA_HD),
        scratch_shapes=[pltpu.VMEM((n_pages * 2 * gd, PAGE_SIZE), jnp.float32),
                        pltpu.VMEM((2, NSA_GROUPS * n_pages, LANES), jnp.float32),
                        pltpu.VMEM((NSA_GROUPS, DEC_SEQ, LANES), jnp.float32),
                        pltpu.VMEM((NSA_GROUPS, rows, NSA_HD), jnp.float32),
                        pltpu.VMEM((rows, n_pages * PAGE_SIZE + LANES), jnp.float32)])
    return pl.pallas_call(
        _nsa_sample_body,
        out_shape=jax.ShapeDtypeStruct((b, NSA_GROUPS, rows, NSA_HD), jnp.float32),
        grid_spec=grid_spec,
        compiler_params=pltpu.CompilerParams(dimension_semantics=("arbitrary", "arbitrary"),
                                             vmem_limit_bytes=VMEM_LIMIT_BYTES),
        name="nsa_sample_attn",
    )(page_table, *([pool_t] * n_pages), q, k_new, v_new, win_t, kw_new, vw_new, gates, w_cmp_t, g_k)


def nsa_cmp_block_weights(w_cmp):
    w = w_cmp.transpose(0, 2, 1, 3)
    z = jnp.zeros_like(w)
    return jnp.concatenate([jnp.concatenate([w, z], axis=-1), jnp.concatenate([z, w], axis=-1)],
                           axis=-2).astype(jnp.bfloat16)


def nsa_sample(zp, pool, li, page_table, win_buf, b_gate, g_qk, w_cmp):
    b, t, _ = zp.shape
    q, rows, win, gates = nsa_project(zp, b_gate, g_qk)
    pool_t = pool.transpose(0, 1, 3, 4, 5, 2).reshape(pool.shape[0], pool.shape[1], 4 * NSA_GROUPS * NSA_HD, PAGE_SIZE)
    win_t = win_buf.transpose(0, 2, 3, 4, 1).reshape(b, 2 * NSA_GROUPS * NSA_HD, win_buf.shape[1])

    def keys_t(x):
        return jnp.pad(x.transpose(0, 2, 3, 1), ((0, 0), (0, 0), (0, 0), (0, LANES - t)))

    def vals(x):
        return jnp.pad(x.transpose(0, 2, 1, 3), ((0, 0), (0, 0), (0, LANES - t), (0, 0)))

    o = nsa_sample_attn(pool_t, li, page_table,
                        q.transpose(0, 2, 1, 3, 4).reshape(b, NSA_GROUPS, t * NSA_HPG, NSA_HD),
                        keys_t(rows[:, :, 2]), vals(rows[:, :, 3]), win_t, keys_t(win[:, :, 0]), vals(win[:, :, 1]),
                        gates.transpose(0, 2, 1, 3, 4).reshape(b, NSA_GROUPS, t * NSA_HPG, 3),
                        nsa_cmp_block_weights(w_cmp), g_qk[1].reshape(1, NSA_HD))
    o = o.reshape(b, NSA_GROUPS, t, NSA_HPG, NSA_HD).transpose(0, 2, 1, 3, 4).reshape(b, t, NSA_HEADS * NSA_HD)
    new_buf = jnp.concatenate([win_buf.astype(win.dtype), win], axis=1)[:, -win_buf.shape[1]:]
    return o, rows, new_buf


def kernel(x_prompt, x_sample, state_mlstm_C, state_mlstm_n, state_mlstm_m, cache_mla, cache_nsa,
           state_nsa_win, state_gdn_S, state_gdn_conv, page_table, c_prompt, c_sample,
           w_ada, b_ada, g_norm, w_ffn_up, w_ffn_down,
           w_ml_in, b_ml_gate, g_ml_hn, w_ml_out,
           w_mla_in, g_mla_qa, g_mla_kva, w_mla_uq, w_mla_uk, w_mla_uv, g_mla_qk, w_mla_out,
           w_nsa_in, b_nsa_gate, g_nsa_qk, w_nsa_cmp, w_nsa_out,
           w_gdn_in, w_gdn_conv, a_gdn_log, b_gdn_dt, g_gdn_out, w_gdn_out):
    bp = x_prompt.shape[0]
    xp, xs = x_prompt, x_sample
    ml_p, ml_s, mla_p, mla_s, nsa_p, nsa_s, gdn_p, gdn_s = [], [], [], [], [], [], [], []
    w_up_bf = w_ffn_up.astype(jnp.bfloat16)
    w_dn_bf = w_ffn_down.astype(jnp.bfloat16)
    for i in range(DEPTH):
        kind, li = i % N_MIXERS, i // N_MIXERS
        mod_p = adaln(c_prompt, w_ada[i], b_ada[i])
        mod_s = adaln(c_sample, w_ada[i], b_ada[i])
        xp = ffn_half_prompt(xp, g_norm[i, 0], mod_p, 0, w_up_bf[i, 0], w_dn_bf[i, 0])
        xs = ffn_half_sample(xs, g_norm[i, 0], mod_s, 0, w_up_bf[i, 0], w_dn_bf[i, 0])
        w_in, w_out = ((w_ml_in, w_ml_out), (w_mla_in, w_mla_out), (w_nsa_in, w_nsa_out), (w_gdn_in, w_gdn_out))[kind]
        zp = mod_proj(xp, g_norm[i, 1], mod_p, 1, w_in[li])
        zs = mod_proj(xs, g_norm[i, 1], mod_s, 1, w_in[li])
        if kind == 0:
            wts = (b_ml_gate[li], g_ml_hn[li])
            op, c_p, n_p, m_p = mlstm_mixer(zp, *wts, jnp.zeros((bp, ML_HEADS, ML_DQK, ML_DV), jnp.float32),
                                            jnp.zeros((bp, ML_HEADS, ML_DQK), jnp.float32),
                                            jnp.zeros((bp, ML_HEADS), jnp.float32))
            os_, c_s, n_s, m_s = mlstm_mixer(zs, *wts, state_mlstm_C[li], state_mlstm_n[li], state_mlstm_m[li])
            ml_p.append((c_p, n_p, m_p))
            ml_s.append((c_s, n_s, m_s))
        elif kind == 1:
            wts = (g_mla_qa[li], g_mla_kva[li], w_mla_uq[li], w_mla_uk[li], w_mla_uv[li], g_mla_qk[li])
            op, r_p = mla_prompt(zp, *wts)
            os_, r_s = mla_sample(zs, cache_mla, li, page_table, *wts)
            mla_p.append((r_p,))
            mla_s.append((r_s,))
        elif kind == 2:
            wts = (b_nsa_gate[li], g_nsa_qk[li], w_nsa_cmp[li])
            op, r_p, wb_p = nsa_prompt(zp, *wts)
            os_, r_s, wb_s = nsa_sample(zs, cache_nsa, li, page_table, state_nsa_win[li], *wts)
            nsa_p.append((r_p, wb_p))
            nsa_s.append((r_s, wb_s))
        else:
            wts = (w_gdn_conv[li], a_gdn_log[li], b_gdn_dt[li], g_gdn_out[li])
            op, s_p, cv_p = gdn_mixer(zp, *wts, jnp.zeros((bp, GDN_HEADS, GDN_DK, GDN_DV), jnp.float32),
                                      jnp.zeros((bp, GDN_CONV - 1, GDN_CONV_DIM), jnp.float32))
            os_, s_s, cv_s = gdn_mixer(zs, *wts, state_gdn_S[li], state_gdn_conv[li])
            gdn_p.append((s_p, cv_p))
            gdn_s.append((s_s, cv_s))
        xp = proj_residual(xp, op, mod_p[:, 1, 2], w_out[li])
        xs = proj_residual(xs, os_, mod_s[:, 1, 2], w_out[li])
        xp = ffn_half_prompt(xp, g_norm[i, 2], mod_p, 2, w_up_bf[i, 1], w_dn_bf[i, 1])
        xs = ffn_half_sample(xs, g_norm[i, 2], mod_s, 2, w_up_bf[i, 1], w_dn_bf[i, 1])
    return (xp, xs,
            _stack(ml_p, 0), _stack(ml_s, 0), _stack(ml_p, 1), _stack(ml_s, 1), _stack(ml_p, 2), _stack(ml_s, 2),
            _stack(mla_p, 0), _stack(mla_s, 0),
            _stack(nsa_p, 0), _stack(nsa_s, 0), _stack(nsa_p, 1), _stack(nsa_s, 1),
            _stack(gdn_p, 0), _stack(gdn_s, 0), _stack(gdn_p, 1), _stack(gdn_s, 1))
```
